```python
import math
import jax, jax.numpy as jnp
from jax import lax
import numpy as np

D_MODEL = 2048
BATCH = 32
SEQ = 256
DEPTH = 4
DEC_BATCH = 8
DEC_SEQ = 2048
PAST_LEN = 512

GRID_W = 64
N_EVEN = (DEPTH + 1) // 2
N_ODD = DEPTH // 2
ALPHA = (2 * DEPTH) ** 0.25
BETA = (8 * DEPTH) ** -0.25
LN_EPS = 1e-5
D_A = D_MODEL // 2
A_BLOCKS = 8
A_BW = D_A // A_BLOCKS
CONV_W = 4
LRU_C = 8.0
D_B = D_MODEL // 2
B_HEADS = 4
B_HD = D_B // B_HEADS
MLSTM_CHUNK = 64
EVEN_COLS = 2 * D_A + 4 * D_B + 4 * B_HEADS
C_HEADS = 4
DK_C = D_MODEL // 2
DV_C = D_MODEL
C_HK = DK_C // C_HEADS
C_HV = DV_C // C_HEADS
GLA_RANK = 16
GLA_TAU = 16.0
GLA_CHUNK = 32
ODD_COLS = 2 * DK_C + 2 * DV_C + 2 * GLA_RANK
N_EXPERTS = 16
N_GROUPS = 4
EXPERTS_PER_GROUP = N_EXPERTS // N_GROUPS
TOP_K = 2
D_FF = 1024

kernel_name = "hybrid_lru_mlstm_gla_moe_diffusion_step"


def layer_norm(x, g, b):
    xf = x.astype(jnp.float32)
    mu = xf.mean(-1, keepdims=True)
    var = jnp.mean(jnp.square(xf - mu), -1, keepdims=True)
    return ((xf - mu) * lax.rsqrt(var + LN_EPS) * g + b).astype(x.dtype)


def head_layernorm(x, g):
    xf = x.astype(jnp.float32)
    mu = xf.mean(-1, keepdims=True)
    var = jnp.mean(jnp.square(xf - mu), -1, keepdims=True)
    y = (xf - mu) * lax.rsqrt(var + LN_EPS)
    return y.reshape(*x.shape[:-2], -1) * g


def head_rmsnorm(x, g):
    xf = x.astype(jnp.float32)
    y = xf * lax.rsqrt(jnp.mean(jnp.square(xf), -1, keepdims=True) + LN_EPS)
    return y.reshape(*x.shape[:-2], -1) * g


def pos_embed_2d(n_tokens, dim):
    rows = n_tokens // GRID_W
    quarter = dim // 4
    freqs = jnp.exp(-math.log(10000.0) * jnp.arange(quarter, dtype=jnp.float32) / quarter)
    r = jnp.broadcast_to(jnp.arange(rows, dtype=jnp.float32)[:, None], (rows, GRID_W)).reshape(-1)
    col = jnp.broadcast_to(jnp.arange(GRID_W, dtype=jnp.float32)[None, :], (rows, GRID_W)).reshape(-1)
    ar = r[:, None] * freqs
    ac = col[:, None] * freqs
    return jnp.concatenate([jnp.sin(ar), jnp.cos(ar), jnp.sin(ac), jnp.cos(ac)], -1)


def conv_centred(x, w, b):
    pl = (CONV_W - 1) // 2
    pr = CONV_W - 1 - pl
    L = x.shape[1]
    xp = jnp.pad(x, ((0, 0), (pl, pr), (0, 0)))
    y = b
    for k in range(CONV_W):
        y = y + xp[:, k:k + L] * w[k]
    return y


def _lin_combine(e1, e2):
    a1, b1 = e1
    a2, b2 = e2
    return a1 * a2, a2 * b1 + b2


def rglru_dir(x, gate_w, gate_b, lam, h0):
    B, L, _ = x.shape
    xh = x.reshape(B, L, A_BLOCKS, A_BW)
    gates = jnp.einsum('blhi,hij->blhj', xh, gate_w).astype(jnp.float32) + gate_b
    r = jax.nn.sigmoid(gates[..., :A_BW]).reshape(B, L, D_A)
    i = jax.nn.sigmoid(gates[..., A_BW:]).reshape(B, L, D_A)
    log_a = -LRU_C * jax.nn.softplus(-lam.astype(jnp.float32)) * r
    a = jnp.exp(log_a)
    u = jnp.sqrt(-jnp.expm1(2.0 * log_a)) * i * x
    u = u.at[:, 0].add(a[:, 0] * h0.astype(jnp.float32))
    _, h = lax.associative_scan(_lin_combine, (a, u), axis=1)
    return h, h[:, -1]


def mlstm_chunked(q, k, v, i_pre, logf, C0, n0, m0):
    B, H, L, _ = q.shape
    dv = v.shape[-1]
    nc = L // MLSTM_CHUNK

    def chunks(t):
        return jnp.moveaxis(t.reshape(B, H, nc, MLSTM_CHUNK, *t.shape[3:]), 2, 0)

    tri = jnp.tril(jnp.ones((MLSTM_CHUNK, MLSTM_CHUNK), dtype=bool))

    def step(carry, inp):
        C, n, m = carry
        qc, kc, vc, ic, fc = inp
        b = jnp.cumsum(fc, axis=-1)
        dmat = jnp.where(tri, b[..., :, None] - b[..., None, :] + ic[..., None, :], -jnp.inf)
        m_inter = b + m[..., None]
        m_t = jnp.maximum(m_inter, dmat.max(-1))
        s = jnp.einsum('bhtd,bhsd->bhts', qc, kc) * jnp.exp(dmat - m_t[..., None])
        inter_scale = jnp.exp(m_inter - m_t)
        num = jnp.einsum('bhts,bhsv->bhtv', s, vc) + inter_scale[..., None] * jnp.einsum('bhtd,bhdv->bhtv', qc, C)
        den = s.sum(-1) + inter_scale * jnp.einsum('bhtd,bhd->bht', qc, n)
        h = num / jnp.maximum(jnp.abs(den), jnp.exp(-m_t))[..., None]
        bL = b[..., -1]
        g = bL[..., None] - b + ic
        m_new = jnp.maximum(bL + m, g.max(-1))
        wk = jnp.exp(g - m_new[..., None])
        decay = jnp.exp(bL + m - m_new)
        C_new = decay[..., None, None] * C + jnp.einsum('bhs,bhsd,bhsv->bhdv', wk, kc, vc)
        n_new = decay[..., None] * n + jnp.einsum('bhs,bhsd->bhd', wk, kc)
        return (C_new, n_new, m_new), h

    carry0 = (C0.astype(jnp.float32), n0.astype(jnp.float32), m0.astype(jnp.float32))
    (C, n, m), h = lax.scan(step, carry0, (chunks(q), chunks(k), chunks(v), chunks(i_pre), chunks(logf)))
    h = jnp.moveaxis(h, 0, 2).reshape(B, H, L, dv)
    return h, (C, n, m)


def gla_chunked(q, k, v, loga, S0):
    B, H, L, _ = q.shape
    dv = v.shape[-1]
    nc = L // GLA_CHUNK

    def chunks(t):
        return jnp.moveaxis(t.reshape(B, H, nc, GLA_CHUNK, t.shape[-1]), 2, 0)

    tri = jnp.tril(jnp.ones((GLA_CHUNK, GLA_CHUNK), dtype=bool))

    def step(S, inp):
        qc, kc, vc, ac = inp
        b = jnp.cumsum(ac, axis=2)
        inter = jnp.einsum('bhtd,bhdv->bhtv', qc * jnp.exp(b), S)
        diff = b[:, :, :, None, :] - b[:, :, None, :, :]
        decay = jnp.exp(jnp.where(tri[:, :, None], diff, -jnp.inf))
        att = jnp.einsum('bhtsd,bhsd->bhts', qc[:, :, :, None, :] * decay, kc)
        o = inter + jnp.einsum('bhts,bhsv->bhtv', att, vc)
        bL = b[:, :, -1:, :]
        S_new = jnp.exp(bL[:, :, 0])[..., None] * S + jnp.einsum('bhsd,bhsv->bhdv', kc * jnp.exp(bL - b), vc)
        return S_new, o

    S, o = lax.scan(step, S0.astype(jnp.float32), (chunks(q), chunks(k), chunks(v), chunks(loga)))
    o = jnp.moveaxis(o, 0, 2).reshape(B, H, L, dv)
    return o, S


def _rev(t):
    return jnp.flip(t, axis=2)


def even_mixer(h, w_in, w_out, conv_w, conv_b, gate_w, gate_b, lam, mg_b, mnorm_g, lru0, c0, n0, m0):
    B, L, _ = h.shape
    proj = h @ w_in
    a_gate, a_x, q, k, v, o, g_pre = jnp.split(
        proj, np.cumsum([D_A, D_A, D_B, D_B, D_B, D_B]).tolist(), axis=-1)
    xc = conv_centred(a_x, conv_w, conv_b)
    hf, sf = rglru_dir(xc, gate_w[0], gate_b[0], lam[0], lru0[:, 0])
    hb, sb = rglru_dir(xc[:, ::-1], gate_w[1], gate_b[1], lam[1], lru0[:, 1])
    y_a = jax.nn.gelu(a_gate) * (hf + hb[:, ::-1])
    def heads(t):
        return t.reshape(B, L, B_HEADS, B_HD).transpose(0, 2, 1, 3)
    qh, kh, vh = heads(q), heads(k) * (B_HD ** -0.5), heads(v)
    gts = (g_pre.astype(jnp.float32).reshape(B, L, 2, 2, B_HEADS) + mg_b).transpose(2, 3, 0, 4, 1)
    mf, stf = mlstm_chunked(qh, kh, vh, gts[0, 0], jax.nn.log_sigmoid(gts[0, 1]),
                            c0[:, 0], n0[:, 0], m0[:, 0])
    mb, stb = mlstm_chunked(_rev(qh), _rev(kh), _rev(vh), _rev(gts[1, 0]), _rev(jax.nn.log_sigmoid(gts[1, 1])),
                            c0[:, 1], n0[:, 1], m0[:, 1])
    hm = (mf + _rev(mb)).transpose(0, 2, 1, 3)
    y_b = jax.nn.sigmoid(o) * head_layernorm(hm, mnorm_g)
    y = jnp.concatenate([y_a.astype(h.dtype), y_b.astype(h.dtype)], -1) @ w_out
    return y, (sf, sb, stf, stb)


def odd_mixer(h, w_in, w_out, gate_w, gate_b, norm_g, s0):
    B, L, _ = h.shape
    proj = h @ w_in
    q, k, v, g, lr = jnp.split(
        proj, np.cumsum([DK_C, DK_C, DV_C, DV_C]).tolist(), axis=-1)
    def heads(t, d):
        return t.reshape(B, L, C_HEADS, d).transpose(0, 2, 1, 3)
    qh, kh, vh = heads(q, C_HK) * (C_HK ** -0.5), heads(k, C_HK), heads(v, C_HV)
    lr = lr.reshape(B, L, 2, GLA_RANK)
    loga = jax.nn.log_sigmoid(jnp.einsum('bldr,drk->dblk', lr, gate_w).astype(jnp.float32)
                              + gate_b[:, None, None, :]) / GLA_TAU
    of, sf = gla_chunked(qh, kh, vh, heads(loga[0], C_HK), s0[:, 0])
    ob, sb = gla_chunked(_rev(qh), _rev(kh), _rev(vh), _rev(heads(loga[1], C_HK)), s0[:, 1])
    oo = (of + _rev(ob)).transpose(0, 2, 1, 3)
    y = (head_rmsnorm(oo, norm_g) * jax.nn.silu(g)).astype(h.dtype) @ w_out
    return y, (sf, sb)


def moe(h, router_w, router_b, w_in, w_out):
    probs = jax.nn.softmax((h @ router_w).astype(jnp.float32) + router_b, axis=-1)
    grp = probs.reshape(*probs.shape[:-1], N_GROUPS, EXPERTS_PER_GROUP)
    gsel = jnp.argmax(lax.top_k(grp, TOP_K)[0].sum(-1), axis=-1)
    in_grp = (jnp.arange(N_EXPERTS) // EXPERTS_PER_GROUP) == gsel[..., None]
    tv, ti = lax.top_k(jnp.where(in_grp, probs, -1.0), TOP_K)
    tv = tv / tv.sum(-1, keepdims=True)
    gates = jnp.sum(jax.nn.one_hot(ti, N_EXPERTS, dtype=jnp.float32) * tv[..., None], axis=-2)
    out = jnp.zeros(h.shape, jnp.float32)
    for e in range(N_EXPERTS):
        u, w = jnp.split(h @ w_in[e], 2, axis=-1)
        out = out + gates[..., e:e + 1] * ((jax.nn.silu(u) * w) @ w_out[e])
    return out.astype(h.dtype)


def trunk(x, cond, lru0, mc0, mn0, mm0, gla0, P, collect):
    s_lru, s_c, s_n, s_m, s_gla = [], [], [], [], []
    for l in range(DEPTH):
        mod = (jax.nn.silu(cond) @ P['mod_w'][l] + P['mod_b'][l])[:, None, :]
        sh1, sc1, g1, sh2, sc2, g2 = jnp.split(mod, 6, axis=-1)
        hm = x * (1 + sc1) + sh1
        j = l // 2
        if l % 2 == 0:
            y, (sf, sb, stf, stb) = even_mixer(
                hm, P['even_w_in'][j], P['even_w_out'][j], P['lru_conv_w'][j], P['lru_conv_b'][j],
                P['lru_gate_w'][j], P['lru_gate_b'][j], P['lru_lambda'][j], P['mlstm_gate_b'][j],
                P['mlstm_norm_g'][j], lru0[:, j], mc0[:, j], mn0[:, j], mm0[:, j])
            if collect:
                s_lru.append(jnp.stack([sf, sb], 1))
                s_c.append(jnp.stack([stf[0], stb[0]], 1))
                s_n.append(jnp.stack([stf[1], stb[1]], 1))
                s_m.append(jnp.stack([stf[2], stb[2]], 1))
        else:
            y, (sf, sb) = odd_mixer(hm, P['odd_w_in'][j], P['odd_w_out'][j], P['gla_gate_w'][j],
                                    P['gla_gate_b'][j], P['gla_norm_g'][j], gla0[:, j])
            if collect:
                s_gla.append(jnp.stack([sf, sb], 1))
        x = layer_norm(ALPHA * x + g1 * y, P['ln_g'][l, 0], P['ln_b'][l, 0])
        hf = x * (1 + sc2) + sh2
        x = layer_norm(ALPHA * x + g2 * moe(hf, P['router_w'], P['router_b'], P['moe_w_in'][l], P['moe_w_out'][l]),
                       P['ln_g'][l, 1], P['ln_b'][l, 1])
    if collect:
        dt = x.dtype
        return x, (jnp.stack(s_lru, 1).astype(dt), jnp.stack(s_c, 1).astype(dt), jnp.stack(s_n, 1).astype(dt),
                   jnp.stack(s_m, 1).astype(dt), jnp.stack(s_gla, 1).astype(dt))
    return x, None


def setup_inputs(seed: int = 0) -> dict:
    key = jax.random.key(seed)
    ks = jax.random.split(key, 32)
    f32 = jnp.float32

    def nrm(k, shape, scale):
        return jax.random.normal(k, shape, f32) * scale

    u = jax.random.uniform(ks[19], (N_EVEN, 2, D_A), f32, 0.9, 0.999)
    s = u ** (1.0 / LRU_C)
    ig = nrm(ks[20], (N_EVEN, 2, 1, B_HEADS), 0.1)
    fg = jnp.linspace(3.0, 6.0, B_HEADS, dtype=f32) + nrm(ks[21], (N_EVEN, 2, 1, B_HEADS), 0.1)
    return {
        "x_prompt": nrm(ks[0], (BATCH, SEQ, D_MODEL), 1.0),
        "x_sample": nrm(ks[1], (DEC_BATCH, DEC_SEQ, D_MODEL), 1.0),
        "state_lru": nrm(ks[2], (DEC_BATCH, N_EVEN, 2, D_A), 0.5),
        "state_mlstm_c": nrm(ks[3], (DEC_BATCH, N_EVEN, 2, B_HEADS, B_HD, B_HD), 0.1),
        "state_mlstm_n": nrm(ks[4], (DEC_BATCH, N_EVEN, 2, B_HEADS, B_HD), 0.1),
        "state_mlstm_m": nrm(ks[5], (DEC_BATCH, N_EVEN, 2, B_HEADS), 1.0),
        "state_gla": nrm(ks[6], (DEC_BATCH, N_ODD, 2, C_HEADS, C_HK, C_HV), 0.1),
        "c": nrm(ks[7], (DEC_BATCH, D_MODEL), 1.0),
        "c_ctx": nrm(ks[8], (D_MODEL,), 1.0),
        "mod_w": nrm(ks[9], (DEPTH, D_MODEL, 6 * D_MODEL), 0.5 * D_MODEL ** -0.5),
        "mod_b": nrm(ks[10], (DEPTH, 6 * D_MODEL), 0.02),
        "ln_g": 1.0 + nrm(ks[11], (DEPTH, 2, D_MODEL), 0.02),
        "ln_b": nrm(ks[12], (DEPTH, 2, D_MODEL), 0.02),
        "even_w_in": nrm(ks[13], (N_EVEN, D_MODEL, EVEN_COLS), D_MODEL ** -0.5),
        "even_w_out": nrm(ks[14], (N_EVEN, D_A + D_B, D_MODEL), BETA * (D_A + D_B) ** -0.5),
        "lru_conv_w": nrm(ks[15], (N_EVEN, CONV_W, D_A), CONV_W ** -0.5),
        "lru_conv_b": nrm(ks[16], (N_EVEN, D_A), 0.02),
        "lru_gate_w": nrm(ks[17], (N_EVEN, 2, A_BLOCKS, A_BW, 2 * A_BW), A_BW ** -0.5),
        "lru_gate_b": nrm(ks[18], (N_EVEN, 2, A_BLOCKS, 2 * A_BW), 0.02),
        "lru_lambda": jnp.log(s) - jnp.log1p(-s),
        "mlstm_gate_b": jnp.concatenate([ig, fg], axis=2),
        "mlstm_norm_g": 1.0 + nrm(ks[22], (N_EVEN, D_B), 0.02),
        "odd_w_in": nrm(ks[23], (N_ODD, D_MODEL, ODD_COLS), D_MODEL ** -0.5),
        "odd_w_out": nrm(ks[24], (N_ODD, DV_C, D_MODEL), BETA * DV_C ** -0.5),
        "gla_gate_w": nrm(ks[25], (N_ODD, 2, GLA_RANK, DK_C), GLA_RANK ** -0.5),
        "gla_gate_b": nrm(ks[26], (N_ODD, 2, DK_C), 0.02),
        "gla_norm_g": 1.0 + nrm(ks[27], (N_ODD, DV_C), 0.02),
        "router_w": nrm(ks[28], (D_MODEL, N_EXPERTS), D_MODEL ** -0.5),
        "router_b": nrm(ks[29], (N_EXPERTS,), 0.01),
        "moe_w_in": nrm(ks[30], (DEPTH, N_EXPERTS, D_MODEL, 2 * D_FF), D_MODEL ** -0.5),
        "moe_w_out": nrm(ks[31], (DEPTH, N_EXPERTS, D_FF, D_MODEL), BETA * D_FF ** -0.5),
    }


def reference(x_prompt, x_sample, state_lru, state_mlstm_c, state_mlstm_n, state_mlstm_m, state_gla,
              c, c_ctx, mod_w, mod_b, ln_g, ln_b, even_w_in, even_w_out, lru_conv_w, lru_conv_b,
              lru_gate_w, lru_gate_b, lru_lambda, mlstm_gate_b, mlstm_norm_g, odd_w_in, odd_w_out,
              gla_gate_w, gla_gate_b, gla_norm_g, router_w, router_b, moe_w_in, moe_w_out):
    P = dict(mod_w=mod_w, mod_b=mod_b, ln_g=ln_g, ln_b=ln_b, even_w_in=even_w_in, even_w_out=even_w_out,
             lru_conv_w=lru_conv_w, lru_conv_b=lru_conv_b, lru_gate_w=lru_gate_w, lru_gate_b=lru_gate_b,
             lru_lambda=lru_lambda, mlstm_gate_b=mlstm_gate_b, mlstm_norm_g=mlstm_norm_g,
             odd_w_in=odd_w_in, odd_w_out=odd_w_out, gla_gate_w=gla_gate_w, gla_gate_b=gla_gate_b,
             gla_norm_g=gla_norm_g, router_w=router_w, router_b=router_b,
             moe_w_in=moe_w_in, moe_w_out=moe_w_out)
    f32 = jnp.float32
    bp = x_prompt.shape[0]
    y_prompt, new_state = trunk(
        x_prompt, c_ctx[None, :],
        jnp.zeros((bp, N_EVEN, 2, D_A), f32),
        jnp.zeros((bp, N_EVEN, 2, B_HEADS, B_HD, B_HD), f32),
        jnp.zeros((bp, N_EVEN, 2, B_HEADS, B_HD), f32),
        jnp.zeros((bp, N_EVEN, 2, B_HEADS), f32),
        jnp.zeros((bp, N_ODD, 2, C_HEADS, C_HK, C_HV), f32),
        P, True)
    new_lru, new_mc, new_mn, new_mm, new_gla = new_state
    xs = x_sample + pos_embed_2d(x_sample.shape[1], x_sample.shape[2]).astype(x_sample.dtype)[None]
    y_sample, _ = trunk(xs, c, state_lru, state_mlstm_c, state_mlstm_n, state_mlstm_m, state_gla, P, False)
    return (y_prompt, y_sample, new_lru, new_mc, new_mn, new_mm, new_gla)
```

```python
import functools
import math

import numpy as np
import jax
import jax.numpy as jnp
from jax import lax
from jax.experimental import pallas as pl
from jax.experimental.pallas import tpu as pltpu

F32 = jnp.float32
BF16 = jnp.bfloat16
HI = lax.Precision.HIGHEST

LN_EPS = 1e-5
LRU_C = 8.0
A_BLOCKS = 8
B_HEADS = 4
C_HEADS = 4
GLA_RANK = 16
GLA_TAU = 16.0
N_EXPERTS = 16
N_GROUPS = 4
GRID_W = 64
LANE = 128
SUBLANE = 8
SCAN_CHUNK = 128
MOE_TM = 512
VMEM_LIMIT = 56 * 1024 * 1024

NT_DIMS = (((1,), (1,)), ((), ()))
TN_DIMS = (((0,), (0,)), ((), ()))


def _cparams(n_axes):
    return pltpu.CompilerParams(dimension_semantics=("arbitrary",) * n_axes,
                                vmem_limit_bytes=VMEM_LIMIT)


def _sigmoid(x):
    return 1.0 / (1.0 + jnp.exp(-x))


def _silu(x):
    return x * _sigmoid(x)


def _log_sigmoid(x):
    return jnp.minimum(x, 0.0) - jnp.log1p(jnp.exp(-jnp.abs(x)))


def _gelu_tanh(x):
    return 0.5 * x * (1.0 + jnp.tanh(math.sqrt(2.0 / math.pi) * (x + 0.044715 * (x * x * x))))


def _mod_body(c_ref, w_ref, b_ref, o_ref):
    c = c_ref[...]
    o_ref[...] = jnp.dot(_silu(c), w_ref[...], precision=HI, preferred_element_type=F32) + b_ref[...]


def _modulation(cond, mod_w, mod_b):
    R, D = cond.shape
    nl, _, N = mod_w.shape
    tn = 1024
    return pl.pallas_call(
        _mod_body,
        grid=(nl, N // tn),
        in_specs=[pl.BlockSpec((R, D), lambda l, j: (0, 0)),
                  pl.BlockSpec((None, D, tn), lambda l, j: (l, 0, j)),
                  pl.BlockSpec((None, 1, tn), lambda l, j: (l, 0, j))],
        out_specs=pl.BlockSpec((None, R, tn), lambda l, j: (l, 0, j)),
        out_shape=jax.ShapeDtypeStruct((nl, R, N), F32),
        compiler_params=_cparams(2),
        name="modulation",
    )(cond, mod_w, mod_b.reshape(nl, 1, N))


def _embed_body(xp_ref, xs_ref, pos_ref, o_ref, *, n_prompt_tiles):
    i = pl.program_id(0)

    @pl.when(i < n_prompt_tiles)
    def _():
        o_ref[...] = xp_ref[...]

    @pl.when(i >= n_prompt_tiles)
    def _():
        o_ref[...] = xs_ref[...] + pos_ref[...]


def _embed(xp, xs, pos, tm):
    Tp, D = xp.shape
    Ts = xs.shape[0]
    Ls = pos.shape[0]
    npt, nst, npos = Tp // tm, Ts // tm, Ls // tm
    return pl.pallas_call(
        functools.partial(_embed_body, n_prompt_tiles=npt),
        grid=(npt + nst,),
        in_specs=[pl.BlockSpec((tm, D), lambda i: (jnp.minimum(i, npt - 1), 0)),
                  pl.BlockSpec((tm, D), lambda i: (jnp.maximum(i - npt, 0), 0)),
                  pl.BlockSpec((tm, D), lambda i: (jnp.maximum(i - npt, 0) % npos, 0))],
        out_specs=pl.BlockSpec((tm, D), lambda i: (i, 0)),
        out_shape=jax.ShapeDtypeStruct((Tp + Ts, D), F32),
        compiler_params=_cparams(1),
        name="embed",
    )(xp, xs, pos)


def _proj_body(x_ref, m_ref, w_ref, o_ref, h_ref, *, sh_row, sc_row, hp):
    @pl.when(pl.program_id(1) == 0)
    def _():
        h = x_ref[...] * (1.0 + m_ref[sc_row:sc_row + 1, :]) + m_ref[sh_row:sh_row + 1, :]
        h_ref[...] = h.astype(h_ref.dtype)

    if hp:
        o = jnp.dot(h_ref[...], w_ref[...], precision=HI, preferred_element_type=F32)
    else:
        o = jnp.dot(h_ref[...], w_ref[...], preferred_element_type=F32)
    o_ref[...] = o.astype(o_ref.dtype)


def _proj(x, modt, w, *, sh_row, sc_row, tm, tn, hp=False):
    T, D = x.shape
    N = w.shape[1]
    mod4, mod_map = modt
    return pl.pallas_call(
        functools.partial(_proj_body, sh_row=sh_row, sc_row=sc_row, hp=hp),
        grid=(T // tm, N // tn),
        in_specs=[pl.BlockSpec((tm, D), lambda i, j: (i, 0)),
                  pl.BlockSpec((None, None, 6, D), lambda i, j: mod_map(i)),
                  pl.BlockSpec((D, tn), lambda i, j: (0, j))],
        out_specs=pl.BlockSpec((tm, tn), lambda i, j: (i, j)),
        out_shape=jax.ShapeDtypeStruct((T, N), F32),
        scratch_shapes=[pltpu.VMEM((tm, D), F32 if hp else BF16)],
        compiler_params=_cparams(2),
        name="proj_hp" if hp else "proj",
    )(x, mod4, w)


def _lru_body(ag_ref, ax_ref, cw_ref, cb_ref, gw_ref, gb_ref, lam_ref, h0_ref, y_ref, st_ref,
              af_s, uf_s, ab_s, ub_s, hf_s, hb_s, *, L):
    x = ax_ref[...]
    row = lax.broadcasted_iota(jnp.int32, (L, LANE), 0)
    xm1 = jnp.where(row >= 1, pltpu.roll(x, 1, 0), 0.0)
    xp1 = jnp.where(row < L - 1, pltpu.roll(x, L - 1, 0), 0.0)
    xp2 = jnp.where(row < L - 2, pltpu.roll(x, L - 2, 0), 0.0)
    xc = cb_ref[...] + xm1 * cw_ref[0:1, :] + x * cw_ref[1:2, :] + xp1 * cw_ref[2:3, :] + xp2 * cw_ref[3:4, :]
    xcb = xc.astype(BF16)
    nj = L // SUBLANE
    sub = lax.broadcasted_iota(jnp.int32, (nj, SUBLANE, LANE), 1)

    for d, (a_s, u_s) in enumerate(((af_s, uf_s), (ab_s, ub_s))):
        g = jnp.dot(xcb, gw_ref[d], preferred_element_type=F32) + gb_ref[d]
        r = _sigmoid(g[:, :LANE])
        ig = _sigmoid(g[:, LANE:])
        nlam = -lam_ref[d]
        softplus = jnp.maximum(nlam, 0.0) + jnp.log1p(jnp.exp(-jnp.abs(nlam)))
        log_a = (-LRU_C * softplus) * r
        a = jnp.exp(log_a)
        u = jnp.sqrt(jnp.tanh(-log_a) * (1.0 + a * a)) * ig * xc
        a3 = a.reshape(nj, SUBLANE, LANE)
        u3 = u.reshape(nj, SUBLANE, LANE)
        for k in (1, 2, 4):
            if d == 0:
                sh, keep = k, sub >= k
            else:
                sh, keep = SUBLANE - k, sub < SUBLANE - k
            a_sh = pltpu.roll(a3, sh, 1)
            u_sh = pltpu.roll(u3, sh, 1)
            u3 = jnp.where(keep, a3 * u_sh + u3, u3)
            a3 = jnp.where(keep, a3 * a_sh, a3)
        a_s[...] = a3.reshape(L, LANE)
        u_s[...] = u3.reshape(L, LANE)

    def carry(j, hs):
        hf, hb = hs
        rf = pl.multiple_of(j * SUBLANE, SUBLANE)
        rb = pl.multiple_of((nj - 1 - j) * SUBLANE, SUBLANE)
        of = af_s[pl.ds(rf, SUBLANE), :] * hf + uf_s[pl.ds(rf, SUBLANE), :]
        ob = ab_s[pl.ds(rb, SUBLANE), :] * hb + ub_s[pl.ds(rb, SUBLANE), :]
        hf_s[pl.ds(rf, SUBLANE), :] = of
        hb_s[pl.ds(rb, SUBLANE), :] = ob
        return (jnp.broadcast_to(of[SUBLANE - 1:SUBLANE, :], (SUBLANE, LANE)),
                jnp.broadcast_to(ob[0:1, :], (SUBLANE, LANE)))

    h0f = jnp.broadcast_to(h0_ref[0:1, :], (SUBLANE, LANE))
    h0b = jnp.broadcast_to(h0_ref[1:2, :], (SUBLANE, LANE))
    lax.fori_loop(0, nj, carry, (h0f, h0b))
    y_ref[...] = (_gelu_tanh(ag_ref[...]) * (hf_s[...] + hb_s[...])).astype(y_ref.dtype)
    st_ref[0:1, :] = hf_s[L - 1:L, :]
    st_ref[1:2, :] = hb_s[0:1, :]


def _lru(proj, row0, nseq, L, conv_w, conv_b, gate_w, gate_b, lam, h0):
    D_A = conv_w.shape[1]
    nb = D_A // LANE
    rb0 = row0 // L
    return pl.pallas_call(
        functools.partial(_lru_body, L=L),
        grid=(nseq, nb),
        in_specs=[pl.BlockSpec((L, LANE), lambda s, h: (rb0 + s, h)),
                  pl.BlockSpec((L, LANE), lambda s, h: (rb0 + s, nb + h)),
                  pl.BlockSpec((4, LANE), lambda s, h: (0, h)),
                  pl.BlockSpec((1, LANE), lambda s, h: (0, h)),
                  pl.BlockSpec((2, None, LANE, 2 * LANE), lambda s, h: (0, h, 0, 0)),
                  pl.BlockSpec((2, None, 1, 2 * LANE), lambda s, h: (0, h, 0, 0)),
                  pl.BlockSpec((2, None, 1, LANE), lambda s, h: (0, h, 0, 0)),
                  pl.BlockSpec((None, 2, LANE), lambda s, h: (s, 0, h))],
        out_specs=[pl.BlockSpec((L, LANE), lambda s, h: (s, h)),
                   pl.BlockSpec((None, 2, LANE), lambda s, h: (s, 0, h))],
        out_shape=[jax.ShapeDtypeStruct((nseq * L, D_A), BF16),
                   jax.ShapeDtypeStruct((nseq, 2, D_A), F32)],
        scratch_shapes=[pltpu.VMEM((L, LANE), F32)] * 6,
        compiler_params=_cparams(2),
        name="rglru",
    )(proj, proj, conv_w, conv_b.reshape(1, D_A), gate_w.astype(BF16),
      gate_b.reshape(2, nb, 1, 2 * LANE), lam.reshape(2, nb, 1, LANE), h0)


def _tri_mask(Tc, d):
    r = lax.broadcasted_iota(jnp.int32, (Tc, Tc), 0)
    c = lax.broadcasted_iota(jnp.int32, (Tc, Tc), 1)
    return (c - r) * (1 - 2 * d) <= 0


def _mlstm_body(q_ref, k_ref, v_ref, g_ref, gb_ref, c0_ref, n0_ref, m0_ref,
                h_ref, co_ref, no_ref, mo_ref, C_s, n_s, m_s, *, Tc, nc, k_scale):
    d = pl.program_id(0)
    hd = pl.program_id(2)
    c = pl.program_id(3)
    fwd = d == 0

    @pl.when(c == 0)
    def _():
        C_s[...] = c0_ref[...]
        n_s[...] = n0_ref[...]
        m_s[...] = m0_ref[...]

    lane = lax.broadcasted_iota(jnp.int32, (Tc, LANE), 1)
    G = g_ref[...] + gb_ref[...]
    tri = _tri_mask(Tc, d)
    Bm = jnp.dot(tri.astype(F32), _log_sigmoid(G), precision=HI, preferred_element_type=F32)
    sel_i = lane == d * (2 * B_HEADS) + hd
    sel_f = lane == d * (2 * B_HEADS) + B_HEADS + hd
    Bf = jnp.where(sel_f, Bm, 0.0)
    Gi = jnp.where(sel_i, G, 0.0)
    b_col = jnp.sum(Bf, axis=1, keepdims=True)
    i_col = jnp.sum(Gi, axis=1, keepdims=True)
    ones = jnp.ones((SUBLANE, LANE), F32)
    b_row = lax.dot_general(ones, Bf, NT_DIMS, precision=HI, preferred_element_type=F32)[0:1]
    i_row = lax.dot_general(ones, Gi, NT_DIMS, precision=HI, preferred_element_type=F32)[0:1]
    dmat = jnp.where(tri, b_col - b_row + i_row, -jnp.inf)
    m_prev = m_s[...]
    m_inter = b_col + m_prev
    m_t = jnp.maximum(m_inter, jnp.max(dmat, axis=1, keepdims=True))
    qf = q_ref[...]
    qb = qf.astype(BF16)
    kf = k_ref[...] * k_scale
    kb = kf.astype(BF16)
    vb = v_ref[...].astype(BF16)
    S = lax.dot_general(qb, kb, NT_DIMS, preferred_element_type=F32) * jnp.exp(dmat - m_t)
    inter_scale = jnp.exp(m_inter - m_t)
    num = (jnp.dot(S.astype(BF16), vb, preferred_element_type=F32)
           + inter_scale * jnp.dot(qb, C_s[...].astype(BF16), preferred_element_type=F32))
    qn = jnp.sum(qf * n_s[...], axis=1, keepdims=True)
    den = jnp.sum(S, axis=1, keepdims=True) + inter_scale * qn
    h_ref[...] = num / jnp.maximum(jnp.abs(den), jnp.exp(-m_t))
    bL = jnp.where(fwd, b_col[Tc - 1:Tc, :], b_col[0:1, :])
    g_col = bL - b_col + i_col
    m_new = jnp.maximum(bL + m_prev, jnp.max(g_col, axis=0, keepdims=True))
    wk = jnp.exp(g_col - m_new)
    decay = jnp.exp(bL + m_prev - m_new)
    kw = kf * wk
    C_s[...] = decay * C_s[...] + lax.dot_general(kw.astype(BF16), vb, TN_DIMS, preferred_element_type=F32)
    n_s[...] = decay * n_s[...] + jnp.sum(kw, axis=0, keepdims=True)
    m_s[...] = m_new

    @pl.when(c == nc - 1)
    def _():
        co_ref[...] = C_s[...]
        no_ref[...] = n_s[...]
        mo_ref[...] = m_s[...]


def _mlstm(proj, gates, gate_bias, row0, nseq, L, col0, c0, n0, m0):
    H = B_HEADS
    dh = c0.shape[-1]
    D_B = H * dh
    Tc = min(SCAN_CHUNK, L)
    nc = L // Tc
    rb0 = row0 // Tc
    cb0 = col0 // dh

    def rb(d, s, c):
        return s * nc + jnp.where(d == 0, c, nc - 1 - c)

    def qkv_spec(off):
        return pl.BlockSpec((Tc, dh), lambda d, s, h, c: (rb0 + rb(d, s, c), cb0 + off * H + h))

    st_map = lambda d, s, h, c: (s, d, h, 0, 0)
    return pl.pallas_call(
        functools.partial(_mlstm_body, Tc=Tc, nc=nc, k_scale=dh ** -0.5),
        grid=(2, nseq, H, nc),
        in_specs=[qkv_spec(0), qkv_spec(1), qkv_spec(2),
                  pl.BlockSpec((Tc, LANE), lambda d, s, h, c: (rb0 + rb(d, s, c), 0)),
                  pl.BlockSpec((1, LANE), lambda d, s, h, c: (0, 0)),
                  pl.BlockSpec((None, None, None, dh, dh), st_map),
                  pl.BlockSpec((None, None, None, 1, dh), st_map),
                  pl.BlockSpec((None, None, None, 1, 1), st_map)],
        out_specs=[pl.BlockSpec((None, Tc, dh), lambda d, s, h, c: (d, rb(d, s, c), h)),
                   pl.BlockSpec((None, None, None, dh, dh), st_map),
                   pl.BlockSpec((None, None, None, 1, dh), st_map),
                   pl.BlockSpec((None, None, None, 1, 1), st_map)],
        out_shape=[jax.ShapeDtypeStruct((2, nseq * L, D_B), F32),
                   jax.ShapeDtypeStruct((nseq, 2, H, dh, dh), F32),
                   jax.ShapeDtypeStruct((nseq, 2, H, 1, dh), F32),
                   jax.ShapeDtypeStruct((nseq, 2, H, 1, 1), F32)],
        scratch_shapes=[pltpu.VMEM((dh, dh), F32), pltpu.VMEM((1, dh), F32), pltpu.VMEM((1, 1), F32)],
        compiler_params=_cparams(4),
        name="mlstm",
    )(proj, proj, proj, gates, gate_bias, c0, n0.reshape(nseq, 2, H, 1, dh), m0.reshape(nseq, 2, H, 1, 1))


def _mlstm_out_body(h_ref, o_ref, g_ref, y_ref):
    hm = h_ref[0] + h_ref[1]
    mu = jnp.mean(hm, axis=1, keepdims=True)
    xc = hm - mu
    var = jnp.mean(xc * xc, axis=1, keepdims=True)
    y_ref[...] = (_sigmoid(o_ref[...]) * (xc * lax.rsqrt(var + LN_EPS) * g_ref[...])).astype(y_ref.dtype)


def _mlstm_out(h, proj, col0, norm_g, tm):
    _, T, D_B = h.shape
    dh = D_B // B_HEADS
    cb0 = col0 // dh
    return pl.pallas_call(
        _mlstm_out_body,
        grid=(T // tm, B_HEADS),
        in_specs=[pl.BlockSpec((2, tm, dh), lambda i, h: (0, i, h)),
                  pl.BlockSpec((tm, dh), lambda i, h: (i, cb0 + h)),
                  pl.BlockSpec((1, dh), lambda i, h: (0, h))],
        out_specs=pl.BlockSpec((tm, dh), lambda i, h: (i, h)),
        out_shape=jax.ShapeDtypeStruct((T, D_B), BF16),
        compiler_params=_cparams(2),
        name="mlstm_out",
    )(h, proj, norm_g.reshape(1, D_B))


def _gla_body(q_ref, k_ref, v_ref, lr_ref, gw_ref, gb_ref, s0_ref, o_ref, so_ref, ST_s, *, Tc, nc, q_scale):
    d = pl.program_id(0)
    c = pl.program_id(3)
    fwd = d == 0

    @pl.when(c == 0)
    def _():
        ST_s[...] = s0_ref[...].T

    z = jnp.dot(lr_ref[...], gw_ref[...], precision=HI, preferred_element_type=F32) + gb_ref[...]
    loga = _log_sigmoid(z) * (1.0 / GLA_TAU)
    tri = _tri_mask(Tc, d)
    b = jnp.dot(tri.astype(F32), loga, precision=HI, preferred_element_type=F32)
    qs = (q_ref[...] * q_scale * jnp.exp(b)).astype(BF16)
    kf = k_ref[...]
    ke = (kf * jnp.exp(-b)).astype(BF16)
    vf = v_ref[...]
    vb = vf.astype(BF16)
    att = jnp.where(tri, lax.dot_general(qs, ke, NT_DIMS, preferred_element_type=F32), 0.0)
    inter = lax.dot_general(qs, ST_s[...].astype(BF16), NT_DIMS, preferred_element_type=F32)
    o_ref[...] = inter + jnp.dot(att.astype(BF16), vb, preferred_element_type=F32)
    bL = jnp.where(fwd, b[Tc - 1:Tc, :], b[0:1, :])
    kd = (kf * jnp.exp(bL - b)).astype(BF16)
    ST_s[...] = ST_s[...] * jnp.exp(bL) + lax.dot_general(vb, kd, TN_DIMS, preferred_element_type=F32)

    @pl.when(c == nc - 1)
    def _():
        so_ref[...] = ST_s[...].T


def _gla(proj, lr, gw_pad, gate_b, row0, nseq, L, s0):
    H = C_HEADS
    dk, dv = s0.shape[-2:]
    DK, DV = H * dk, H * dv
    Tc = min(SCAN_CHUNK, L)
    nc = L // Tc
    rb0 = row0 // Tc

    def rb(d, s, c):
        return s * nc + jnp.where(d == 0, c, nc - 1 - c)

    st_map = lambda d, s, h, c: (s, d, h, 0, 0)
    return pl.pallas_call(
        functools.partial(_gla_body, Tc=Tc, nc=nc, q_scale=dk ** -0.5),
        grid=(2, nseq, H, nc),
        in_specs=[pl.BlockSpec((Tc, dk), lambda d, s, h, c: (rb0 + rb(d, s, c), h)),
                  pl.BlockSpec((Tc, dk), lambda d, s, h, c: (rb0 + rb(d, s, c), H + h)),
                  pl.BlockSpec((Tc, dv), lambda d, s, h, c: (rb0 + rb(d, s, c), (2 * DK) // dv + h)),
                  pl.BlockSpec((Tc, LANE), lambda d, s, h, c: (rb0 + rb(d, s, c), 0)),
                  pl.BlockSpec((None, LANE, dk), lambda d, s, h, c: (d, 0, h)),
                  pl.BlockSpec((None, 1, dk), lambda d, s, h, c: (d, 0, h)),
                  pl.BlockSpec((None, None, None, dk, dv), st_map)],
        out_specs=[pl.BlockSpec((None, Tc, dv), lambda d, s, h, c: (d, rb(d, s, c), h)),
                   pl.BlockSpec((None, None, None, dk, dv), st_map)],
        out_shape=[jax.ShapeDtypeStruct((2, nseq * L, DV), F32),
                   jax.ShapeDtypeStruct((nseq, 2, H, dk, dv), F32)],
        scratch_shapes=[pltpu.VMEM((dv, dk), F32)],
        compiler_params=_cparams(4),
        name="gla",
    )(proj, proj, proj, lr, gw_pad, gate_b.reshape(2, 1, DK), s0)


def _gla_out_body(o_ref, g_ref, ng_ref, y_ref):
    oo = o_ref[0] + o_ref[1]
    ms = jnp.mean(oo * oo, axis=1, keepdims=True)
    y_ref[...] = (oo * lax.rsqrt(ms + LN_EPS) * ng_ref[...] * _silu(g_ref[...])).astype(y_ref.dtype)


def _gla_out(o, proj, col0, norm_g, tm):
    _, T, DV = o.shape
    dv = DV // C_HEADS
    cb0 = col0 // dv
    return pl.pallas_call(
        _gla_out_body,
        grid=(T // tm, C_HEADS),
        in_specs=[pl.BlockSpec((2, tm, dv), lambda i, h: (0, i, h)),
                  pl.BlockSpec((tm, dv), lambda i, h: (i, cb0 + h)),
                  pl.BlockSpec((1, dv), lambda i, h: (0, h))],
        out_specs=pl.BlockSpec((tm, dv), lambda i, h: (i, h)),
        out_shape=jax.ShapeDtypeStruct((T, DV), BF16),
        compiler_params=_cparams(2),
        name="gla_out",
    )(o, proj, norm_g.reshape(1, DV))


def _layer_norm(z, g, b):
    mu = jnp.mean(z, axis=1, keepdims=True)
    zc = z - mu
    var = jnp.mean(zc * zc, axis=1, keepdims=True)
    return zc * lax.rsqrt(var + LN_EPS) * g + b


def _outproj_body(*refs, nparts, g_row, alpha):
    ys = refs[:nparts]
    ws = refs[nparts:2 * nparts]
    x_ref, m_ref, lg_ref, lb_ref, o_ref = refs[2 * nparts:]
    acc = jnp.dot(ys[0][...], ws[0][...], preferred_element_type=F32)
    for y, w in zip(ys[1:], ws[1:]):
        acc = acc + jnp.dot(y[...], w[...], preferred_element_type=F32)
    z = alpha * x_ref[...] + m_ref[g_row:g_row + 1, :] * acc
    o_ref[...] = _layer_norm(z, lg_ref[...], lb_ref[...])


def _outproj_ln(ys, ws, x, modt, ln_g, ln_b, *, g_row, alpha, tm):
    T, D = x.shape
    n = len(ys)
    mod4, mod_map = modt
    in_specs = ([pl.BlockSpec((tm, y.shape[1]), lambda i: (i, 0)) for y in ys]
                + [pl.BlockSpec(w.shape, lambda i: (0, 0)) for w in ws]
                + [pl.BlockSpec((tm, D), lambda i: (i, 0)),
                   pl.BlockSpec((None, None, 6, D), lambda i: mod_map(i)),
                   pl.BlockSpec((1, D), lambda i: (0, 0)),
                   pl.BlockSpec((1, D), lambda i: (0, 0))])
    return pl.pallas_call(
        functools.partial(_outproj_body, nparts=n, g_row=g_row, alpha=alpha),
        grid=(T // tm,),
        in_specs=in_specs,
        out_specs=pl.BlockSpec((tm, D), lambda i: (i, 0)),
        out_shape=jax.ShapeDtypeStruct((T, D), F32),
        compiler_params=_cparams(1),
        name="outproj_ln",
    )(*ys, *ws, x, mod4, ln_g.reshape(1, D), ln_b.reshape(1, D))


def _router_body(x_ref, m_ref, w_ref, b_ref, h_ref, ids_ref, gts_ref, *, sh_row, sc_row):
    h = x_ref[...] * (1.0 + m_ref[sc_row:sc_row + 1, :]) + m_ref[sh_row:sh_row + 1, :]
    h_ref[...] = h.astype(h_ref.dtype)
    logits = jnp.dot(h, w_ref[...], precision=HI, preferred_element_type=F32) + b_ref[...]
    tm = logits.shape[0]
    lane = lax.broadcasted_iota(jnp.int32, (tm, LANE), 1)
    real = lane < N_EXPERTS
    logits = jnp.where(real, logits, -jnp.inf)
    mx = jnp.max(logits, axis=1, keepdims=True)
    ex = jnp.exp(logits - mx)
    p = ex / jnp.sum(ex, axis=1, keepdims=True)
    per = N_EXPERTS // N_GROUPS
    big = jnp.int32(LANE)
    best = None
    for g in range(N_GROUPS):
        in_g = (lane >= g * per) & (lane < (g + 1) * per)
        pg = jnp.where(in_g, p, -1.0)
        v1 = jnp.max(pg, axis=1, keepdims=True)
        i1 = jnp.min(jnp.where(pg == v1, lane, big), axis=1, keepdims=True)
        pg2 = jnp.where(lane == i1, -1.0, pg)
        v2 = jnp.max(pg2, axis=1, keepdims=True)
        i2 = jnp.min(jnp.where(pg2 == v2, lane, big), axis=1, keepdims=True)
        score = v1 + v2
        if best is None:
            best = (score, v1, i1, v2, i2)
        else:
            take = score > best[0]
            best = tuple(jnp.where(take, n, o) for n, o in zip((score, v1, i1, v2, i2), best))
    _, v1, i1, v2, i2 = best
    tot = v1 + v2
    ids_ref[...] = jnp.where(lane == 0, i1, jnp.where(lane == 1, i2, 0))
    gts_ref[...] = jnp.where(lane == 0, v1 / tot, jnp.where(lane == 1, v2 / tot, 0.0))


def _router(x, modt, rw_pad, rb_pad, *, sh_row, sc_row, tm):
    T, D = x.shape
    mod4, mod_map = modt
    return pl.pallas_call(
        functools.partial(_router_body, sh_row=sh_row, sc_row=sc_row),
        grid=(T // tm,),
        in_specs=[pl.BlockSpec((tm, D), lambda i: (i, 0)),
                  pl.BlockSpec((None, None, 6, D), lambda i: mod_map(i)),
                  pl.BlockSpec((D, LANE), lambda i: (0, 0)),
                  pl.BlockSpec((1, LANE), lambda i: (0, 0))],
        out_specs=[pl.BlockSpec((tm, D), lambda i: (i, 0)),
                   pl.BlockSpec((tm, LANE), lambda i: (i, 0)),
                   pl.BlockSpec((tm, LANE), lambda i: (i, 0))],
        out_shape=[jax.ShapeDtypeStruct((T, D), BF16),
                   jax.ShapeDtypeStruct((T, LANE), jnp.int32),
                   jax.ShapeDtypeStruct((T, LANE), F32)],
        compiler_params=_cparams(1),
        name="router",
    )(x, mod4, rw_pad, rb_pad)


def _ffn_body(te_ref, va_ref, x_ref, wi_ref, wo_ref, o_ref, *, F):
    i = pl.program_id(0)

    @pl.when(va_ref[i] != 0)
    def _():
        uw = jnp.dot(x_ref[...], wi_ref[...], preferred_element_type=F32)
        act = (_silu(uw[:, :F]) * uw[:, F:]).astype(BF16)
        o_ref[...] = jnp.dot(act, wo_ref[...], preferred_element_type=F32)

    @pl.when(va_ref[i] == 0)
    def _():
        o_ref[...] = jnp.zeros_like(o_ref)


def _expert_ffn(xs, w_in, w_out, layer, tile_e, tile_valid):
    R, D = xs.shape
    F = w_out.shape[2]
    tm = MOE_TM
    return pl.pallas_call(
        functools.partial(_ffn_body, F=F),
        grid_spec=pltpu.PrefetchScalarGridSpec(
            num_scalar_prefetch=2,
            grid=(R // tm,),
            in_specs=[pl.BlockSpec((tm, D), lambda i, te, va: (i, 0)),
                      pl.BlockSpec((None, None, D, 2 * F), lambda i, te, va: (layer, te[i], 0, 0)),
                      pl.BlockSpec((None, None, F, D), lambda i, te, va: (layer, te[i], 0, 0))],
            out_specs=pl.BlockSpec((tm, D), lambda i, te, va: (i, 0))),
        out_shape=jax.ShapeDtypeStruct((R, D), F32),
        compiler_params=_cparams(1),
        name="expert_ffn",
    )(tile_e, tile_valid, xs, w_in, w_out)


def _combine_body(x_ref, y0_ref, y1_ref, gt_ref, m_ref, lg_ref, lb_ref, o_ref, *, g_row, alpha):
    gt = gt_ref[...]
    moe = gt[:, 0:1] * y0_ref[...] + gt[:, 1:2] * y1_ref[...]
    z = alpha * x_ref[...] + m_ref[g_row:g_row + 1, :] * moe
    o_ref[...] = _layer_norm(z, lg_ref[...], lb_ref[...])


def _combine_ln(x, y0, y1, gts, modt, ln_g, ln_b, *, g_row, alpha, tm):
    T, D = x.shape
    mod4, mod_map = modt
    row = pl.BlockSpec((tm, D), lambda i: (i, 0))
    vec = pl.BlockSpec((1, D), lambda i: (0, 0))
    return pl.pallas_call(
        functools.partial(_combine_body, g_row=g_row, alpha=alpha),
        grid=(T // tm,),
        in_specs=[row, row, row, pl.BlockSpec((tm, LANE), lambda i: (i, 0)),
                  pl.BlockSpec((None, None, 6, D), lambda i: mod_map(i)), vec, vec],
        out_specs=row,
        out_shape=jax.ShapeDtypeStruct((T, D), F32),
        compiler_params=_cparams(1),
        name="combine_ln",
    )(x, y0, y1, gts, mod4, ln_g.reshape(1, D), ln_b.reshape(1, D))


def _route_meta(ids, tm):
    T = ids.shape[0]
    E = N_EXPERTS
    e_flat = ids.reshape(-1)
    onehot = (e_flat[:, None] == jnp.arange(E, dtype=jnp.int32)[None, :]).astype(jnp.int32)
    csum = jnp.cumsum(onehot, axis=0)
    rank = jnp.take_along_axis(csum, e_flat[:, None], axis=1)[:, 0] - 1
    counts = csum[-1]
    padded = ((counts + tm - 1) // tm) * tm
    ends = jnp.cumsum(padded)
    starts = ends - padded
    dest = starts[e_flat] + rank
    n_rows = 2 * T + E * tm
    src_tok = jnp.zeros((n_rows,), jnp.int32).at[dest].set(jnp.arange(2 * T, dtype=jnp.int32) // 2)
    tile_start = jnp.arange(n_rows // tm, dtype=jnp.int32) * tm
    valid = (tile_start < ends[-1]).astype(jnp.int32)
    tile_e = jnp.searchsorted(ends, jnp.minimum(tile_start, ends[-1] - 1), side="right").astype(jnp.int32)
    return dest.reshape(T, 2), src_tok, jnp.minimum(tile_e, E - 1), valid


def _pos_embed_2d(n_tokens, dim):
    rows = n_tokens // GRID_W
    quarter = dim // 4
    freqs = jnp.exp(-math.log(10000.0) * jnp.arange(quarter, dtype=F32) / quarter)
    r = jnp.broadcast_to(jnp.arange(rows, dtype=F32)[:, None], (rows, GRID_W)).reshape(-1)
    col = jnp.broadcast_to(jnp.arange(GRID_W, dtype=F32)[None, :], (rows, GRID_W)).reshape(-1)
    ar = r[:, None] * freqs
    ac = col[:, None] * freqs
    return jnp.concatenate([jnp.sin(ar), jnp.cos(ar), jnp.sin(ac), jnp.cos(ac)], -1)


def _pad_cols(w, n):
    return jnp.pad(w, ((0, 0),) * (w.ndim - 1) + ((0, n - w.shape[-1]),))


def kernel(x_prompt, x_sample, state_lru, state_mlstm_c, state_mlstm_n, state_mlstm_m, state_gla, c, c_ctx, mod_w, mod_b, ln_g, ln_b, even_w_in, even_w_out, lru_conv_w, lru_conv_b, lru_gate_w, lru_gate_b, lru_lambda, mlstm_gate_b, mlstm_norm_g, odd_w_in, odd_w_out, gla_gate_w, gla_gate_b, gla_norm_g, router_w, router_b, moe_w_in, moe_w_out):
    Bp, Lp, D = x_prompt.shape
    Bs, Ls, _ = x_sample.shape
    depth = mod_w.shape[0]
    Tp, Ts = Bp * Lp, Bs * Ls
    T = Tp + Ts
    alpha = (2 * depth) ** 0.25
    D_A = lru_conv_w.shape[-1]
    D_B = mlstm_norm_g.shape[-1]
    DK = gla_gate_w.shape[-1]
    DV = gla_norm_g.shape[-1]
    tm = next(t for t in (512, 256, 128) if Tp % t == 0 and Ls % t == 0)
    tn = 1024

    n_cond = 1 + Bs
    R = -(-n_cond // SUBLANE) * SUBLANE
    cond = jnp.zeros((R, D), F32).at[0].set(c_ctx).at[1:n_cond].set(c)
    mod_all = _modulation(cond, mod_w, mod_b)
    mod4 = mod_all.reshape(depth, R, 6, D)
    npt, per_seq = Tp // tm, Ls // tm

    x = _embed(x_prompt.reshape(Tp, D), x_sample.reshape(Ts, D), _pos_embed_2d(Ls, D), tm)

    rw_pad = _pad_cols(router_w, LANE)
    rb_pad = _pad_cols(router_b.reshape(1, -1), LANE)
    moe_w_in_b = moe_w_in.astype(BF16)
    moe_w_out_b = moe_w_out.astype(BF16)
    groups = ((0, Bp, Lp), (Tp, Bs, Ls))

    s_lru, s_c, s_n, s_m, s_gla = [], [], [], [], []
    for l in range(depth):
        j = l // 2
        mt = (mod4, lambda i, l=l: (l, jnp.where(i < npt, 0, 1 + (i - npt) // per_seq), 0, 0))
        if l % 2 == 0:
            w_in = even_w_in[j]
            n_main = 2 * D_A + 4 * D_B
            proj = _proj(x, mt, w_in[:, :n_main].astype(BF16), sh_row=0, sc_row=1, tm=tm, tn=tn)
            gates = _proj(x, mt, _pad_cols(w_in[:, n_main:], LANE), sh_row=0, sc_row=1, tm=tm, tn=LANE, hp=True)
            gate_bias = _pad_cols(mlstm_gate_b[j].reshape(1, -1), LANE)
            ya, hs = [], []
            for gi, (row0, nseq, L) in enumerate(groups):
                if gi == 0:
                    lru0 = jnp.zeros((nseq, 2, D_A), F32)
                    c0 = jnp.zeros((nseq, 2, B_HEADS, D_B // B_HEADS, D_B // B_HEADS), F32)
                    n0 = jnp.zeros((nseq, 2, B_HEADS, D_B // B_HEADS), F32)
                    m0 = jnp.zeros((nseq, 2, B_HEADS), F32)
                else:
                    lru0, c0, n0, m0 = state_lru[:, j], state_mlstm_c[:, j], state_mlstm_n[:, j], state_mlstm_m[:, j]
                y_a, st = _lru(proj, row0, nseq, L, lru_conv_w[j], lru_conv_b[j], lru_gate_w[j], lru_gate_b[j],
                               lru_lambda[j], lru0)
                h, C, n, m = _mlstm(proj, gates, gate_bias, row0, nseq, L, 2 * D_A, c0, n0, m0)
                ya.append(y_a)
                hs.append(h)
                if gi == 0:
                    s_lru.append(st)
                    s_c.append(C)
                    s_n.append(n.reshape(nseq, 2, B_HEADS, -1))
                    s_m.append(m.reshape(nseq, 2, B_HEADS))
            y_a = jnp.concatenate(ya, axis=0)
            y_b = _mlstm_out(jnp.concatenate(hs, axis=1), proj, 2 * D_A + 3 * D_B, mlstm_norm_g[j], tm)
            w_out = even_w_out[j].astype(BF16)
            x = _outproj_ln([y_a, y_b], [w_out[:D_A], w_out[D_A:]], x, mt, ln_g[l, 0], ln_b[l, 0],
                            g_row=2, alpha=alpha, tm=tm)
        else:
            w_in = odd_w_in[j]
            n_main = 2 * DK + 2 * DV
            proj = _proj(x, mt, w_in[:, :n_main].astype(BF16), sh_row=0, sc_row=1, tm=tm, tn=tn)
            lr = _proj(x, mt, _pad_cols(w_in[:, n_main:], LANE), sh_row=0, sc_row=1, tm=tm, tn=LANE, hp=True)
            gw_pad = jnp.zeros((2, LANE, DK), F32)
            for dd in range(2):
                gw_pad = gw_pad.at[dd, dd * GLA_RANK:(dd + 1) * GLA_RANK].set(gla_gate_w[j, dd])
            os_ = []
            for gi, (row0, nseq, L) in enumerate(groups):
                s0 = (jnp.zeros((nseq, 2, C_HEADS, DK // C_HEADS, DV // C_HEADS), F32) if gi == 0
                      else state_gla[:, j])
                o, S = _gla(proj, lr, gw_pad, gla_gate_b[j], row0, nseq, L, s0)
                os_.append(o)
                if gi == 0:
                    s_gla.append(S)
            y = _gla_out(jnp.concatenate(os_, axis=1), proj, 2 * DK + DV, gla_norm_g[j], tm)
            x = _outproj_ln([y], [odd_w_out[j].astype(BF16)], x, mt, ln_g[l, 0], ln_b[l, 0],
                            g_row=2, alpha=alpha, tm=tm)

        hf, ids, gts = _router(x, mt, rw_pad, rb_pad, sh_row=3, sc_row=4, tm=tm)
        dest, src_tok, tile_e, tile_valid = _route_meta(ids[:, :2], MOE_TM)
        xs = jnp.take(hf, src_tok, axis=0)
        ys = _expert_ffn(xs, moe_w_in_b, moe_w_out_b, l, tile_e, tile_valid)
        y0 = jnp.take(ys, dest[:, 0], axis=0)
        y1 = jnp.take(ys, dest[:, 1], axis=0)
        x = _combine_ln(x, y0, y1, gts, mt, ln_g[l, 1], ln_b[l, 1], g_row=5, alpha=alpha, tm=tm)

    y_prompt = x[:Tp].reshape(Bp, Lp, D)
    y_sample = x[Tp:].reshape(Bs, Ls, D)
    return (y_prompt, y_sample, jnp.stack(s_lru, 1), jnp.stack(s_c, 1), jnp.stack(s_n, 1),
            jnp.stack(s_m, 1), jnp.stack(s_gla, 1))
```

```python
import functools
import math

import jax
import jax.numpy as jnp
from jax import lax
from jax.experimental import pallas as pl
from jax.experimental.pallas import tpu as pltpu

F32 = jnp.float32
BF16 = jnp.bfloat16
HI = lax.Precision.HIGHEST

LN_EPS = 1e-5
LRU_C = 8.0
A_BLOCKS = 8
B_HEADS = 4
C_HEADS = 4
GLA_RANK = 16
GLA_TAU = 16.0
N_EXPERTS = 16
N_GROUPS = 4
GRID_W = 64
LANE = 128
SUBLANE = 8
SCAN_CHUNK = 128
MOE_TM = 512
VMEM_LIMIT = 56 * 1024 * 1024

NT_DIMS = (((1,), (1,)), ((), ()))
TN_DIMS = (((0,), (0,)), ((), ()))


def _cparams(n_axes):
    return pltpu.CompilerParams(dimension_semantics=("arbitrary",) * n_axes,
                                vmem_limit_bytes=VMEM_LIMIT)


def _sigmoid(x):
    return 1.0 / (1.0 + jnp.exp(-x))


def _silu(x):
    return x * _sigmoid(x)


def _log_sigmoid(x):
    return jnp.minimum(x, 0.0) - jnp.log1p(jnp.exp(-jnp.abs(x)))


def _gelu_tanh(x):
    return 0.5 * x * (1.0 + jnp.tanh(math.sqrt(2.0 / math.pi) * (x + 0.044715 * (x * x * x))))


def _mod_body(c_ref, w_ref, b_ref, o_ref):
    c = c_ref[...]
    o_ref[...] = jnp.dot(_silu(c), w_ref[...], precision=HI, preferred_element_type=F32) + b_ref[...]


def _modulation(cond, mod_w, mod_b):
    R, D = cond.shape
    nl, _, N = mod_w.shape
    tn = 1024
    return pl.pallas_call(
        _mod_body,
        grid=(nl, N // tn),
        in_specs=[pl.BlockSpec((R, D), lambda l, j: (0, 0)),
                  pl.BlockSpec((None, D, tn), lambda l, j: (l, 0, j)),
                  pl.BlockSpec((None, 1, tn), lambda l, j: (l, 0, j))],
        out_specs=pl.BlockSpec((None, R, tn), lambda l, j: (l, 0, j)),
        out_shape=jax.ShapeDtypeStruct((nl, R, N), F32),
        compiler_params=_cparams(2),
        name="modulation",
    )(cond, mod_w, mod_b.reshape(nl, 1, N))


def _embed_body(xp_ref, xs_ref, pos_ref, o_ref, *, n_prompt_tiles):
    i = pl.program_id(0)

    @pl.when(i < n_prompt_tiles)
    def _():
        o_ref[...] = xp_ref[...]

    @pl.when(i >= n_prompt_tiles)
    def _():
        o_ref[...] = xs_ref[...] + pos_ref[...]


def _embed(xp, xs, pos, tm):
    Tp, D = xp.shape
    Ts = xs.shape[0]
    Ls = pos.shape[0]
    npt, nst, npos = Tp // tm, Ts // tm, Ls // tm
    return pl.pallas_call(
        functools.partial(_embed_body, n_prompt_tiles=npt),
        grid=(npt + nst,),
        in_specs=[pl.BlockSpec((tm, D), lambda i: (jnp.minimum(i, npt - 1), 0)),
                  pl.BlockSpec((tm, D), lambda i: (jnp.maximum(i - npt, 0), 0)),
                  pl.BlockSpec((tm, D), lambda i: (jnp.maximum(i - npt, 0) % npos, 0))],
        out_specs=pl.BlockSpec((tm, D), lambda i: (i, 0)),
        out_shape=jax.ShapeDtypeStruct((Tp + Ts, D), F32),
        compiler_params=_cparams(1),
        name="embed",
    )(xp, xs, pos)


def _proj_body(x_ref, m_ref, w_ref, o_ref, h_ref, *, sh_row, sc_row, hp):
    @pl.when(pl.program_id(1) == 0)
    def _():
        h = x_ref[...] * (1.0 + m_ref[sc_row:sc_row + 1, :]) + m_ref[sh_row:sh_row + 1, :]
        h_ref[...] = h.astype(h_ref.dtype)

    if hp:
        o = jnp.dot(h_ref[...], w_ref[...], precision=HI, preferred_element_type=F32)
    else:
        o = jnp.dot(h_ref[...], w_ref[...], preferred_element_type=F32)
    o_ref[...] = o.astype(o_ref.dtype)


def _proj(x, modt, w, *, sh_row, sc_row, tm, tn, hp=False):
    T, D = x.shape
    N = w.shape[1]
    mod4, mod_map = modt
    return pl.pallas_call(
        functools.partial(_proj_body, sh_row=sh_row, sc_row=sc_row, hp=hp),
        grid=(T // tm, N // tn),
        in_specs=[pl.BlockSpec((tm, D), lambda i, j: (i, 0)),
                  pl.BlockSpec((None, None, 6, D), lambda i, j: mod_map(i)),
                  pl.BlockSpec((D, tn), lambda i, j: (0, j))],
        out_specs=pl.BlockSpec((tm, tn), lambda i, j: (i, j)),
        out_shape=jax.ShapeDtypeStruct((T, N), F32),
        scratch_shapes=[pltpu.VMEM((tm, D), F32 if hp else BF16)],
        compiler_params=_cparams(2),
        name="proj_hp" if hp else "proj",
    )(x, mod4, w)


def _lru_body(ag_ref, ax_ref, cw_ref, cb_ref, gw_ref, gb_ref, lam_ref, h0_ref, y_ref, st_ref,
              af_s, uf_s, ab_s, ub_s, hf_s, hb_s, *, L):
    x = ax_ref[...]
    row = lax.broadcasted_iota(jnp.int32, (L, LANE), 0)
    xm1 = jnp.where(row >= 1, pltpu.roll(x, 1, 0), 0.0)
    xp1 = jnp.where(row < L - 1, pltpu.roll(x, L - 1, 0), 0.0)
    xp2 = jnp.where(row < L - 2, pltpu.roll(x, L - 2, 0), 0.0)
    xc = cb_ref[...] + xm1 * cw_ref[0:1, :] + x * cw_ref[1:2, :] + xp1 * cw_ref[2:3, :] + xp2 * cw_ref[3:4, :]
    xcb = xc.astype(BF16)
    nj = L // SUBLANE
    sub = lax.broadcasted_iota(jnp.int32, (nj, SUBLANE, LANE), 1)

    for d, (a_s, u_s) in enumerate(((af_s, uf_s), (ab_s, ub_s))):
        g = jnp.dot(xcb, gw_ref[d], preferred_element_type=F32) + gb_ref[d]
        r = _sigmoid(g[:, :LANE])
        ig = _sigmoid(g[:, LANE:])
        nlam = -lam_ref[d]
        softplus = jnp.maximum(nlam, 0.0) + jnp.log1p(jnp.exp(-jnp.abs(nlam)))
        log_a = (-LRU_C * softplus) * r
        a = jnp.exp(log_a)
        u = jnp.sqrt(jnp.tanh(-log_a) * (1.0 + a * a)) * ig * xc
        a3 = a.reshape(nj, SUBLANE, LANE)
        u3 = u.reshape(nj, SUBLANE, LANE)
        for k in (1, 2, 4):
            if d == 0:
                sh, keep = k, sub >= k
            else:
                sh, keep = SUBLANE - k, sub < SUBLANE - k
            a_sh = pltpu.roll(a3, sh, 1)
            u_sh = pltpu.roll(u3, sh, 1)
            u3 = jnp.where(keep, a3 * u_sh + u3, u3)
            a3 = jnp.where(keep, a3 * a_sh, a3)
        a_s[...] = a3.reshape(L, LANE)
        u_s[...] = u3.reshape(L, LANE)

    def carry(j, hs):
        hf, hb = hs
        rf = pl.multiple_of(j * SUBLANE, SUBLANE)
        rb = pl.multiple_of((nj - 1 - j) * SUBLANE, SUBLANE)
        of = af_s[pl.ds(rf, SUBLANE), :] * hf + uf_s[pl.ds(rf, SUBLANE), :]
        ob = ab_s[pl.ds(rb, SUBLANE), :] * hb + ub_s[pl.ds(rb, SUBLANE), :]
        hf_s[pl.ds(rf, SUBLANE), :] = of
        hb_s[pl.ds(rb, SUBLANE), :] = ob
        return (jnp.broadcast_to(of[SUBLANE - 1:SUBLANE, :], (SUBLANE, LANE)),
                jnp.broadcast_to(ob[0:1, :], (SUBLANE, LANE)))

    h0f = jnp.broadcast_to(h0_ref[0:1, :], (SUBLANE, LANE))
    h0b = jnp.broadcast_to(h0_ref[1:2, :], (SUBLANE, LANE))
    lax.fori_loop(0, nj, carry, (h0f, h0b))
    y_ref[...] = (_gelu_tanh(ag_ref[...]) * (hf_s[...] + hb_s[...])).astype(y_ref.dtype)
    st_ref[0:1, :] = hf_s[L - 1:L, :]
    st_ref[1:2, :] = hb_s[0:1, :]


def _lru(proj, row0, nseq, L, conv_w, conv_b, gate_w, gate_b, lam, h0):
    D_A = conv_w.shape[1]
    nb = D_A // LANE
    rb0 = row0 // L
    in_specs = [pl.BlockSpec((L, LANE), lambda s, h: (rb0 + s, h)),
                pl.BlockSpec((L, LANE), lambda s, h: (rb0 + s, nb + h)),
                pl.BlockSpec((4, LANE), lambda s, h: (0, h)),
                pl.BlockSpec((1, LANE), lambda s, h: (0, h)),
                pl.BlockSpec((2, None, LANE, 2 * LANE), lambda s, h: (0, h, 0, 0)),
                pl.BlockSpec((2, None, 1, 2 * LANE), lambda s, h: (0, h, 0, 0)),
                pl.BlockSpec((2, None, 1, LANE), lambda s, h: (0, h, 0, 0)),
                pl.BlockSpec((None, 2, LANE), lambda s, h: (s, 0, h))]
    args = [proj, proj, conv_w, conv_b.reshape(1, D_A), gate_w.astype(BF16),
            gate_b.reshape(2, nb, 1, 2 * LANE), lam.reshape(2, nb, 1, LANE), h0]
    return pl.pallas_call(
        functools.partial(_lru_body, L=L),
        grid=(nseq, nb),
        in_specs=in_specs,
        out_specs=[pl.BlockSpec((L, LANE), lambda s, h: (s, h)),
                   pl.BlockSpec((None, 2, LANE), lambda s, h: (s, 0, h))],
        out_shape=[jax.ShapeDtypeStruct((nseq * L, D_A), BF16),
                   jax.ShapeDtypeStruct((nseq, 2, D_A), F32)],
        scratch_shapes=[pltpu.VMEM((L, LANE), F32)] * 6,
        compiler_params=_cparams(2),
        name="rglru",
    )(*args)


def _tri_mask(Tc, d):
    r = lax.broadcasted_iota(jnp.int32, (Tc, Tc), 0)
    c = lax.broadcasted_iota(jnp.int32, (Tc, Tc), 1)
    return (c - r) * (1 - 2 * d) <= 0


def _mlstm_body(q_ref, k_ref, v_ref, g_ref, gb_ref, c0_ref, n0_ref, m0_ref,
                h_ref, co_ref, no_ref, mo_ref, C_s, n_s, m_s, *, Tc, nc, k_scale, dh):
    d = pl.program_id(0)
    c = pl.program_id(2)
    fwd = d == 0
    H = B_HEADS

    @pl.when(c == 0)
    def _():
        C_s[...] = c0_ref[...]
        n_s[...] = n0_ref[...]
        m_s[...] = m0_ref[...]

    lane = lax.broadcasted_iota(jnp.int32, (Tc, LANE), 1)
    G = g_ref[...] + gb_ref[...]
    tri = _tri_mask(Tc, d)
    Bm = jnp.dot(tri.astype(F32), _log_sigmoid(G), precision=HI, preferred_element_type=F32)
    col0 = d * (2 * H)
    X = jnp.where(lane >= col0 + H, Bm, G)
    srow = lax.broadcasted_iota(jnp.int32, (2 * H, LANE), 0)
    slane = lax.broadcasted_iota(jnp.int32, (2 * H, LANE), 1)
    sel = (slane == col0 + srow).astype(F32)
    R = lax.dot_general(sel, X, NT_DIMS, precision=HI, preferred_element_type=F32)
    for hd in range(H):
        i_col = jnp.sum(jnp.where(lane == col0 + hd, X, 0.0), axis=1, keepdims=True)
        b_col = jnp.sum(jnp.where(lane == col0 + H + hd, X, 0.0), axis=1, keepdims=True)
        i_row = R[hd:hd + 1, :]
        b_row = R[H + hd:H + hd + 1, :]
        dmat = jnp.where(tri, b_col - b_row + i_row, -jnp.inf)
        m_prev = m_s[hd]
        m_inter = b_col + m_prev
        m_t = jnp.maximum(m_inter, jnp.max(dmat, axis=1, keepdims=True))
        cs = slice(hd * dh, (hd + 1) * dh)
        qf = q_ref[:, cs]
        qb = qf.astype(BF16)
        kf = k_ref[:, cs] * k_scale
        kb = kf.astype(BF16)
        vb = v_ref[:, cs].astype(BF16)
        S = lax.dot_general(qb, kb, NT_DIMS, preferred_element_type=F32) * jnp.exp(dmat - m_t)
        inter_scale = jnp.exp(m_inter - m_t)
        Cm = C_s[hd]
        num = (jnp.dot(S.astype(BF16), vb, preferred_element_type=F32)
               + inter_scale * jnp.dot(qb, Cm.astype(BF16), preferred_element_type=F32))
        nv = n_s[hd]
        qn = jnp.sum(qf * nv, axis=1, keepdims=True)
        den = jnp.sum(S, axis=1, keepdims=True) + inter_scale * qn
        h_ref[:, cs] = num / jnp.maximum(jnp.abs(den), jnp.exp(-m_t))
        bL = jnp.where(fwd, b_col[Tc - 1:Tc, :], b_col[0:1, :])
        g_col = bL - b_col + i_col
        m_new = jnp.maximum(bL + m_prev, jnp.max(g_col, axis=0, keepdims=True))
        wk = jnp.exp(g_col - m_new)
        decay = jnp.exp(bL + m_prev - m_new)
        kw = kf * wk
        C_s[hd] = decay * Cm + lax.dot_general(kw.astype(BF16), vb, TN_DIMS, preferred_element_type=F32)
        n_s[hd] = decay * nv + jnp.sum(kw, axis=0, keepdims=True)
        m_s[hd] = m_new

    @pl.when(c == nc - 1)
    def _():
        co_ref[...] = C_s[...]
        no_ref[...] = n_s[...]
        mo_ref[...] = m_s[...]


def _mlstm(proj, gates, gate_bias, row0, nseq, L, col0, c0, n0, m0):
    H = B_HEADS
    dh = c0.shape[-1]
    D_B = H * dh
    Tc = min(SCAN_CHUNK, L)
    nc = L // Tc
    rb0 = row0 // Tc
    cb0 = col0 // D_B

    def rb(d, s, c):
        return s * nc + jnp.where(d == 0, c, nc - 1 - c)

    def qkv_spec(off):
        return pl.BlockSpec((Tc, D_B), lambda d, s, c: (rb0 + rb(d, s, c), cb0 + off))

    st_map = lambda d, s, c: (s, d, 0, 0, 0)
    in_specs = [qkv_spec(0), qkv_spec(1), qkv_spec(2),
                pl.BlockSpec((Tc, LANE), lambda d, s, c: (rb0 + rb(d, s, c), 0)),
                pl.BlockSpec((1, LANE), lambda d, s, c: (0, 0)),
                pl.BlockSpec((None, None, H, dh, dh), st_map),
                pl.BlockSpec((None, None, H, 1, dh), st_map),
                pl.BlockSpec((None, None, H, 1, 1), st_map)]
    args = [proj, proj, proj, gates, gate_bias, c0, n0.reshape(nseq, 2, H, 1, dh), m0.reshape(nseq, 2, H, 1, 1)]
    return pl.pallas_call(
        functools.partial(_mlstm_body, Tc=Tc, nc=nc, k_scale=dh ** -0.5, dh=dh),
        grid=(2, nseq, nc),
        in_specs=in_specs,
        out_specs=[pl.BlockSpec((None, Tc, D_B), lambda d, s, c: (d, rb(d, s, c), 0)),
                   pl.BlockSpec((None, None, H, dh, dh), st_map),
                   pl.BlockSpec((None, None, H, 1, dh), st_map),
                   pl.BlockSpec((None, None, H, 1, 1), st_map)],
        out_shape=[jax.ShapeDtypeStruct((2, nseq * L, D_B), F32),
                   jax.ShapeDtypeStruct((nseq, 2, H, dh, dh), F32),
                   jax.ShapeDtypeStruct((nseq, 2, H, 1, dh), F32),
                   jax.ShapeDtypeStruct((nseq, 2, H, 1, 1), F32)],
        scratch_shapes=[pltpu.VMEM((H, dh, dh), F32), pltpu.VMEM((H, 1, dh), F32), pltpu.VMEM((H, 1, 1), F32)],
        compiler_params=_cparams(3),
        name="mlstm",
    )(*args)


def _pair_specs(pair, block, tm):
    npt = pair[0].shape[-2] // tm
    (shape, _), mk = block(0), block
    return npt, [pl.BlockSpec(shape, lambda i, *rest: mk(jnp.minimum(i, npt - 1), *rest)[1]),
                 pl.BlockSpec(shape, lambda i, *rest: mk(jnp.maximum(i - npt, 0), *rest)[1])]


def _pick(p_ref, s_ref, npt):
    return jnp.where(pl.program_id(0) < npt, p_ref[...], s_ref[...])


def _mlstm_out_body(hp_ref, hs_ref, o_ref, g_ref, y_ref, *, npt):
    h = _pick(hp_ref, hs_ref, npt)
    hm = h[0] + h[1]
    mu = jnp.mean(hm, axis=1, keepdims=True)
    xc = hm - mu
    var = jnp.mean(xc * xc, axis=1, keepdims=True)
    y_ref[...] = (_sigmoid(o_ref[...]) * (xc * lax.rsqrt(var + LN_EPS) * g_ref[...])).astype(y_ref.dtype)


def _mlstm_out(h_pair, proj, col0, norm_g, tm):
    T = proj.shape[0]
    D_B = h_pair[0].shape[-1]
    dh = D_B // B_HEADS
    cb0 = col0 // dh
    npt, h_specs = _pair_specs(h_pair, lambda r, h=0: ((2, tm, dh), (0, r, h)), tm)
    return pl.pallas_call(
        functools.partial(_mlstm_out_body, npt=npt),
        grid=(T // tm, B_HEADS),
        in_specs=h_specs + [pl.BlockSpec((tm, dh), lambda i, h: (i, cb0 + h)),
                            pl.BlockSpec((1, dh), lambda i, h: (0, h))],
        out_specs=pl.BlockSpec((tm, dh), lambda i, h: (i, h)),
        out_shape=jax.ShapeDtypeStruct((T, D_B), BF16),
        compiler_params=_cparams(2),
        name="mlstm_out",
    )(*h_pair, proj, norm_g.reshape(1, D_B))


def _gla_body(q_ref, k_ref, v_ref, lr_ref, gw_ref, gb_ref, s0_ref, o_ref, so_ref, ST_s,
              *, Tc, nc, q_scale, dk, dv):
    d = pl.program_id(0)
    c = pl.program_id(2)
    fwd = d == 0
    H = C_HEADS

    @pl.when(c == 0)
    def _():
        for hd in range(H):
            ST_s[hd] = s0_ref[hd].T

    z = jnp.dot(lr_ref[...], gw_ref[...], precision=HI, preferred_element_type=F32) + gb_ref[...]
    loga = _log_sigmoid(z) * (1.0 / GLA_TAU)
    tri = _tri_mask(Tc, d)
    b = jnp.dot(tri.astype(F32), loga, precision=HI, preferred_element_type=F32)
    qs = (q_ref[...] * q_scale * jnp.exp(b)).astype(BF16)
    kf = k_ref[...]
    ke = (kf * jnp.exp(-b)).astype(BF16)
    bL = jnp.where(fwd, b[Tc - 1:Tc, :], b[0:1, :])
    kd = (kf * jnp.exp(bL - b)).astype(BF16)
    ebL = jnp.exp(bL)
    for hd in range(H):
        ks = slice(hd * dk, (hd + 1) * dk)
        vs = slice(hd * dv, (hd + 1) * dv)
        vb = v_ref[:, vs].astype(BF16)
        att = jnp.where(tri, lax.dot_general(qs[:, ks], ke[:, ks], NT_DIMS, preferred_element_type=F32), 0.0)
        ST = ST_s[hd]
        inter = lax.dot_general(qs[:, ks], ST.astype(BF16), NT_DIMS, preferred_element_type=F32)
        o_ref[:, vs] = inter + jnp.dot(att.astype(BF16), vb, preferred_element_type=F32)
        ST_s[hd] = ST * ebL[:, ks] + lax.dot_general(vb, kd[:, ks], TN_DIMS, preferred_element_type=F32)

    @pl.when(c == nc - 1)
    def _():
        for hd in range(H):
            so_ref[hd] = ST_s[hd].T


def _gla(proj, lr, gw_pad, gate_b, row0, nseq, L, s0):
    H = C_HEADS
    dk, dv = s0.shape[-2:]
    DK, DV = H * dk, H * dv
    Tc = min(SCAN_CHUNK, L)
    nc = L // Tc
    rb0 = row0 // Tc

    def rb(d, s, c):
        return s * nc + jnp.where(d == 0, c, nc - 1 - c)

    st_map = lambda d, s, c: (s, d, 0, 0, 0)
    in_specs = [pl.BlockSpec((Tc, DK), lambda d, s, c: (rb0 + rb(d, s, c), 0)),
                pl.BlockSpec((Tc, DK), lambda d, s, c: (rb0 + rb(d, s, c), 1)),
                pl.BlockSpec((Tc, DV), lambda d, s, c: (rb0 + rb(d, s, c), (2 * DK) // DV)),
                pl.BlockSpec((Tc, LANE), lambda d, s, c: (rb0 + rb(d, s, c), 0)),
                pl.BlockSpec((None, LANE, DK), lambda d, s, c: (d, 0, 0)),
                pl.BlockSpec((None, 1, DK), lambda d, s, c: (d, 0, 0)),
                pl.BlockSpec((None, None, H, dk, dv), st_map)]
    args = [proj, proj, proj, lr, gw_pad, gate_b.reshape(2, 1, DK), s0]
    return pl.pallas_call(
        functools.partial(_gla_body, Tc=Tc, nc=nc, q_scale=dk ** -0.5, dk=dk, dv=dv),
        grid=(2, nseq, nc),
        in_specs=in_specs,
        out_specs=[pl.BlockSpec((None, Tc, DV), lambda d, s, c: (d, rb(d, s, c), 0)),
                   pl.BlockSpec((None, None, H, dk, dv), st_map)],
        out_shape=[jax.ShapeDtypeStruct((2, nseq * L, DV), F32),
                   jax.ShapeDtypeStruct((nseq, 2, H, dk, dv), F32)],
        scratch_shapes=[pltpu.VMEM((H, dv, dk), F32)],
        compiler_params=_cparams(3),
        name="gla",
    )(*args)


def _gla_out_body(op_ref, os_ref, g_ref, ng_ref, y_ref, *, npt):
    o = _pick(op_ref, os_ref, npt)
    oo = o[0] + o[1]
    ms = jnp.mean(oo * oo, axis=1, keepdims=True)
    y_ref[...] = (oo * lax.rsqrt(ms + LN_EPS) * ng_ref[...] * _silu(g_ref[...])).astype(y_ref.dtype)


def _gla_out(o_pair, proj, col0, norm_g, tm):
    T = proj.shape[0]
    DV = o_pair[0].shape[-1]
    dv = DV // C_HEADS
    cb0 = col0 // dv
    npt, o_specs = _pair_specs(o_pair, lambda r, h=0: ((2, tm, dv), (0, r, h)), tm)
    return pl.pallas_call(
        functools.partial(_gla_out_body, npt=npt),
        grid=(T // tm, C_HEADS),
        in_specs=o_specs + [pl.BlockSpec((tm, dv), lambda i, h: (i, cb0 + h)),
                            pl.BlockSpec((1, dv), lambda i, h: (0, h))],
        out_specs=pl.BlockSpec((tm, dv), lambda i, h: (i, h)),
        out_shape=jax.ShapeDtypeStruct((T, DV), BF16),
        compiler_params=_cparams(2),
        name="gla_out",
    )(*o_pair, proj, norm_g.reshape(1, DV))


def _layer_norm(z, g, b):
    mu = jnp.mean(z, axis=1, keepdims=True)
    zc = z - mu
    var = jnp.mean(zc * zc, axis=1, keepdims=True)
    return zc * lax.rsqrt(var + LN_EPS) * g + b


def _outproj_body(*refs, npts, g_row, alpha):
    n = len(npts)
    n_y = sum(1 if p is None else 2 for p in npts)
    y_refs = list(refs[:n_y])
    ws = refs[n_y:n_y + n]
    x_ref, m_ref, lg_ref, lb_ref, o_ref = refs[n_y + n:]
    acc = None
    for npt, w in zip(npts, ws):
        y = y_refs.pop(0)[...] if npt is None else _pick(y_refs.pop(0), y_refs.pop(0), npt)
        part = jnp.dot(y, w[...], preferred_element_type=F32)
        acc = part if acc is None else acc + part
    z = alpha * x_ref[...] + m_ref[g_row:g_row + 1, :] * acc
    o_ref[...] = _layer_norm(z, lg_ref[...], lb_ref[...])


def _outproj_ln(ys, ws, x, modt, ln_g, ln_b, *, g_row, alpha, tm):
    T, D = x.shape
    mod4, mod_map = modt
    y_specs, y_args, npts = [], [], []
    for y in ys:
        if isinstance(y, tuple):
            K = y[0].shape[1]
            npt, specs = _pair_specs(y, lambda r, K=K: ((tm, K), (r, 0)), tm)
            y_specs += specs
            y_args += list(y)
            npts.append(npt)
        else:
            y_specs.append(pl.BlockSpec((tm, y.shape[1]), lambda i: (i, 0)))
            y_args.append(y)
            npts.append(None)
    in_specs = (y_specs
                + [pl.BlockSpec(w.shape, lambda i: (0, 0)) for w in ws]
                + [pl.BlockSpec((tm, D), lambda i: (i, 0)),
                   pl.BlockSpec((None, None, 6, D), lambda i: mod_map(i)),
                   pl.BlockSpec((1, D), lambda i: (0, 0)),
                   pl.BlockSpec((1, D), lambda i: (0, 0))])
    return pl.pallas_call(
        functools.partial(_outproj_body, npts=tuple(npts), g_row=g_row, alpha=alpha),
        grid=(T // tm,),
        in_specs=in_specs,
        out_specs=pl.BlockSpec((tm, D), lambda i: (i, 0)),
        out_shape=jax.ShapeDtypeStruct((T, D), F32),
        compiler_params=_cparams(1),
        name="outproj_ln",
    )(*y_args, *ws, x, mod4, ln_g.reshape(1, D), ln_b.reshape(1, D))


def _router_body(x_ref, m_ref, w_ref, b_ref, h_ref, ids_ref, gts_ref, *, sh_row, sc_row):
    h = x_ref[...] * (1.0 + m_ref[sc_row:sc_row + 1, :]) + m_ref[sh_row:sh_row + 1, :]
    h_ref[...] = h.astype(h_ref.dtype)
    logits = lax.dot_general(w_ref[...], h, NT_DIMS, precision=HI, preferred_element_type=F32) + b_ref[...]
    E, tm = logits.shape
    row = lax.broadcasted_iota(jnp.int32, (E, tm), 0)
    mx = jnp.max(logits, axis=0, keepdims=True)
    ex = jnp.exp(logits - mx)
    p = ex / jnp.sum(ex, axis=0, keepdims=True)
    grp = row // (E // N_GROUPS)
    best = None
    for g in range(N_GROUPS):
        pg = jnp.where(grp == g, p, -1.0)
        v1 = jnp.max(pg, axis=0, keepdims=True)
        i1 = jnp.min(jnp.where(pg == v1, row, E), axis=0, keepdims=True)
        pg2 = jnp.where(row == i1, -1.0, pg)
        v2 = jnp.max(pg2, axis=0, keepdims=True)
        i2 = jnp.min(jnp.where(pg2 == v2, row, E), axis=0, keepdims=True)
        score = v1 + v2
        if best is None:
            best = (score, v1, i1, v2, i2)
        else:
            take = score > best[0]
            best = tuple(jnp.where(take, n, o) for n, o in zip((score, v1, i1, v2, i2), best))
    _, v1, i1, v2, i2 = best
    tot = v1 + v2
    slot = lax.broadcasted_iota(jnp.int32, (SUBLANE, tm), 0)
    ids_ref[...] = jnp.where(slot == 0, i1, jnp.where(slot == 1, i2, 0))
    gts_ref[...] = jnp.where(slot == 0, v1 / tot, jnp.where(slot == 1, v2 / tot, 0.0))


def _router(x, modt, router_w, router_b, *, sh_row, sc_row, tm):
    T, D = x.shape
    E = router_w.shape[1]
    mod4, mod_map = modt
    return pl.pallas_call(
        functools.partial(_router_body, sh_row=sh_row, sc_row=sc_row),
        grid=(T // tm,),
        in_specs=[pl.BlockSpec((tm, D), lambda i: (i, 0)),
                  pl.BlockSpec((None, None, 6, D), lambda i: mod_map(i)),
                  pl.BlockSpec((E, D), lambda i: (0, 0)),
                  pl.BlockSpec((E, 1), lambda i: (0, 0))],
        out_specs=[pl.BlockSpec((tm, D), lambda i: (i, 0)),
                   pl.BlockSpec((SUBLANE, tm), lambda i: (0, i)),
                   pl.BlockSpec((SUBLANE, tm), lambda i: (0, i))],
        out_shape=[jax.ShapeDtypeStruct((T, D), F32),
                   jax.ShapeDtypeStruct((SUBLANE, T), jnp.int32),
                   jax.ShapeDtypeStruct((SUBLANE, T), F32)],
        compiler_params=_cparams(1),
        name="router",
    )(x, mod4, router_w.T, router_b.reshape(E, 1))


class _RowGather:
    def __init__(self, src_hbm, idx_hbm, idx_s, bufs, isem, gsem, tm):
        self.src, self.idx_hbm, self.idx_s, self.bufs = src_hbm, idx_hbm, idx_s, bufs
        self.isem, self.gsem, self.tm = isem, gsem, tm

    def _idx_copy(self, tile, slot):
        return pltpu.make_async_copy(self.idx_hbm.at[tile], self.idx_s.at[slot], self.isem.at[slot])

    def _row_copy(self, slot, a, r):
        row = self.idx_s[slot, a, r]
        return pltpu.make_async_copy(self.src.at[pl.ds(row, 1), :], self.bufs[a].at[slot, pl.ds(r, 1), :],
                                     self.gsem.at[slot, a])

    def _rows(self, slot, start):
        def body(r, carry):
            for a in range(len(self.bufs)):
                cp = self._row_copy(slot, a, r)
                cp.start() if start else cp.wait()
            return carry
        lax.fori_loop(0, self.tm, body, 0, unroll=8)

    def step(self, i, n_tiles, cur_valid, next_valid):
        slot = i % 2

        @pl.when(i == 0)
        def _():
            self._idx_copy(0, 0).start()
            self._idx_copy(0, 0).wait()
            self._rows(0, True)
            if n_tiles > 1:
                self._idx_copy(1, 1).start()

        @pl.when(i + 1 < n_tiles)
        def _():
            self._idx_copy(i + 1, 1 - slot).wait()

            @pl.when(next_valid)
            def _():
                self._rows(1 - slot, True)

        @pl.when(cur_valid)
        def _():
            self._rows(slot, False)

        @pl.when(i + 2 < n_tiles)
        def _():
            self._idx_copy(i + 2, slot).start()


def _ffn_body(te_ref, va_ref, idx_hbm, h_hbm, wi_ref, wo_ref, o_ref, xbuf, idx_s, isem, gsem, *, F, tm, n_tiles):
    i = pl.program_id(0)
    valid = va_ref[i] != 0
    nxt = va_ref[jnp.minimum(i + 1, n_tiles - 1)] != 0
    _RowGather(h_hbm, idx_hbm, idx_s, [xbuf], isem, gsem, tm).step(i, n_tiles, valid, nxt)

    @pl.when(valid)
    def _():
        x = xbuf[i % 2].astype(BF16)
        uw = jnp.dot(x, wi_ref[...], preferred_element_type=F32)
        act = (_silu(uw[:, :F]) * uw[:, F:]).astype(BF16)
        o_ref[...] = jnp.dot(act, wo_ref[...], preferred_element_type=F32)

    @pl.when(jnp.logical_not(valid))
    def _():
        o_ref[...] = jnp.zeros_like(o_ref)


def _expert_ffn(h, src_tok, w_in, w_out, layer, tile_e, tile_valid):
    T, D = h.shape
    n_tiles, _, tm = src_tok.shape
    F = w_out.shape[2]
    return pl.pallas_call(
        functools.partial(_ffn_body, F=F, tm=tm, n_tiles=n_tiles),
        grid_spec=pltpu.PrefetchScalarGridSpec(
            num_scalar_prefetch=2,
            grid=(n_tiles,),
            in_specs=[pl.BlockSpec(memory_space=pl.ANY),
                      pl.BlockSpec(memory_space=pl.ANY),
                      pl.BlockSpec((None, None, D, 2 * F), lambda i, te, va: (layer, te[i], 0, 0)),
                      pl.BlockSpec((None, None, F, D), lambda i, te, va: (layer, te[i], 0, 0))],
            out_specs=pl.BlockSpec((tm, D), lambda i, te, va: (i, 0)),
            scratch_shapes=[pltpu.VMEM((2, tm, D), F32), pltpu.SMEM((2, 1, tm), jnp.int32),
                            pltpu.SemaphoreType.DMA((2,)), pltpu.SemaphoreType.DMA((2, 1))]),
        out_shape=jax.ShapeDtypeStruct((n_tiles * tm, D), F32),
        compiler_params=_cparams(1),
        name="expert_ffn",
    )(tile_e, tile_valid, src_tok, h, w_in, w_out)


def _combine_body(x_ref, idx_hbm, y_hbm, gt_ref, m_ref, lg_ref, lb_ref, o_ref, y0buf, y1buf, idx_s, isem, gsem,
                  *, g_row, alpha, tm, n_tiles):
    i = pl.program_id(0)
    always = i >= 0
    _RowGather(y_hbm, idx_hbm, idx_s, [y0buf, y1buf], isem, gsem, tm).step(i, n_tiles, always, always)
    gt = gt_ref[...]
    slot = i % 2
    moe = gt[:, 0:1] * y0buf[slot] + gt[:, 1:2] * y1buf[slot]
    z = alpha * x_ref[...] + m_ref[g_row:g_row + 1, :] * moe
    o_ref[...] = _layer_norm(z, lg_ref[...], lb_ref[...])


def _combine_ln(x, ys, dest, gts, modt, ln_g, ln_b, *, g_row, alpha):
    T, D = x.shape
    n_tiles, _, tm = dest.shape
    mod4, mod_map = modt
    row = pl.BlockSpec((tm, D), lambda i: (i, 0))
    vec = pl.BlockSpec((1, D), lambda i: (0, 0))
    hbm = pl.BlockSpec(memory_space=pl.ANY)
    return pl.pallas_call(
        functools.partial(_combine_body, g_row=g_row, alpha=alpha, tm=tm, n_tiles=n_tiles),
        grid=(n_tiles,),
        in_specs=[row, hbm, hbm, pl.BlockSpec((tm, LANE), lambda i: (i, 0)),
                  pl.BlockSpec((None, None, 6, D), lambda i: mod_map(i)), vec, vec],
        out_specs=row,
        out_shape=jax.ShapeDtypeStruct((T, D), F32),
        scratch_shapes=[pltpu.VMEM((2, tm, D), F32), pltpu.VMEM((2, tm, D), F32), pltpu.SMEM((2, 2, tm), jnp.int32),
                        pltpu.SemaphoreType.DMA((2,)), pltpu.SemaphoreType.DMA((2, 2))],
        compiler_params=_cparams(1),
        name="combine_ln",
    )(x, dest, ys, gts, mod4, ln_g.reshape(1, D), ln_b.reshape(1, D))


def _route_meta(ids, tm, tm_tok):
    T = ids.shape[0]
    E = N_EXPERTS
    e_flat = ids.reshape(-1)
    onehot = (e_flat[:, None] == jnp.arange(E, dtype=jnp.int32)[None, :]).astype(jnp.int32)
    csum = jnp.cumsum(onehot, axis=0)
    rank = jnp.sum(csum * onehot, axis=1) - 1
    counts = csum[-1]
    padded = ((counts + tm - 1) // tm) * tm
    ends = jnp.cumsum(padded)
    starts = ends - padded
    dest = jnp.sum(starts[None, :] * onehot, axis=1) + rank
    n_rows = 2 * T + E * tm
    src_tok = jnp.zeros((n_rows,), jnp.int32).at[dest].set(jnp.arange(2 * T, dtype=jnp.int32) // 2)
    tile_start = jnp.arange(n_rows // tm, dtype=jnp.int32) * tm
    valid = (tile_start < ends[-1]).astype(jnp.int32)
    probe = jnp.minimum(tile_start, ends[-1] - 1)
    tile_e = jnp.sum((ends[None, :] <= probe[:, None]).astype(jnp.int32), axis=1)
    dest_t = dest.reshape(T // tm_tok, tm_tok, 2).transpose(0, 2, 1)
    return dest_t, src_tok.reshape(n_rows // tm, 1, tm), jnp.minimum(tile_e, E - 1), valid


def _pos_embed_2d(n_tokens, dim):
    rows = n_tokens // GRID_W
    quarter = dim // 4
    freqs = jnp.exp(-math.log(10000.0) * jnp.arange(quarter, dtype=F32) / quarter)
    r = jnp.broadcast_to(jnp.arange(rows, dtype=F32)[:, None], (rows, GRID_W)).reshape(-1)
    col = jnp.broadcast_to(jnp.arange(GRID_W, dtype=F32)[None, :], (rows, GRID_W)).reshape(-1)
    ar = r[:, None] * freqs
    ac = col[:, None] * freqs
    return jnp.concatenate([jnp.sin(ar), jnp.cos(ar), jnp.sin(ac), jnp.cos(ac)], -1)


def _pad_cols(w, n):
    return jnp.pad(w, ((0, 0),) * (w.ndim - 1) + ((0, n - w.shape[-1]),))


def kernel(x_prompt, x_sample, state_lru, state_mlstm_c, state_mlstm_n, state_mlstm_m, state_gla, c, c_ctx, mod_w, mod_b, ln_g, ln_b, even_w_in, even_w_out, lru_conv_w, lru_conv_b, lru_gate_w, lru_gate_b, lru_lambda, mlstm_gate_b, mlstm_norm_g, odd_w_in, odd_w_out, gla_gate_w, gla_gate_b, gla_norm_g, router_w, router_b, moe_w_in, moe_w_out):
    Bp, Lp, D = x_prompt.shape
    Bs, Ls, _ = x_sample.shape
    depth = mod_w.shape[0]
    Tp, Ts = Bp * Lp, Bs * Ls
    T = Tp + Ts
    alpha = (2 * depth) ** 0.25
    D_A = lru_conv_w.shape[-1]
    D_B = mlstm_norm_g.shape[-1]
    DK = gla_gate_w.shape[-1]
    DV = gla_norm_g.shape[-1]
    tm = next(t for t in (512, 256, 128) if Tp % t == 0 and Ls % t == 0)
    tn = 1024

    n_cond = 1 + Bs
    R = -(-n_cond // SUBLANE) * SUBLANE
    cond = jnp.zeros((R, D), F32).at[0].set(c_ctx).at[1:n_cond].set(c)
    mod_all = _modulation(cond, mod_w, mod_b)
    mod4 = mod_all.reshape(depth, R, 6, D)
    npt, per_seq = Tp // tm, Ls // tm

    x = _embed(x_prompt.reshape(Tp, D), x_sample.reshape(Ts, D), _pos_embed_2d(Ls, D), tm)

    moe_w_in_b = moe_w_in.astype(BF16)
    moe_w_out_b = moe_w_out.astype(BF16)
    groups = ((0, Bp, Lp), (Tp, Bs, Ls))

    s_lru, s_c, s_n, s_m, s_gla = [], [], [], [], []
    for l in range(depth):
        j = l // 2
        mt = (mod4, lambda i, l=l: (l, jnp.where(i < npt, 0, 1 + (i - npt) // per_seq), 0, 0))
        if l % 2 == 0:
            w_in = even_w_in[j]
            n_main = 2 * D_A + 4 * D_B
            proj = _proj(x, mt, w_in[:, :n_main].astype(BF16), sh_row=0, sc_row=1, tm=tm, tn=tn)
            gates = _proj(x, mt, _pad_cols(w_in[:, n_main:], LANE), sh_row=0, sc_row=1, tm=tm, tn=LANE, hp=True)
            gate_bias = _pad_cols(mlstm_gate_b[j].reshape(1, -1), LANE)
            ya, hs = [], []
            for gi, (row0, nseq, L) in enumerate(groups):
                if gi == 0:
                    lru0 = jnp.zeros((nseq, 2, D_A), F32)
                    c0 = jnp.zeros((nseq, 2, B_HEADS, D_B // B_HEADS, D_B // B_HEADS), F32)
                    n0 = jnp.zeros((nseq, 2, B_HEADS, D_B // B_HEADS), F32)
                    m0 = jnp.zeros((nseq, 2, B_HEADS), F32)
                else:
                    lru0, c0, n0, m0 = state_lru[:, j], state_mlstm_c[:, j], state_mlstm_n[:, j], state_mlstm_m[:, j]
                y_a, st = _lru(proj, row0, nseq, L, lru_conv_w[j], lru_conv_b[j], lru_gate_w[j], lru_gate_b[j],
                               lru_lambda[j], lru0)
                h, C, n, m = _mlstm(proj, gates, gate_bias, row0, nseq, L, 2 * D_A, c0, n0, m0)
                ya.append(y_a)
                hs.append(h)
                if gi == 0:
                    s_lru.append(st)
                    s_c.append(C)
                    s_n.append(n.reshape(nseq, 2, B_HEADS, -1))
                    s_m.append(m.reshape(nseq, 2, B_HEADS))
            y_b = _mlstm_out(tuple(hs), proj, 2 * D_A + 3 * D_B, mlstm_norm_g[j], tm)
            w_out = even_w_out[j].astype(BF16)
            x = _outproj_ln([tuple(ya), y_b], [w_out[:D_A], w_out[D_A:]], x, mt, ln_g[l, 0], ln_b[l, 0],
                            g_row=2, alpha=alpha, tm=tm)
        else:
            w_in = odd_w_in[j]
            n_main = 2 * DK + 2 * DV
            proj = _proj(x, mt, w_in[:, :n_main].astype(BF16), sh_row=0, sc_row=1, tm=tm, tn=tn)
            lr = _proj(x, mt, _pad_cols(w_in[:, n_main:], LANE), sh_row=0, sc_row=1, tm=tm, tn=LANE, hp=True)
            gw_pad = jnp.zeros((2, LANE, DK), F32)
            for dd in range(2):
                gw_pad = gw_pad.at[dd, dd * GLA_RANK:(dd + 1) * GLA_RANK].set(gla_gate_w[j, dd])
            os_ = []
            for gi, (row0, nseq, L) in enumerate(groups):
                s0 = (jnp.zeros((nseq, 2, C_HEADS, DK // C_HEADS, DV // C_HEADS), F32) if gi == 0
                      else state_gla[:, j])
                o, S = _gla(proj, lr, gw_pad, gla_gate_b[j], row0, nseq, L, s0)
                os_.append(o)
                if gi == 0:
                    s_gla.append(S)
            y = _gla_out(tuple(os_), proj, 2 * DK + DV, gla_norm_g[j], tm)
            x = _outproj_ln([y], [odd_w_out[j].astype(BF16)], x, mt, ln_g[l, 0], ln_b[l, 0],
                            g_row=2, alpha=alpha, tm=tm)

        hf, ids, gts = _router(x, mt, router_w, router_b, sh_row=3, sc_row=4, tm=tm)
        gts = _pad_cols(gts[:2].T, LANE)
        dest, src_tok, tile_e, tile_valid = _route_meta(ids[:2].T, MOE_TM, tm)
        ys = _expert_ffn(hf, src_tok, moe_w_in_b, moe_w_out_b, l, tile_e, tile_valid)
        x = _combine_ln(x, ys, dest, gts, mt, ln_g[l, 1], ln_b[l, 1], g_row=5, alpha=alpha)

    y_prompt = x[:Tp].reshape(Bp, Lp, D)
    y_sample = x[Tp:].reshape(Bs, Ls, D)
    return (y_prompt, y_sample, jnp.stack(s_lru, 1), jnp.stack(s_c, 1), jnp.stack(s_n, 1),
            jnp.stack(s_m, 1), jnp.stack(s_gla, 1))
```

```python
import functools
import math

import jax
import jax.numpy as jnp
from jax import lax
from jax.experimental import pallas as pl
from jax.experimental.pallas import tpu as pltpu

F32 = jnp.float32
BF16 = jnp.bfloat16
HI = lax.Precision.HIGHEST

LN_EPS = 1e-5
LRU_C = 8.0
A_BLOCKS = 8
B_HEADS = 4
C_HEADS = 4
GLA_RANK = 16
GLA_TAU = 16.0
N_EXPERTS = 16
N_GROUPS = 4
GRID_W = 64
LANE = 128
SUBLANE = 8
SCAN_CHUNK = 128
MOE_TM = 512
TOK_ROWS = 16
VMEM_LIMIT = 56 * 1024 * 1024

NT_DIMS = (((1,), (1,)), ((), ()))
TN_DIMS = (((0,), (0,)), ((), ()))


def _cparams(n_axes):
    return pltpu.CompilerParams(dimension_semantics=("arbitrary",) * n_axes,
                                vmem_limit_bytes=VMEM_LIMIT)


def _sigmoid(x):
    return 1.0 / (1.0 + jnp.exp(-x))


def _silu(x):
    return x * _sigmoid(x)


def _log_sigmoid(x):
    return jnp.minimum(x, 0.0) - jnp.log1p(jnp.exp(-jnp.abs(x)))


def _gelu_tanh(x):
    return 0.5 * x * (1.0 + jnp.tanh(math.sqrt(2.0 / math.pi) * (x + 0.044715 * (x * x * x))))


def _mod_body(c_ref, w_ref, b_ref, o_ref):
    c = c_ref[...]
    o_ref[...] = jnp.dot(_silu(c), w_ref[...], precision=HI, preferred_element_type=F32) + b_ref[...]


def _modulation(cond, mod_w, mod_b):
    R, D = cond.shape
    nl, _, N = mod_w.shape
    tn = 1024
    return pl.pallas_call(
        _mod_body,
        grid=(nl, N // tn),
        in_specs=[pl.BlockSpec((R, D), lambda l, j: (0, 0)),
                  pl.BlockSpec((None, D, tn), lambda l, j: (l, 0, j)),
                  pl.BlockSpec((None, 1, tn), lambda l, j: (l, 0, j))],
        out_specs=pl.BlockSpec((None, R, tn), lambda l, j: (l, 0, j)),
        out_shape=jax.ShapeDtypeStruct((nl, R, N), F32),
        compiler_params=_cparams(2),
        name="modulation",
    )(cond, mod_w, mod_b.reshape(nl, 1, N))


def _embed_body(xp_ref, xs_ref, pos_ref, o_ref, *, n_prompt_tiles):
    i = pl.program_id(0)

    @pl.when(i < n_prompt_tiles)
    def _():
        o_ref[...] = xp_ref[...]

    @pl.when(i >= n_prompt_tiles)
    def _():
        o_ref[...] = xs_ref[...] + pos_ref[...]


def _embed(xp, xs, pos, tm):
    Tp, D = xp.shape
    Ts = xs.shape[0]
    Ls = pos.shape[0]
    npt, nst, npos = Tp // tm, Ts // tm, Ls // tm
    return pl.pallas_call(
        functools.partial(_embed_body, n_prompt_tiles=npt),
        grid=(npt + nst,),
        in_specs=[pl.BlockSpec((tm, D), lambda i: (jnp.minimum(i, npt - 1), 0)),
                  pl.BlockSpec((tm, D), lambda i: (jnp.maximum(i - npt, 0), 0)),
                  pl.BlockSpec((tm, D), lambda i: (jnp.maximum(i - npt, 0) % npos, 0))],
        out_specs=pl.BlockSpec((tm, D), lambda i: (i, 0)),
        out_shape=jax.ShapeDtypeStruct((Tp + Ts, D), F32),
        compiler_params=_cparams(1),
        name="embed",
    )(xp, xs, pos)


def _proj_body(x_ref, m_ref, w_ref, o_ref, h_ref, *, sh_row, sc_row, hp):
    @pl.when(pl.program_id(1) == 0)
    def _():
        h = x_ref[...] * (1.0 + m_ref[sc_row:sc_row + 1, :]) + m_ref[sh_row:sh_row + 1, :]
        h_ref[...] = h.astype(h_ref.dtype)

    if hp:
        o = jnp.dot(h_ref[...], w_ref[...], precision=HI, preferred_element_type=F32)
    else:
        o = jnp.dot(h_ref[...], w_ref[...], preferred_element_type=F32)
    o_ref[...] = o.astype(o_ref.dtype)


def _proj(x, modt, w, *, sh_row, sc_row, tm, tn, hp=False):
    T, D = x.shape
    N = w.shape[1]
    mod4, mod_map = modt[0], modt[1](tm)
    return pl.pallas_call(
        functools.partial(_proj_body, sh_row=sh_row, sc_row=sc_row, hp=hp),
        grid=(T // tm, N // tn),
        in_specs=[pl.BlockSpec((tm, D), lambda i, j: (i, 0)),
                  pl.BlockSpec((None, None, 6, D), lambda i, j: mod_map(i)),
                  pl.BlockSpec((D, tn), lambda i, j: (0, j))],
        out_specs=pl.BlockSpec((tm, tn), lambda i, j: (i, j)),
        out_shape=jax.ShapeDtypeStruct((T, N), F32),
        scratch_shapes=[pltpu.VMEM((tm, D), F32 if hp else BF16)],
        compiler_params=_cparams(2),
        name="proj_hp" if hp else "proj",
    )(x, mod4, w)


def _lru_body(ag_ref, ax_ref, cw_ref, cb_ref, gw_ref, gb_ref, lam_ref, h0_ref, y_ref, st_ref,
              af_s, uf_s, ab_s, ub_s, hf_s, hb_s, *, L):
    x = ax_ref[...]
    row = lax.broadcasted_iota(jnp.int32, (L, LANE), 0)
    xm1 = jnp.where(row >= 1, pltpu.roll(x, 1, 0), 0.0)
    xp1 = jnp.where(row < L - 1, pltpu.roll(x, L - 1, 0), 0.0)
    xp2 = jnp.where(row < L - 2, pltpu.roll(x, L - 2, 0), 0.0)
    xc = cb_ref[...] + xm1 * cw_ref[0:1, :] + x * cw_ref[1:2, :] + xp1 * cw_ref[2:3, :] + xp2 * cw_ref[3:4, :]
    xcb = xc.astype(BF16)
    nj = L // SUBLANE
    sub = lax.broadcasted_iota(jnp.int32, (nj, SUBLANE, LANE), 1)

    for d, (a_s, u_s) in enumerate(((af_s, uf_s), (ab_s, ub_s))):
        g = jnp.dot(xcb, gw_ref[d], preferred_element_type=F32) + gb_ref[d]
        r = _sigmoid(g[:, :LANE])
        ig = _sigmoid(g[:, LANE:])
        nlam = -lam_ref[d]
        softplus = jnp.maximum(nlam, 0.0) + jnp.log1p(jnp.exp(-jnp.abs(nlam)))
        log_a = (-LRU_C * softplus) * r
        a = jnp.exp(log_a)
        u = jnp.sqrt(jnp.tanh(-log_a) * (1.0 + a * a)) * ig * xc
        a3 = a.reshape(nj, SUBLANE, LANE)
        u3 = u.reshape(nj, SUBLANE, LANE)
        for k in (1, 2, 4):
            if d == 0:
                sh, keep = k, sub >= k
            else:
                sh, keep = SUBLANE - k, sub < SUBLANE - k
            a_sh = pltpu.roll(a3, sh, 1)
            u_sh = pltpu.roll(u3, sh, 1)
            u3 = jnp.where(keep, a3 * u_sh + u3, u3)
            a3 = jnp.where(keep, a3 * a_sh, a3)
        a_s[...] = a3.reshape(L, LANE)
        u_s[...] = u3.reshape(L, LANE)

    def carry(j, hs):
        hf, hb = hs
        rf = pl.multiple_of(j * SUBLANE, SUBLANE)
        rb = pl.multiple_of((nj - 1 - j) * SUBLANE, SUBLANE)
        of = af_s[pl.ds(rf, SUBLANE), :] * hf + uf_s[pl.ds(rf, SUBLANE), :]
        ob = ab_s[pl.ds(rb, SUBLANE), :] * hb + ub_s[pl.ds(rb, SUBLANE), :]
        hf_s[pl.ds(rf, SUBLANE), :] = of
        hb_s[pl.ds(rb, SUBLANE), :] = ob
        return (jnp.broadcast_to(of[SUBLANE - 1:SUBLANE, :], (SUBLANE, LANE)),
                jnp.broadcast_to(ob[0:1, :], (SUBLANE, LANE)))

    h0f = jnp.broadcast_to(h0_ref[0:1, :], (SUBLANE, LANE))
    h0b = jnp.broadcast_to(h0_ref[1:2, :], (SUBLANE, LANE))
    lax.fori_loop(0, nj, carry, (h0f, h0b))
    y_ref[...] = (_gelu_tanh(ag_ref[...]) * (hf_s[...] + hb_s[...])).astype(y_ref.dtype)
    st_ref[0:1, :] = hf_s[L - 1:L, :]
    st_ref[1:2, :] = hb_s[0:1, :]


def _lru(proj, row0, nseq, L, conv_w, conv_b, gate_w, gate_b, lam, h0):
    D_A = conv_w.shape[1]
    nb = D_A // LANE
    rb0 = row0 // L
    in_specs = [pl.BlockSpec((L, LANE), lambda s, h: (rb0 + s, h)),
                pl.BlockSpec((L, LANE), lambda s, h: (rb0 + s, nb + h)),
                pl.BlockSpec((4, LANE), lambda s, h: (0, h)),
                pl.BlockSpec((1, LANE), lambda s, h: (0, h)),
                pl.BlockSpec((2, None, LANE, 2 * LANE), lambda s, h: (0, h, 0, 0)),
                pl.BlockSpec((2, None, 1, 2 * LANE), lambda s, h: (0, h, 0, 0)),
                pl.BlockSpec((2, None, 1, LANE), lambda s, h: (0, h, 0, 0)),
                pl.BlockSpec((None, 2, LANE), lambda s, h: (s, 0, h))]
    args = [proj, proj, conv_w, conv_b.reshape(1, D_A), gate_w.astype(BF16),
            gate_b.reshape(2, nb, 1, 2 * LANE), lam.reshape(2, nb, 1, LANE), h0]
    return pl.pallas_call(
        functools.partial(_lru_body, L=L),
        grid=(nseq, nb),
        in_specs=in_specs,
        out_specs=[pl.BlockSpec((L, LANE), lambda s, h: (s, h)),
                   pl.BlockSpec((None, 2, LANE), lambda s, h: (s, 0, h))],
        out_shape=[jax.ShapeDtypeStruct((nseq * L, D_A), BF16),
                   jax.ShapeDtypeStruct((nseq, 2, D_A), F32)],
        scratch_shapes=[pltpu.VMEM((L, LANE), F32)] * 6,
        compiler_params=_cparams(2),
        name="rglru",
    )(*args)


def _tri_mask(Tc, d):
    r = lax.broadcasted_iota(jnp.int32, (Tc, Tc), 0)
    c = lax.broadcasted_iota(jnp.int32, (Tc, Tc), 1)
    return (c - r) * (1 - 2 * d) <= 0


def _mlstm_body(q_ref, k_ref, v_ref, g_ref, gb_ref, c0_ref, n0_ref, m0_ref,
                h_ref, co_ref, no_ref, mo_ref, C_s, n_s, m_s, *, Tc, nc, k_scale, dh):
    d = pl.program_id(0)
    c = pl.program_id(2)
    fwd = d == 0
    H = B_HEADS

    @pl.when(c == 0)
    def _():
        C_s[...] = c0_ref[...]
        n_s[...] = n0_ref[...]
        m_s[...] = m0_ref[...]

    lane = lax.broadcasted_iota(jnp.int32, (Tc, LANE), 1)
    G = g_ref[...] + gb_ref[...]
    tri = _tri_mask(Tc, d)
    Bm = jnp.dot(tri.astype(F32), _log_sigmoid(G), precision=HI, preferred_element_type=F32)
    col0 = d * (2 * H)
    X = jnp.where(lane >= col0 + H, Bm, G)
    srow = lax.broadcasted_iota(jnp.int32, (2 * H, LANE), 0)
    slane = lax.broadcasted_iota(jnp.int32, (2 * H, LANE), 1)
    sel = (slane == col0 + srow).astype(F32)
    R = lax.dot_general(sel, X, NT_DIMS, precision=HI, preferred_element_type=F32)
    for hd in range(H):
        i_col = jnp.sum(jnp.where(lane == col0 + hd, X, 0.0), axis=1, keepdims=True)
        b_col = jnp.sum(jnp.where(lane == col0 + H + hd, X, 0.0), axis=1, keepdims=True)
        i_row = R[hd:hd + 1, :]
        b_row = R[H + hd:H + hd + 1, :]
        dmat = jnp.where(tri, b_col - b_row + i_row, -jnp.inf)
        m_prev = m_s[hd]
        m_inter = b_col + m_prev
        m_t = jnp.maximum(m_inter, jnp.max(dmat, axis=1, keepdims=True))
        cs = slice(hd * dh, (hd + 1) * dh)
        qf = q_ref[:, cs]
        qb = qf.astype(BF16)
        kf = k_ref[:, cs] * k_scale
        kb = kf.astype(BF16)
        vb = v_ref[:, cs].astype(BF16)
        S = lax.dot_general(qb, kb, NT_DIMS, preferred_element_type=F32) * jnp.exp(dmat - m_t)
        inter_scale = jnp.exp(m_inter - m_t)
        Cm = C_s[hd]
        num = (jnp.dot(S.astype(BF16), vb, preferred_element_type=F32)
               + inter_scale * jnp.dot(qb, Cm.astype(BF16), preferred_element_type=F32))
        nv = n_s[hd]
        qn = jnp.sum(qf * nv, axis=1, keepdims=True)
        den = jnp.sum(S, axis=1, keepdims=True) + inter_scale * qn
        h_ref[:, cs] = num / jnp.maximum(jnp.abs(den), jnp.exp(-m_t))
        bL = jnp.where(fwd, b_col[Tc - 1:Tc, :], b_col[0:1, :])
        g_col = bL - b_col + i_col
        m_new = jnp.maximum(bL + m_prev, jnp.max(g_col, axis=0, keepdims=True))
        wk = jnp.exp(g_col - m_new)
        decay = jnp.exp(bL + m_prev - m_new)
        kw = kf * wk
        C_s[hd] = decay * Cm + lax.dot_general(kw.astype(BF16), vb, TN_DIMS, preferred_element_type=F32)
        n_s[hd] = decay * nv + jnp.sum(kw, axis=0, keepdims=True)
        m_s[hd] = m_new

    @pl.when(c == nc - 1)
    def _():
        co_ref[...] = C_s[...]
        no_ref[...] = n_s[...]
        mo_ref[...] = m_s[...]


def _mlstm(proj, gates, gate_bias, row0, nseq, L, col0, c0, n0, m0):
    H = B_HEADS
    dh = c0.shape[-1]
    D_B = H * dh
    Tc = min(SCAN_CHUNK, L)
    nc = L // Tc
    rb0 = row0 // Tc
    cb0 = col0 // D_B

    def rb(d, s, c):
        return s * nc + jnp.where(d == 0, c, nc - 1 - c)

    def qkv_spec(off):
        return pl.BlockSpec((Tc, D_B), lambda d, s, c: (rb0 + rb(d, s, c), cb0 + off))

    st_map = lambda d, s, c: (s, d, 0, 0, 0)
    in_specs = [qkv_spec(0), qkv_spec(1), qkv_spec(2),
                pl.BlockSpec((Tc, LANE), lambda d, s, c: (rb0 + rb(d, s, c), 0)),
                pl.BlockSpec((1, LANE), lambda d, s, c: (0, 0)),
                pl.BlockSpec((None, None, H, dh, dh), st_map),
                pl.BlockSpec((None, None, H, 1, dh), st_map),
                pl.BlockSpec((None, None, H, 1, 1), st_map)]
    args = [proj, proj, proj, gates, gate_bias, c0, n0.reshape(nseq, 2, H, 1, dh), m0.reshape(nseq, 2, H, 1, 1)]
    return pl.pallas_call(
        functools.partial(_mlstm_body, Tc=Tc, nc=nc, k_scale=dh ** -0.5, dh=dh),
        grid=(2, nseq, nc),
        in_specs=in_specs,
        out_specs=[pl.BlockSpec((None, Tc, D_B), lambda d, s, c: (d, rb(d, s, c), 0)),
                   pl.BlockSpec((None, None, H, dh, dh), st_map),
                   pl.BlockSpec((None, None, H, 1, dh), st_map),
                   pl.BlockSpec((None, None, H, 1, 1), st_map)],
        out_shape=[jax.ShapeDtypeStruct((2, nseq * L, D_B), F32),
                   jax.ShapeDtypeStruct((nseq, 2, H, dh, dh), F32),
                   jax.ShapeDtypeStruct((nseq, 2, H, 1, dh), F32),
                   jax.ShapeDtypeStruct((nseq, 2, H, 1, 1), F32)],
        scratch_shapes=[pltpu.VMEM((H, dh, dh), F32), pltpu.VMEM((H, 1, dh), F32), pltpu.VMEM((H, 1, 1), F32)],
        compiler_params=_cparams(3),
        name="mlstm",
    )(*args)


def _pair_specs(pair, block, tm):
    npt = pair[0].shape[-2] // tm
    (shape, _), mk = block(0), block
    return npt, [pl.BlockSpec(shape, lambda i, *rest: mk(jnp.minimum(i, npt - 1), *rest)[1]),
                 pl.BlockSpec(shape, lambda i, *rest: mk(jnp.maximum(i - npt, 0), *rest)[1])]


def _pick(p_ref, s_ref, npt):
    return jnp.where(pl.program_id(0) < npt, p_ref[...], s_ref[...])


def _mlstm_out_body(hp_ref, hs_ref, o_ref, g_ref, y_ref, *, npt):
    h = _pick(hp_ref, hs_ref, npt)
    hm = h[0] + h[1]
    mu = jnp.mean(hm, axis=1, keepdims=True)
    xc = hm - mu
    var = jnp.mean(xc * xc, axis=1, keepdims=True)
    y_ref[...] = (_sigmoid(o_ref[...]) * (xc * lax.rsqrt(var + LN_EPS) * g_ref[...])).astype(y_ref.dtype)


def _mlstm_out(h_pair, proj, col0, norm_g, tm):
    T = proj.shape[0]
    D_B = h_pair[0].shape[-1]
    dh = D_B // B_HEADS
    cb0 = col0 // dh
    npt, h_specs = _pair_specs(h_pair, lambda r, h=0: ((2, tm, dh), (0, r, h)), tm)
    return pl.pallas_call(
        functools.partial(_mlstm_out_body, npt=npt),
        grid=(T // tm, B_HEADS),
        in_specs=h_specs + [pl.BlockSpec((tm, dh), lambda i, h: (i, cb0 + h)),
                            pl.BlockSpec((1, dh), lambda i, h: (0, h))],
        out_specs=pl.BlockSpec((tm, dh), lambda i, h: (i, h)),
        out_shape=jax.ShapeDtypeStruct((T, D_B), BF16),
        compiler_params=_cparams(2),
        name="mlstm_out",
    )(*h_pair, proj, norm_g.reshape(1, D_B))


def _split3(x):
    hi = x.astype(BF16)
    r1 = x - hi.astype(F32)
    mid = r1.astype(BF16)
    return hi, mid, (r1 - mid.astype(F32)).astype(BF16)


def _gla_body(q_ref, k_ref, v_ref, lr_ref, gw_ref, gb_ref, s0_ref, o_ref, so_ref, ST_s,
              *, Tc, nc, q_scale, dk, dv):
    d = pl.program_id(0)
    c = pl.program_id(2)
    fwd = d == 0
    H = C_HEADS

    @pl.when(c == 0)
    def _():
        for hd in range(H):
            ST_s[hd] = s0_ref[hd].T

    lr_hi, lr_mid, _ = _split3(lr_ref[...])
    gw_hi, gw_mid, _ = _split3(gw_ref[...])
    z = (jnp.dot(lr_hi, gw_hi, preferred_element_type=F32)
         + jnp.dot(lr_hi, gw_mid, preferred_element_type=F32)
         + jnp.dot(lr_mid, gw_hi, preferred_element_type=F32)) + gb_ref[...]
    loga = _log_sigmoid(z) * (1.0 / GLA_TAU)
    tri = _tri_mask(Tc, d)
    trib = tri.astype(F32).astype(BF16)
    b = sum(jnp.dot(trib, piece, preferred_element_type=F32) for piece in _split3(loga))
    qs = (q_ref[...] * q_scale * jnp.exp(b)).astype(BF16)
    kf = k_ref[...]
    ke = (kf * jnp.exp(-b)).astype(BF16)
    bL = jnp.where(fwd, b[Tc - 1:Tc, :], b[0:1, :])
    kd = (kf * jnp.exp(bL - b)).astype(BF16)
    ebL = jnp.exp(bL)
    for hd in range(H):
        ks = slice(hd * dk, (hd + 1) * dk)
        vs = slice(hd * dv, (hd + 1) * dv)
        vb = v_ref[:, vs].astype(BF16)
        att = jnp.where(tri, lax.dot_general(qs[:, ks], ke[:, ks], NT_DIMS, preferred_element_type=F32), 0.0)
        ST = ST_s[hd]
        inter = lax.dot_general(qs[:, ks], ST.astype(BF16), NT_DIMS, preferred_element_type=F32)
        o_ref[:, vs] = inter + jnp.dot(att.astype(BF16), vb, preferred_element_type=F32)
        ST_s[hd] = ST * ebL[:, ks] + lax.dot_general(vb, kd[:, ks], TN_DIMS, preferred_element_type=F32)

    @pl.when(c == nc - 1)
    def _():
        for hd in range(H):
            so_ref[hd] = ST_s[hd].T


def _gla(proj, lr, gw_pad, gate_b, row0, nseq, L, s0):
    H = C_HEADS
    dk, dv = s0.shape[-2:]
    DK, DV = H * dk, H * dv
    Tc = min(SCAN_CHUNK, L)
    nc = L // Tc
    rb0 = row0 // Tc

    def rb(d, s, c):
        return s * nc + jnp.where(d == 0, c, nc - 1 - c)

    st_map = lambda d, s, c: (s, d, 0, 0, 0)
    in_specs = [pl.BlockSpec((Tc, DK), lambda d, s, c: (rb0 + rb(d, s, c), 0)),
                pl.BlockSpec((Tc, DK), lambda d, s, c: (rb0 + rb(d, s, c), 1)),
                pl.BlockSpec((Tc, DV), lambda d, s, c: (rb0 + rb(d, s, c), (2 * DK) // DV)),
                pl.BlockSpec((Tc, LANE), lambda d, s, c: (rb0 + rb(d, s, c), 0)),
                pl.BlockSpec((None, LANE, DK), lambda d, s, c: (d, 0, 0)),
                pl.BlockSpec((None, 1, DK), lambda d, s, c: (d, 0, 0)),
                pl.BlockSpec((None, None, H, dk, dv), st_map)]
    args = [proj, proj, proj, lr, gw_pad, gate_b.reshape(2, 1, DK), s0]
    return pl.pallas_call(
        functools.partial(_gla_body, Tc=Tc, nc=nc, q_scale=dk ** -0.5, dk=dk, dv=dv),
        grid=(2, nseq, nc),
        in_specs=in_specs,
        out_specs=[pl.BlockSpec((None, Tc, DV), lambda d, s, c: (d, rb(d, s, c), 0)),
                   pl.BlockSpec((None, None, H, dk, dv), st_map)],
        out_shape=[jax.ShapeDtypeStruct((2, nseq * L, DV), F32),
                   jax.ShapeDtypeStruct((nseq, 2, H, dk, dv), F32)],
        scratch_shapes=[pltpu.VMEM((H, dv, dk), F32)],
        compiler_params=_cparams(3),
        name="gla",
    )(*args)


def _gla_out_body(op_ref, os_ref, g_ref, ng_ref, y_ref, *, npt):
    o = _pick(op_ref, os_ref, npt)
    oo = o[0] + o[1]
    ms = jnp.mean(oo * oo, axis=1, keepdims=True)
    y_ref[...] = (oo * lax.rsqrt(ms + LN_EPS) * ng_ref[...] * _silu(g_ref[...])).astype(y_ref.dtype)


def _gla_out(o_pair, proj, col0, norm_g, tm):
    T = proj.shape[0]
    DV = o_pair[0].shape[-1]
    dv = DV // C_HEADS
    cb0 = col0 // dv
    npt, o_specs = _pair_specs(o_pair, lambda r, h=0: ((2, tm, dv), (0, r, h)), tm)
    return pl.pallas_call(
        functools.partial(_gla_out_body, npt=npt),
        grid=(T // tm, C_HEADS),
        in_specs=o_specs + [pl.BlockSpec((tm, dv), lambda i, h: (i, cb0 + h)),
                            pl.BlockSpec((1, dv), lambda i, h: (0, h))],
        out_specs=pl.BlockSpec((tm, dv), lambda i, h: (i, h)),
        out_shape=jax.ShapeDtypeStruct((T, DV), BF16),
        compiler_params=_cparams(2),
        name="gla_out",
    )(*o_pair, proj, norm_g.reshape(1, DV))


def _layer_norm(z, g, b):
    mu = jnp.mean(z, axis=1, keepdims=True)
    zc = z - mu
    var = jnp.mean(zc * zc, axis=1, keepdims=True)
    return zc * lax.rsqrt(var + LN_EPS) * g + b


def _outproj_body(*refs, npts, g_row, alpha):
    n = len(npts)
    n_y = sum(1 if p is None else 2 for p in npts)
    y_refs = list(refs[:n_y])
    ws = refs[n_y:n_y + n]
    x_ref, m_ref, lg_ref, lb_ref, o_ref = refs[n_y + n:]
    acc = None
    for npt, w in zip(npts, ws):
        y = y_refs.pop(0)[...] if npt is None else _pick(y_refs.pop(0), y_refs.pop(0), npt)
        part = jnp.dot(y, w[...], preferred_element_type=F32)
        acc = part if acc is None else acc + part
    z = alpha * x_ref[...] + m_ref[g_row:g_row + 1, :] * acc
    o_ref[...] = _layer_norm(z, lg_ref[...], lb_ref[...])


def _outproj_ln(ys, ws, x, modt, ln_g, ln_b, *, g_row, alpha, tm):
    T, D = x.shape
    mod4, mod_map = modt[0], modt[1](tm)
    y_specs, y_args, npts = [], [], []
    for y in ys:
        if isinstance(y, tuple):
            K = y[0].shape[1]
            npt, specs = _pair_specs(y, lambda r, K=K: ((tm, K), (r, 0)), tm)
            y_specs += specs
            y_args += list(y)
            npts.append(npt)
        else:
            y_specs.append(pl.BlockSpec((tm, y.shape[1]), lambda i: (i, 0)))
            y_args.append(y)
            npts.append(None)
    in_specs = (y_specs
                + [pl.BlockSpec(w.shape, lambda i: (0, 0)) for w in ws]
                + [pl.BlockSpec((tm, D), lambda i: (i, 0)),
                   pl.BlockSpec((None, None, 6, D), lambda i: mod_map(i)),
                   pl.BlockSpec((1, D), lambda i: (0, 0)),
                   pl.BlockSpec((1, D), lambda i: (0, 0))])
    return pl.pallas_call(
        functools.partial(_outproj_body, npts=tuple(npts), g_row=g_row, alpha=alpha),
        grid=(T // tm,),
        in_specs=in_specs,
        out_specs=pl.BlockSpec((tm, D), lambda i: (i, 0)),
        out_shape=jax.ShapeDtypeStruct((T, D), F32),
        compiler_params=_cparams(1),
        name="outproj_ln",
    )(*y_args, *ws, x, mod4, ln_g.reshape(1, D), ln_b.reshape(1, D))


def _router_body(x_ref, m_ref, w_ref, b_ref, h_ref, ids_ref, gts_ref, *, sh_row, sc_row):
    h = x_ref[...] * (1.0 + m_ref[sc_row:sc_row + 1, :]) + m_ref[sh_row:sh_row + 1, :]
    _to_tok_blocks(h_ref, h)
    logits = lax.dot_general(w_ref[...], h, NT_DIMS, precision=HI, preferred_element_type=F32) + b_ref[...]
    E, tm = logits.shape
    row = lax.broadcasted_iota(jnp.int32, (E, tm), 0)
    mx = jnp.max(logits, axis=0, keepdims=True)
    ex = jnp.exp(logits - mx)
    p = ex / jnp.sum(ex, axis=0, keepdims=True)
    grp = row // (E // N_GROUPS)
    best = None
    for g in range(N_GROUPS):
        pg = jnp.where(grp == g, p, -1.0)
        v1 = jnp.max(pg, axis=0, keepdims=True)
        i1 = jnp.min(jnp.where(pg == v1, row, E), axis=0, keepdims=True)
        pg2 = jnp.where(row == i1, -1.0, pg)
        v2 = jnp.max(pg2, axis=0, keepdims=True)
        i2 = jnp.min(jnp.where(pg2 == v2, row, E), axis=0, keepdims=True)
        score = v1 + v2
        if best is None:
            best = (score, v1, i1, v2, i2)
        else:
            take = score > best[0]
            best = tuple(jnp.where(take, n, o) for n, o in zip((score, v1, i1, v2, i2), best))
    _, v1, i1, v2, i2 = best
    tot = v1 + v2
    slot = lax.broadcasted_iota(jnp.int32, (SUBLANE, tm), 0)
    ids_ref[...] = jnp.where(slot == 0, i1, jnp.where(slot == 1, i2, 0))
    gts_ref[...] = jnp.where(slot == 0, v1 / tot, jnp.where(slot == 1, v2 / tot, 0.0))


def _router(x, modt, router_w, router_b, *, sh_row, sc_row, tm):
    T, D = x.shape
    E = router_w.shape[1]
    mod4, mod_map = modt[0], modt[1](tm)
    return pl.pallas_call(
        functools.partial(_router_body, sh_row=sh_row, sc_row=sc_row),
        grid=(T // tm,),
        in_specs=[pl.BlockSpec((tm, D), lambda i: (i, 0)),
                  pl.BlockSpec((None, None, 6, D), lambda i: mod_map(i)),
                  pl.BlockSpec((E, D), lambda i: (0, 0)),
                  pl.BlockSpec((E, 1), lambda i: (0, 0))],
        out_specs=[pl.BlockSpec((tm * TOK_ROWS, LANE), lambda i: (i, 0)),
                   pl.BlockSpec((SUBLANE, tm), lambda i: (0, i)),
                   pl.BlockSpec((SUBLANE, tm), lambda i: (0, i))],
        out_shape=[jax.ShapeDtypeStruct((T * TOK_ROWS, LANE), F32),
                   jax.ShapeDtypeStruct((SUBLANE, T), jnp.int32),
                   jax.ShapeDtypeStruct((SUBLANE, T), F32)],
        compiler_params=_cparams(1),
        name="router",
    )(x, mod4, router_w.T, router_b.reshape(E, 1))


def _to_tok_blocks(ref, x):
    tm = x.shape[0]
    for s in range(TOK_ROWS):
        ref[pl.ds(s, tm, stride=TOK_ROWS), :] = x[:, s * LANE:(s + 1) * LANE]


def _tok_chunk(ref, s, tm):
    return ref[pl.ds(s, tm, stride=TOK_ROWS), :]


class _RowGather:
    def __init__(self, src_hbm, idx_hbm, idx_s, bufs, isem, gsem, tm):
        self.src, self.idx_hbm, self.idx_s, self.bufs = src_hbm, idx_hbm, idx_s, bufs
        self.isem, self.gsem, self.tm = isem, gsem, tm

    def _idx_copy(self, tile, slot):
        return pltpu.make_async_copy(self.idx_hbm.at[tile], self.idx_s.at[slot], self.isem.at[slot])

    def _row_copy(self, slot, a, r):
        row = pl.multiple_of(self.idx_s[slot, a, r], TOK_ROWS)
        dst = pl.multiple_of(r * TOK_ROWS, TOK_ROWS)
        return pltpu.make_async_copy(self.src.at[pl.ds(row, TOK_ROWS), :],
                                     self.bufs[a].at[slot, pl.ds(dst, TOK_ROWS), :], self.gsem.at[slot, a])

    def _rows(self, slot, start):
        def body(r, carry):
            for a in range(len(self.bufs)):
                cp = self._row_copy(slot, a, r)
                cp.start() if start else cp.wait()
            return carry
        lax.fori_loop(0, self.tm, body, 0, unroll=8)

    def step(self, i, n_tiles, cur_valid, next_valid):
        slot = i % 2

        @pl.when(i == 0)
        def _():
            self._idx_copy(0, 0).start()
            self._idx_copy(0, 0).wait()
            self._rows(0, True)
            if n_tiles > 1:
                self._idx_copy(1, 1).start()

        @pl.when(i + 1 < n_tiles)
        def _():
            self._idx_copy(i + 1, 1 - slot).wait()

            @pl.when(next_valid)
            def _():
                self._rows(1 - slot, True)

        @pl.when(cur_valid)
        def _():
            self._rows(slot, False)

        @pl.when(i + 2 < n_tiles)
        def _():
            self._idx_copy(i + 2, slot).start()


def _ffn_body(te_ref, va_ref, idx_hbm, h_hbm, wi_ref, wo_ref, o_ref, xbuf, idx_s, isem, gsem, *, F, tm, n_tiles):
    i = pl.program_id(0)
    valid = va_ref[i] != 0
    nxt = va_ref[jnp.minimum(i + 1, n_tiles - 1)] != 0
    _RowGather(h_hbm, idx_hbm, idx_s, [xbuf], isem, gsem, tm).step(i, n_tiles, valid, nxt)

    @pl.when(valid)
    def _():
        xb = xbuf.at[i % 2]
        uw = None
        for p in range(TOK_ROWS // 2):
            xs = jnp.concatenate([_tok_chunk(xb, 2 * p, tm), _tok_chunk(xb, 2 * p + 1, tm)], axis=1).astype(BF16)
            part = jnp.dot(xs, wi_ref[2 * p * LANE:(2 * p + 2) * LANE, :], preferred_element_type=F32)
            uw = part if uw is None else uw + part
        act = (_silu(uw[:, :F]) * uw[:, F:]).astype(BF16)
        _to_tok_blocks(o_ref, jnp.dot(act, wo_ref[...], preferred_element_type=F32))

    @pl.when(jnp.logical_not(valid))
    def _():
        o_ref[...] = jnp.zeros_like(o_ref)


def _expert_ffn(h, src_tok, w_in, w_out, layer, tile_e, tile_valid):
    n_tiles, _, tm = src_tok.shape
    F, D = w_out.shape[2:]
    return pl.pallas_call(
        functools.partial(_ffn_body, F=F, tm=tm, n_tiles=n_tiles),
        grid_spec=pltpu.PrefetchScalarGridSpec(
            num_scalar_prefetch=2,
            grid=(n_tiles,),
            in_specs=[pl.BlockSpec(memory_space=pl.ANY),
                      pl.BlockSpec(memory_space=pl.ANY),
                      pl.BlockSpec((None, None, D, 2 * F), lambda i, te, va: (layer, te[i], 0, 0)),
                      pl.BlockSpec((None, None, F, D), lambda i, te, va: (layer, te[i], 0, 0))],
            out_specs=pl.BlockSpec((tm * TOK_ROWS, LANE), lambda i, te, va: (i, 0)),
            scratch_shapes=[pltpu.VMEM((2, tm * TOK_ROWS, LANE), F32), pltpu.SMEM((2, 1, tm), jnp.int32),
                            pltpu.SemaphoreType.DMA((2,)), pltpu.SemaphoreType.DMA((2, 1))]),
        out_shape=jax.ShapeDtypeStruct((n_tiles * tm * TOK_ROWS, LANE), F32),
        compiler_params=_cparams(1),
        name="expert_ffn",
    )(tile_e, tile_valid, src_tok, h, w_in, w_out)


def _combine_body(x_ref, idx_hbm, y_hbm, gt_ref, m_ref, lg_ref, lb_ref, o_ref, y0buf, y1buf, idx_s, isem, gsem,
                  *, g_row, alpha, tm, n_tiles):
    i = pl.program_id(0)
    always = i >= 0
    _RowGather(y_hbm, idx_hbm, idx_s, [y0buf, y1buf], isem, gsem, tm).step(i, n_tiles, always, always)
    gt = gt_ref[...]
    g0, g1 = gt[:, 0:1], gt[:, 1:2]
    y0b, y1b = y0buf.at[i % 2], y1buf.at[i % 2]
    moe = jnp.concatenate([g0 * _tok_chunk(y0b, s, tm) + g1 * _tok_chunk(y1b, s, tm) for s in range(TOK_ROWS)],
                          axis=1)
    z = alpha * x_ref[...] + m_ref[g_row:g_row + 1, :] * moe
    o_ref[...] = _layer_norm(z, lg_ref[...], lb_ref[...])


def _combine_ln(x, ys, dest, gts, modt, ln_g, ln_b, *, g_row, alpha):
    T, D = x.shape
    n_tiles, _, tm = dest.shape
    mod4, mod_map = modt[0], modt[1](tm)
    row = pl.BlockSpec((tm, D), lambda i: (i, 0))
    vec = pl.BlockSpec((1, D), lambda i: (0, 0))
    hbm = pl.BlockSpec(memory_space=pl.ANY)
    return pl.pallas_call(
        functools.partial(_combine_body, g_row=g_row, alpha=alpha, tm=tm, n_tiles=n_tiles),
        grid=(n_tiles,),
        in_specs=[row, hbm, hbm, pl.BlockSpec((tm, LANE), lambda i: (i, 0)),
                  pl.BlockSpec((None, None, 6, D), lambda i: mod_map(i)), vec, vec],
        out_specs=row,
        out_shape=jax.ShapeDtypeStruct((T, D), F32),
        scratch_shapes=[pltpu.VMEM((2, tm * TOK_ROWS, LANE), F32), pltpu.VMEM((2, tm * TOK_ROWS, LANE), F32),
                        pltpu.SMEM((2, 2, tm), jnp.int32),
                        pltpu.SemaphoreType.DMA((2,)), pltpu.SemaphoreType.DMA((2, 2))],
        compiler_params=_cparams(1),
        name="combine_ln",
    )(x, dest, ys, gts, mod4, ln_g.reshape(1, D), ln_b.reshape(1, D))


def _route_meta(ids, tm, tm_tok):
    T = ids.shape[0]
    E = N_EXPERTS
    e_flat = ids.reshape(-1)
    onehot = (e_flat[:, None] == jnp.arange(E, dtype=jnp.int32)[None, :]).astype(jnp.int32)
    csum = jnp.cumsum(onehot, axis=0)
    rank = jnp.sum(csum * onehot, axis=1) - 1
    counts = csum[-1]
    padded = ((counts + tm - 1) // tm) * tm
    ends = jnp.cumsum(padded)
    starts = ends - padded
    dest = jnp.sum(starts[None, :] * onehot, axis=1) + rank
    n_rows = 2 * T + E * tm
    src_tok = jnp.zeros((n_rows,), jnp.int32).at[dest].set(jnp.arange(2 * T, dtype=jnp.int32) // 2)
    tile_start = jnp.arange(n_rows // tm, dtype=jnp.int32) * tm
    valid = (tile_start < ends[-1]).astype(jnp.int32)
    probe = jnp.minimum(tile_start, ends[-1] - 1)
    tile_e = jnp.sum((ends[None, :] <= probe[:, None]).astype(jnp.int32), axis=1)
    dest_t = dest.reshape(T // tm_tok, tm_tok, 2).transpose(0, 2, 1)
    return (dest_t * TOK_ROWS, src_tok.reshape(n_rows // tm, 1, tm) * TOK_ROWS, jnp.minimum(tile_e, E - 1), valid)


def _pos_embed_2d(n_tokens, dim):
    rows = n_tokens // GRID_W
    quarter = dim // 4
    freqs = jnp.exp(-math.log(10000.0) * jnp.arange(quarter, dtype=F32) / quarter)
    r = jnp.broadcast_to(jnp.arange(rows, dtype=F32)[:, None], (rows, GRID_W)).reshape(-1)
    col = jnp.broadcast_to(jnp.arange(GRID_W, dtype=F32)[None, :], (rows, GRID_W)).reshape(-1)
    ar = r[:, None] * freqs
    ac = col[:, None] * freqs
    return jnp.concatenate([jnp.sin(ar), jnp.cos(ar), jnp.sin(ac), jnp.cos(ac)], -1)


def _pad_cols(w, n):
    return jnp.pad(w, ((0, 0),) * (w.ndim - 1) + ((0, n - w.shape[-1]),))


def kernel(x_prompt, x_sample, state_lru, state_mlstm_c, state_mlstm_n, state_mlstm_m, state_gla, c, c_ctx, mod_w, mod_b, ln_g, ln_b, even_w_in, even_w_out, lru_conv_w, lru_conv_b, lru_gate_w, lru_gate_b, lru_lambda, mlstm_gate_b, mlstm_norm_g, odd_w_in, odd_w_out, gla_gate_w, gla_gate_b, gla_norm_g, router_w, router_b, moe_w_in, moe_w_out):
    Bp, Lp, D = x_prompt.shape
    Bs, Ls, _ = x_sample.shape
    depth = mod_w.shape[0]
    Tp, Ts = Bp * Lp, Bs * Ls
    T = Tp + Ts
    alpha = (2 * depth) ** 0.25
    D_A = lru_conv_w.shape[-1]
    D_B = mlstm_norm_g.shape[-1]
    DK = gla_gate_w.shape[-1]
    DV = gla_norm_g.shape[-1]
    tm = next(t for t in (512, 256, 128) if Tp % t == 0 and Ls % t == 0)
    tn = 1024

    n_cond = 1 + Bs
    R = -(-n_cond // SUBLANE) * SUBLANE
    cond = jnp.zeros((R, D), F32).at[0].set(c_ctx).at[1:n_cond].set(c)
    mod_all = _modulation(cond, mod_w, mod_b)
    mod4 = mod_all.reshape(depth, R, 6, D)
    tm_proj = next(t for t in (1024, 512, 256, 128) if Tp % t == 0 and Ls % t == 0)
    assert D == TOK_ROWS * LANE

    x = _embed(x_prompt.reshape(Tp, D), x_sample.reshape(Ts, D), _pos_embed_2d(Ls, D), tm)

    moe_w_in_b = moe_w_in.astype(BF16)
    moe_w_out_b = moe_w_out.astype(BF16)
    groups = ((0, Bp, Lp), (Tp, Bs, Ls))

    s_lru, s_c, s_n, s_m, s_gla = [], [], [], [], []
    for l in range(depth):
        j = l // 2
        mt = (mod4, lambda t, l=l: (lambda i: (l, jnp.where(i < Tp // t, 0, 1 + (i - Tp // t) // (Ls // t)), 0, 0)))
        if l % 2 == 0:
            w_in = even_w_in[j]
            n_main = 2 * D_A + 4 * D_B
            proj = _proj(x, mt, w_in[:, :n_main].astype(BF16), sh_row=0, sc_row=1, tm=tm_proj, tn=tn)
            gates = _proj(x, mt, _pad_cols(w_in[:, n_main:], LANE), sh_row=0, sc_row=1, tm=tm, tn=LANE, hp=True)
            gate_bias = _pad_cols(mlstm_gate_b[j].reshape(1, -1), LANE)
            ya, hs = [], []
            for gi, (row0, nseq, L) in enumerate(groups):
                if gi == 0:
                    lru0 = jnp.zeros((nseq, 2, D_A), F32)
                    c0 = jnp.zeros((nseq, 2, B_HEADS, D_B // B_HEADS, D_B // B_HEADS), F32)
                    n0 = jnp.zeros((nseq, 2, B_HEADS, D_B // B_HEADS), F32)
                    m0 = jnp.zeros((nseq, 2, B_HEADS), F32)
                else:
                    lru0, c0, n0, m0 = state_lru[:, j], state_mlstm_c[:, j], state_mlstm_n[:, j], state_mlstm_m[:, j]
                y_a, st = _lru(proj, row0, nseq, L, lru_conv_w[j], lru_conv_b[j], lru_gate_w[j], lru_gate_b[j],
                               lru_lambda[j], lru0)
                h, C, n, m = _mlstm(proj, gates, gate_bias, row0, nseq, L, 2 * D_A, c0, n0, m0)
                ya.append(y_a)
                hs.append(h)
                if gi == 0:
                    s_lru.append(st)
                    s_c.append(C)
                    s_n.append(n.reshape(nseq, 2, B_HEADS, -1))
                    s_m.append(m.reshape(nseq, 2, B_HEADS))
            y_b = _mlstm_out(tuple(hs), proj, 2 * D_A + 3 * D_B, mlstm_norm_g[j], tm)
            w_out = even_w_out[j].astype(BF16)
            x = _outproj_ln([tuple(ya), y_b], [w_out[:D_A], w_out[D_A:]], x, mt, ln_g[l, 0], ln_b[l, 0],
                            g_row=2, alpha=alpha, tm=tm)
        else:
            w_in = odd_w_in[j]
            n_main = 2 * DK + 2 * DV
            proj = _proj(x, mt, w_in[:, :n_main].astype(BF16), sh_row=0, sc_row=1, tm=tm_proj, tn=tn)
            lr = _proj(x, mt, _pad_cols(w_in[:, n_main:], LANE), sh_row=0, sc_row=1, tm=tm, tn=LANE, hp=True)
            gw_pad = jnp.zeros((2, LANE, DK), F32)
            for dd in range(2):
                gw_pad = gw_pad.at[dd, dd * GLA_RANK:(dd + 1) * GLA_RANK].set(gla_gate_w[j, dd])
            os_ = []
            for gi, (row0, nseq, L) in enumerate(groups):
                s0 = (jnp.zeros((nseq, 2, C_HEADS, DK // C_HEADS, DV // C_HEADS), F32) if gi == 0
                      else state_gla[:, j])
                o, S = _gla(proj, lr, gw_pad, gla_gate_b[j], row0, nseq, L, s0)
                os_.append(o)
                if gi == 0:
                    s_gla.append(S)
            y = _gla_out(tuple(os_), proj, 2 * DK + DV, gla_norm_g[j], tm)
            x = _outproj_ln([y], [odd_w_out[j].astype(BF16)], x, mt, ln_g[l, 0], ln_b[l, 0],
                            g_row=2, alpha=alpha, tm=tm)

        hf, ids, gts = _router(x, mt, router_w, router_b, sh_row=3, sc_row=4, tm=tm)
        gts = _pad_cols(gts[:2].T, LANE)
        dest, src_tok, tile_e, tile_valid = _route_meta(ids[:2].T, MOE_TM, tm)
        ys = _expert_ffn(hf, src_tok, moe_w_in_b, moe_w_out_b, l, tile_e, tile_valid)
        x = _combine_ln(x, ys, dest, gts, mt, ln_g[l, 1], ln_b[l, 1], g_row=5, alpha=alpha)

    y_prompt = x[:Tp].reshape(Bp, Lp, D)
    y_sample = x[Tp:].reshape(Bs, Ls, D)
    return (y_prompt, y_sample, jnp.stack(s_lru, 1), jnp.stack(s_c, 1), jnp.stack(s_n, 1),
            jnp.stack(s_m, 1), jnp.stack(s_gla, 1))
```

```python
import functools
import math

import jax
import jax.numpy as jnp
from jax import lax
from jax.experimental import pallas as pl
from jax.experimental.pallas import tpu as pltpu

F32 = jnp.float32
BF16 = jnp.bfloat16
HI = lax.Precision.HIGHEST

LN_EPS = 1e-5
LRU_C = 8.0
A_BLOCKS = 8
B_HEADS = 4
C_HEADS = 4
GLA_RANK = 16
GLA_TAU = 16.0
N_EXPERTS = 16
N_GROUPS = 4
GRID_W = 64
LANE = 128
SUBLANE = 8
SCAN_CHUNK = 128
MOE_TM = 512
TOK_ROWS = 16
VMEM_LIMIT = 56 * 1024 * 1024

NT_DIMS = (((1,), (1,)), ((), ()))
TN_DIMS = (((0,), (0,)), ((), ()))


def _cparams(n_axes):
    return pltpu.CompilerParams(dimension_semantics=("arbitrary",) * n_axes,
                                vmem_limit_bytes=VMEM_LIMIT)


def _sigmoid(x):
    return 1.0 / (1.0 + jnp.exp(-x))


def _silu(x):
    return x * _sigmoid(x)


def _log_sigmoid(x):
    return jnp.minimum(x, 0.0) - jnp.log1p(jnp.exp(-jnp.abs(x)))


def _gelu_tanh(x):
    return 0.5 * x * (1.0 + jnp.tanh(math.sqrt(2.0 / math.pi) * (x + 0.044715 * (x * x * x))))


def _mod_body(c_ref, w_ref, b_ref, o_ref):
    c = c_ref[...]
    o_ref[...] = jnp.dot(_silu(c), w_ref[...], precision=HI, preferred_element_type=F32) + b_ref[...]


def _modulation(cond, mod_w, mod_b):
    R, D = cond.shape
    nl, _, N = mod_w.shape
    tn = 1024
    return pl.pallas_call(
        _mod_body,
        grid=(nl, N // tn),
        in_specs=[pl.BlockSpec((R, D), lambda l, j: (0, 0)),
                  pl.BlockSpec((None, D, tn), lambda l, j: (l, 0, j)),
                  pl.BlockSpec((None, 1, tn), lambda l, j: (l, 0, j))],
        out_specs=pl.BlockSpec((None, R, tn), lambda l, j: (l, 0, j)),
        out_shape=jax.ShapeDtypeStruct((nl, R, N), F32),
        compiler_params=_cparams(2),
        name="modulation",
    )(cond, mod_w, mod_b.reshape(nl, 1, N))


def _embed_body(xp_ref, xs_ref, pos_ref, o_ref, *, n_prompt_tiles):
    i = pl.program_id(0)

    @pl.when(i < n_prompt_tiles)
    def _():
        o_ref[...] = xp_ref[...]

    @pl.when(i >= n_prompt_tiles)
    def _():
        o_ref[...] = xs_ref[...] + pos_ref[...]


def _embed(xp, xs, pos, tm):
    Tp, D = xp.shape
    Ts = xs.shape[0]
    Ls = pos.shape[0]
    npt, nst, npos = Tp // tm, Ts // tm, Ls // tm
    return pl.pallas_call(
        functools.partial(_embed_body, n_prompt_tiles=npt),
        grid=(npt + nst,),
        in_specs=[pl.BlockSpec((tm, D), lambda i: (jnp.minimum(i, npt - 1), 0)),
                  pl.BlockSpec((tm, D), lambda i: (jnp.maximum(i - npt, 0), 0)),
                  pl.BlockSpec((tm, D), lambda i: (jnp.maximum(i - npt, 0) % npos, 0))],
        out_specs=pl.BlockSpec((tm, D), lambda i: (i, 0)),
        out_shape=jax.ShapeDtypeStruct((Tp + Ts, D), F32),
        compiler_params=_cparams(1),
        name="embed",
    )(xp, xs, pos)


def _proj_body(x_ref, m_ref, w_ref, o_ref, h_ref, *, sh_row, sc_row, hp):
    @pl.when(pl.program_id(1) == 0)
    def _():
        h = x_ref[...] * (1.0 + m_ref[sc_row:sc_row + 1, :]) + m_ref[sh_row:sh_row + 1, :]
        h_ref[...] = h.astype(h_ref.dtype)

    if hp:
        o = jnp.dot(h_ref[...], w_ref[...], precision=HI, preferred_element_type=F32)
    else:
        o = jnp.dot(h_ref[...], w_ref[...], preferred_element_type=F32)
    o_ref[...] = o.astype(o_ref.dtype)


def _proj(x, modt, w, *, sh_row, sc_row, tm, tn, hp=False):
    T, D = x.shape
    N = w.shape[1]
    mod4, mod_map = modt[0], modt[1](tm)
    return pl.pallas_call(
        functools.partial(_proj_body, sh_row=sh_row, sc_row=sc_row, hp=hp),
        grid=(T // tm, N // tn),
        in_specs=[pl.BlockSpec((tm, D), lambda i, j: (i, 0)),
                  pl.BlockSpec((None, None, 6, D), lambda i, j: mod_map(i)),
                  pl.BlockSpec((D, tn), lambda i, j: (0, j))],
        out_specs=pl.BlockSpec((tm, tn), lambda i, j: (i, j)),
        out_shape=jax.ShapeDtypeStruct((T, N), F32),
        scratch_shapes=[pltpu.VMEM((tm, D), F32 if hp else BF16)],
        compiler_params=_cparams(2),
        name="proj_hp" if hp else "proj",
    )(x, mod4, w)


def _lru_body(ag_ref, ax_ref, cw_ref, cb_ref, gw_ref, gb_ref, lam_ref, h0_ref, y_ref, st_ref,
              af_s, uf_s, ab_s, ub_s, hf_s, hb_s, *, L):
    x = ax_ref[...]
    row = lax.broadcasted_iota(jnp.int32, (L, LANE), 0)
    xm1 = jnp.where(row >= 1, pltpu.roll(x, 1, 0), 0.0)
    xp1 = jnp.where(row < L - 1, pltpu.roll(x, L - 1, 0), 0.0)
    xp2 = jnp.where(row < L - 2, pltpu.roll(x, L - 2, 0), 0.0)
    xc = cb_ref[...] + xm1 * cw_ref[0:1, :] + x * cw_ref[1:2, :] + xp1 * cw_ref[2:3, :] + xp2 * cw_ref[3:4, :]
    xcb = xc.astype(BF16)
    nj = L // SUBLANE
    sub = lax.broadcasted_iota(jnp.int32, (nj, SUBLANE, LANE), 1)

    for d, (a_s, u_s) in enumerate(((af_s, uf_s), (ab_s, ub_s))):
        g = jnp.dot(xcb, gw_ref[d], preferred_element_type=F32) + gb_ref[d]
        r = _sigmoid(g[:, :LANE])
        ig = _sigmoid(g[:, LANE:])
        nlam = -lam_ref[d]
        softplus = jnp.maximum(nlam, 0.0) + jnp.log1p(jnp.exp(-jnp.abs(nlam)))
        log_a = (-LRU_C * softplus) * r
        a = jnp.exp(log_a)
        u = jnp.sqrt(jnp.tanh(-log_a) * (1.0 + a * a)) * ig * xc
        a3 = a.reshape(nj, SUBLANE, LANE)
        u3 = u.reshape(nj, SUBLANE, LANE)
        for k in (1, 2, 4):
            if d == 0:
                sh, keep = k, sub >= k
            else:
                sh, keep = SUBLANE - k, sub < SUBLANE - k
            a_sh = pltpu.roll(a3, sh, 1)
            u_sh = pltpu.roll(u3, sh, 1)
            u3 = jnp.where(keep, a3 * u_sh + u3, u3)
            a3 = jnp.where(keep, a3 * a_sh, a3)
        a_s[...] = a3.reshape(L, LANE)
        u_s[...] = u3.reshape(L, LANE)

    def carry(j, hs):
        hf, hb = hs
        rf = pl.multiple_of(j * SUBLANE, SUBLANE)
        rb = pl.multiple_of((nj - 1 - j) * SUBLANE, SUBLANE)
        of = af_s[pl.ds(rf, SUBLANE), :] * hf + uf_s[pl.ds(rf, SUBLANE), :]
        ob = ab_s[pl.ds(rb, SUBLANE), :] * hb + ub_s[pl.ds(rb, SUBLANE), :]
        hf_s[pl.ds(rf, SUBLANE), :] = of
        hb_s[pl.ds(rb, SUBLANE), :] = ob
        return (jnp.broadcast_to(of[SUBLANE - 1:SUBLANE, :], (SUBLANE, LANE)),
                jnp.broadcast_to(ob[0:1, :], (SUBLANE, LANE)))

    h0f = jnp.broadcast_to(h0_ref[0:1, :], (SUBLANE, LANE))
    h0b = jnp.broadcast_to(h0_ref[1:2, :], (SUBLANE, LANE))
    lax.fori_loop(0, nj, carry, (h0f, h0b))
    y_ref[...] = (_gelu_tanh(ag_ref[...]) * (hf_s[...] + hb_s[...])).astype(y_ref.dtype)
    st_ref[0:1, :] = hf_s[L - 1:L, :]
    st_ref[1:2, :] = hb_s[0:1, :]


def _lru(proj, row0, nseq, L, conv_w, conv_b, gate_w, gate_b, lam, h0):
    D_A = conv_w.shape[1]
    nb = D_A // LANE
    rb0 = row0 // L
    in_specs = [pl.BlockSpec((L, LANE), lambda s, h: (rb0 + s, h)),
                pl.BlockSpec((L, LANE), lambda s, h: (rb0 + s, nb + h)),
                pl.BlockSpec((4, LANE), lambda s, h: (0, h)),
                pl.BlockSpec((1, LANE), lambda s, h: (0, h)),
                pl.BlockSpec((2, None, LANE, 2 * LANE), lambda s, h: (0, h, 0, 0)),
                pl.BlockSpec((2, None, 1, 2 * LANE), lambda s, h: (0, h, 0, 0)),
                pl.BlockSpec((2, None, 1, LANE), lambda s, h: (0, h, 0, 0)),
                pl.BlockSpec((None, 2, LANE), lambda s, h: (s, 0, h))]
    args = [proj, proj, conv_w, conv_b.reshape(1, D_A), gate_w.astype(BF16),
            gate_b.reshape(2, nb, 1, 2 * LANE), lam.reshape(2, nb, 1, LANE), h0]
    return pl.pallas_call(
        functools.partial(_lru_body, L=L),
        grid=(nseq, nb),
        in_specs=in_specs,
        out_specs=[pl.BlockSpec((L, LANE), lambda s, h: (s, h)),
                   pl.BlockSpec((None, 2, LANE), lambda s, h: (s, 0, h))],
        out_shape=[jax.ShapeDtypeStruct((nseq * L, D_A), BF16),
                   jax.ShapeDtypeStruct((nseq, 2, D_A), F32)],
        scratch_shapes=[pltpu.VMEM((L, LANE), F32)] * 6,
        compiler_params=_cparams(2),
        name="rglru",
    )(*args)


def _tri_mask(Tc, d):
    r = lax.broadcasted_iota(jnp.int32, (Tc, Tc), 0)
    c = lax.broadcasted_iota(jnp.int32, (Tc, Tc), 1)
    return (c - r) * (1 - 2 * d) <= 0


def _mlstm_dir(d, q_ref, k_ref, v_ref, g_ref, gb_ref, h_ref, C_s, n_s, m_s, *, Tc, k_scale, dh):
    H = B_HEADS
    last = Tc - 1 if d == 0 else 0
    lane = lax.broadcasted_iota(jnp.int32, (Tc, LANE), 1)
    G = g_ref[...] + gb_ref[...]
    tri = _tri_mask(Tc, d)
    Bm = jnp.dot(tri.astype(F32), _log_sigmoid(G), precision=HI, preferred_element_type=F32)
    col0 = d * (2 * H)
    X = jnp.where(lane >= col0 + H, Bm, G)
    srow = lax.broadcasted_iota(jnp.int32, (2 * H, LANE), 0)
    slane = lax.broadcasted_iota(jnp.int32, (2 * H, LANE), 1)
    sel = (slane == col0 + srow).astype(F32)
    R = lax.dot_general(sel, X, NT_DIMS, precision=HI, preferred_element_type=F32)
    for hd in range(H):
        i_col = jnp.sum(jnp.where(lane == col0 + hd, X, 0.0), axis=1, keepdims=True)
        b_col = jnp.sum(jnp.where(lane == col0 + H + hd, X, 0.0), axis=1, keepdims=True)
        i_row = R[hd:hd + 1, :]
        b_row = R[H + hd:H + hd + 1, :]
        dmat = jnp.where(tri, b_col - b_row + i_row, -jnp.inf)
        m_prev = m_s[d, hd]
        m_inter = b_col + m_prev
        m_t = jnp.maximum(m_inter, jnp.max(dmat, axis=1, keepdims=True))
        cs = slice(hd * dh, (hd + 1) * dh)
        qf = q_ref[:, cs]
        qb = qf.astype(BF16)
        kf = k_ref[:, cs] * k_scale
        kb = kf.astype(BF16)
        vb = v_ref[:, cs].astype(BF16)
        S = lax.dot_general(qb, kb, NT_DIMS, preferred_element_type=F32) * jnp.exp(dmat - m_t)
        inter_scale = jnp.exp(m_inter - m_t)
        Cm = C_s[d, hd]
        num =(jnp.dot(S.astype(BF16), vb, preferred_element_type=F32)
               + inter_scale * jnp.dot(qb, Cm.astype(BF16), preferred_element_type=F32))
        nv = n_s[d, hd]
        qn = jnp.sum(qf * nv, axis=1, keepdims=True)
        den = jnp.sum(S, axis=1, keepdims=True) + inter_scale * qn
        h_ref[:, cs] = num / jnp.maximum(jnp.abs(den), jnp.exp(-m_t))
        bL = b_col[last:last + 1, :]
        g_col = bL - b_col + i_col
        m_new = jnp.maximum(bL + m_prev, jnp.max(g_col, axis=0, keepdims=True))
        wk = jnp.exp(g_col - m_new)
        decay = jnp.exp(bL + m_prev - m_new)
        kw = kf * wk
        C_s[d, hd] = decay * Cm + lax.dot_general(kw.astype(BF16), vb, TN_DIMS, preferred_element_type=F32)
        n_s[d, hd] = decay * nv + jnp.sum(kw, axis=0, keepdims=True)
        m_s[d, hd] = m_new


def _mlstm_body(qf_ref, kf_ref, vf_ref, gf_ref, qb_ref, kb_ref, vb_ref, gbk_ref, gb_ref, c0_ref, n0_ref, m0_ref,
                hf_ref, hb_ref, co_ref, no_ref, mo_ref, C_s, n_s, m_s, *, nc, **kw):
    c = pl.program_id(1)

    @pl.when(c == 0)
    def _():
        C_s[...] = c0_ref[...]
        n_s[...] = n0_ref[...]
        m_s[...] = m0_ref[...]

    _mlstm_dir(0, qf_ref, kf_ref, vf_ref, gf_ref, gb_ref, hf_ref, C_s, n_s, m_s, **kw)
    _mlstm_dir(1, qb_ref, kb_ref, vb_ref, gbk_ref, gb_ref, hb_ref, C_s, n_s, m_s, **kw)

    @pl.when(c == nc - 1)
    def _():
        co_ref[...] = C_s[...]
        no_ref[...] = n_s[...]
        mo_ref[...] = m_s[...]


def _mlstm(proj, gates, gate_bias, row0, nseq, L, col0, c0, n0, m0):
    H = B_HEADS
    dh = c0.shape[-1]
    D_B = H * dh
    Tc = min(SCAN_CHUNK, L)
    nc = L // Tc
    rb0 = row0 // Tc
    cb0 = col0 // D_B
    fwd = lambda s, c: s * nc + c
    bwd = lambda s, c: s * nc + nc - 1 - c

    def chunk_specs(rb):
        return [pl.BlockSpec((Tc, D_B), lambda s, c, off=off: (rb0 + rb(s, c), cb0 + off)) for off in range(3)] + [
            pl.BlockSpec((Tc, LANE), lambda s, c: (rb0 + rb(s, c), 0))]

    def st_specs():
        st_map = lambda s, c: (s, 0, 0, 0, 0)
        return [pl.BlockSpec((None, 2, H, dh, dh), st_map), pl.BlockSpec((None, 2, H, 1, dh), st_map),
                pl.BlockSpec((None, 2, H, 1, 1), st_map)]

    hf, hb, C, n, m = pl.pallas_call(
        functools.partial(_mlstm_body, Tc=Tc, nc=nc, k_scale=dh ** -0.5, dh=dh),
        grid=(nseq, nc),
        in_specs=chunk_specs(fwd) + chunk_specs(bwd) + [pl.BlockSpec((1, LANE), lambda s, c: (0, 0))] + st_specs(),
        out_specs=[pl.BlockSpec((Tc, D_B), lambda s, c: (fwd(s, c), 0)),
                   pl.BlockSpec((Tc, D_B), lambda s, c: (bwd(s, c), 0))] + st_specs(),
        out_shape=[jax.ShapeDtypeStruct((nseq * L, D_B), F32),
                   jax.ShapeDtypeStruct((nseq * L, D_B), F32),
                   jax.ShapeDtypeStruct((nseq, 2, H, dh, dh), F32),
                   jax.ShapeDtypeStruct((nseq, 2, H, 1, dh), F32),
                   jax.ShapeDtypeStruct((nseq, 2, H, 1, 1), F32)],
        scratch_shapes=[pltpu.VMEM((2, H, dh, dh), F32), pltpu.VMEM((2, H, 1, dh), F32),
                        pltpu.VMEM((2, H, 1, 1), F32)],
        compiler_params=_cparams(2),
        name="mlstm",
    )(*([proj] * 3 + [gates]) * 2, gate_bias, c0, n0.reshape(nseq, 2, H, 1, dh), m0.reshape(nseq, 2, H, 1, 1))
    return (hf, hb), C, n, m


def _pair_specs(pair, block, tm):
    npt = pair[0].shape[-2] // tm
    (shape, _), mk = block(0), block
    return npt, [pl.BlockSpec(shape, lambda i, *rest: mk(jnp.minimum(i, npt - 1), *rest)[1]),
                 pl.BlockSpec(shape, lambda i, *rest: mk(jnp.maximum(i - npt, 0), *rest)[1])]


def _pick(p_ref, s_ref, npt):
    return jnp.where(pl.program_id(0) < npt, p_ref[...], s_ref[...])


def _mlstm_out_body(hfp_ref, hfs_ref, hbp_ref, hbs_ref, o_ref, g_ref, y_ref, *, npt):
    hm = _pick(hfp_ref, hfs_ref, npt) + _pick(hbp_ref, hbs_ref, npt)
    mu = jnp.mean(hm, axis=1, keepdims=True)
    xc = hm - mu
    var = jnp.mean(xc * xc, axis=1, keepdims=True)
    y_ref[...] = (_sigmoid(o_ref[...]) * (xc * lax.rsqrt(var + LN_EPS) * g_ref[...])).astype(y_ref.dtype)


def _mlstm_out(h_fwd, h_bwd, proj, col0, norm_g, tm):
    T = proj.shape[0]
    D_B = h_fwd[0].shape[-1]
    dh = D_B // B_HEADS
    cb0 = col0 // dh
    blk = lambda r, h=0: ((tm, dh), (r, h))
    npt, f_specs = _pair_specs(h_fwd, blk, tm)
    _, b_specs = _pair_specs(h_bwd, blk, tm)
    return pl.pallas_call(
        functools.partial(_mlstm_out_body, npt=npt),
        grid=(T // tm, B_HEADS),
        in_specs=f_specs + b_specs + [pl.BlockSpec((tm, dh), lambda i, h: (i, cb0 + h)),
                                      pl.BlockSpec((1, dh), lambda i, h: (0, h))],
        out_specs=pl.BlockSpec((tm, dh), lambda i, h: (i, h)),
        out_shape=jax.ShapeDtypeStruct((T, D_B), BF16),
        compiler_params=_cparams(2),
        name="mlstm_out",
    )(*h_fwd, *h_bwd, proj, norm_g.reshape(1, D_B))


def _split3(x):
    hi = x.astype(BF16)
    r1 = x - hi.astype(F32)
    mid = r1.astype(BF16)
    return hi, mid, (r1 - mid.astype(F32)).astype(BF16)


def _gla_dir(d, q_ref, k_ref, v_ref, lr_ref, gw_ref, gb_ref, o_ref, ST_s, *, Tc, q_scale, dk, dv):
    H = C_HEADS
    last = Tc - 1 if d == 0 else 0
    lr_hi, lr_mid, _ = _split3(lr_ref[...])
    gw_hi, gw_mid, _ = _split3(gw_ref[d])
    z = (jnp.dot(lr_hi, gw_hi, preferred_element_type=F32)
         + jnp.dot(lr_hi, gw_mid, preferred_element_type=F32)
         + jnp.dot(lr_mid, gw_hi, preferred_element_type=F32)) + gb_ref[d]
    loga = _log_sigmoid(z) * (1.0 / GLA_TAU)
    tri = _tri_mask(Tc, d)
    trib = tri.astype(F32).astype(BF16)
    b = sum(jnp.dot(trib, piece, preferred_element_type=F32) for piece in _split3(loga))
    qs = (q_ref[...] * q_scale * jnp.exp(b)).astype(BF16)
    kf = k_ref[...]
    ke = (kf * jnp.exp(-b)).astype(BF16)
    bL = b[last:last + 1, :]
    kd = (kf * jnp.exp(bL - b)).astype(BF16)
    ebL = jnp.exp(bL)
    for hd in range(H):
        ks = slice(hd * dk, (hd + 1) * dk)
        vs = slice(hd * dv, (hd + 1) * dv)
        vb = v_ref[:, vs].astype(BF16)
        att = jnp.where(tri, lax.dot_general(qs[:, ks], ke[:, ks], NT_DIMS, preferred_element_type=F32), 0.0)
        ST = ST_s[d, hd]
        inter = lax.dot_general(qs[:, ks], ST.astype(BF16), NT_DIMS, preferred_element_type=F32)
        o_ref[:, vs] = inter + jnp.dot(att.astype(BF16), vb, preferred_element_type=F32)
        ST_s[d, hd] = ST * ebL[:, ks] + lax.dot_general(vb, kd[:, ks], TN_DIMS, preferred_element_type=F32)


def _gla_body(qf_ref, kf_ref, vf_ref, lrf_ref, qb_ref, kb_ref, vb_ref, lrb_ref, gw_ref, gb_ref, s0_ref,
              of_ref, ob_ref, so_ref, ST_s, *, nc, **kw):
    c = pl.program_id(1)
    states = [(d, hd) for d in range(2) for hd in range(C_HEADS)]

    @pl.when(c == 0)
    def _():
        for d, hd in states:
            ST_s[d, hd] = s0_ref[d, hd].T

    _gla_dir(0, qf_ref, kf_ref, vf_ref, lrf_ref, gw_ref, gb_ref, of_ref, ST_s, **kw)
    _gla_dir(1, qb_ref, kb_ref, vb_ref, lrb_ref, gw_ref, gb_ref, ob_ref, ST_s, **kw)

    @pl.when(c == nc - 1)
    def _():
        for d, hd in states:
            so_ref[d, hd] = ST_s[d, hd].T


def _gla(proj, lr, gw_pad, gate_b, row0, nseq, L, s0):
    H = C_HEADS
    dk, dv = s0.shape[-2:]
    DK, DV = H * dk, H * dv
    Tc = min(SCAN_CHUNK, L)
    nc = L // Tc
    rb0 = row0 // Tc
    fwd = lambda s, c: s * nc + c
    bwd = lambda s, c: s * nc + nc - 1 - c

    def chunk_specs(rb):
        return [pl.BlockSpec((Tc, DK), lambda s, c: (rb0 + rb(s, c), 0)),
                pl.BlockSpec((Tc, DK), lambda s, c: (rb0 + rb(s, c), 1)),
                pl.BlockSpec((Tc, DV), lambda s, c: (rb0 + rb(s, c), (2 * DK) // DV)),
                pl.BlockSpec((Tc, LANE), lambda s, c: (rb0 + rb(s, c), 0))]

    st_map = lambda s, c: (s, 0, 0, 0, 0)
    of, ob, S = pl.pallas_call(
        functools.partial(_gla_body, Tc=Tc, nc=nc, q_scale=dk ** -0.5, dk=dk, dv=dv),
        grid=(nseq, nc),
        in_specs=chunk_specs(fwd) + chunk_specs(bwd) + [
            pl.BlockSpec((2, LANE, DK), lambda s, c: (0, 0, 0)),
            pl.BlockSpec((2, 1, DK), lambda s, c: (0, 0, 0)),
            pl.BlockSpec((None, 2, H, dk, dv), st_map)],
        out_specs=[pl.BlockSpec((Tc, DV), lambda s, c: (fwd(s, c), 0)),
                   pl.BlockSpec((Tc, DV), lambda s, c: (bwd(s, c), 0)),
                   pl.BlockSpec((None, 2, H, dk, dv), st_map)],
        out_shape=[jax.ShapeDtypeStruct((nseq * L, DV), F32),
                   jax.ShapeDtypeStruct((nseq * L, DV), F32),
                   jax.ShapeDtypeStruct((nseq, 2, H, dk, dv), F32)],
        scratch_shapes=[pltpu.VMEM((2, H, dv, dk), F32)],
        compiler_params=_cparams(2),
        name="gla",
    )(*([proj] * 3 + [lr]) * 2, gw_pad, gate_b.reshape(2, 1, DK), s0)
    return (of, ob), S


def _gla_out_body(ofp_ref, ofs_ref, obp_ref, obs_ref, g_ref, ng_ref, y_ref, *, npt):
    oo = _pick(ofp_ref, ofs_ref, npt) + _pick(obp_ref, obs_ref, npt)
    ms = jnp.mean(oo * oo, axis=1, keepdims=True)
    y_ref[...] = (oo * lax.rsqrt(ms + LN_EPS) * ng_ref[...] * _silu(g_ref[...])).astype(y_ref.dtype)


def _gla_out(o_fwd, o_bwd, proj, col0, norm_g, tm):
    T = proj.shape[0]
    DV = o_fwd[0].shape[-1]
    dv = DV // C_HEADS
    cb0 = col0 // dv
    blk = lambda r, h=0: ((tm, dv), (r, h))
    npt, f_specs = _pair_specs(o_fwd, blk, tm)
    _, b_specs = _pair_specs(o_bwd, blk, tm)
    return pl.pallas_call(
        functools.partial(_gla_out_body, npt=npt),
        grid=(T // tm, C_HEADS),
        in_specs=f_specs + b_specs + [pl.BlockSpec((tm, dv), lambda i, h: (i, cb0 + h)),
                                      pl.BlockSpec((1, dv), lambda i, h: (0, h))],
        out_specs=pl.BlockSpec((tm, dv), lambda i, h: (i, h)),
        out_shape=jax.ShapeDtypeStruct((T, DV), BF16),
        compiler_params=_cparams(2),
        name="gla_out",
    )(*o_fwd, *o_bwd, proj, norm_g.reshape(1, DV))


def _layer_norm(z, g, b):
    mu = jnp.mean(z, axis=1, keepdims=True)
    zc = z - mu
    var = jnp.mean(zc * zc, axis=1, keepdims=True)
    return zc * lax.rsqrt(var + LN_EPS) * g + b


def _outproj_body(*refs, npts, g_row, alpha):
    n = len(npts)
    n_y = sum(1 if p is None else 2 for p in npts)
    y_refs = list(refs[:n_y])
    ws = refs[n_y:n_y + n]
    x_ref, m_ref, lg_ref, lb_ref, o_ref = refs[n_y + n:]
    acc = None
    for npt, w in zip(npts, ws):
        y = y_refs.pop(0)[...] if npt is None else _pick(y_refs.pop(0), y_refs.pop(0), npt)
        part = jnp.dot(y, w[...], preferred_element_type=F32)
        acc = part if acc is None else acc + part
    z = alpha * x_ref[...] + m_ref[g_row:g_row + 1, :] * acc
    o_ref[...] = _layer_norm(z, lg_ref[...], lb_ref[...])


def _outproj_ln(ys, ws, x, modt, ln_g, ln_b, *, g_row, alpha, tm):
    T, D = x.shape
    mod4, mod_map = modt[0], modt[1](tm)
    y_specs, y_args, npts = [], [], []
    for y in ys:
        if isinstance(y, tuple):
            K = y[0].shape[1]
            npt, specs = _pair_specs(y, lambda r, K=K: ((tm, K), (r, 0)), tm)
            y_specs += specs
            y_args += list(y)
            npts.append(npt)
        else:
            y_specs.append(pl.BlockSpec((tm, y.shape[1]), lambda i: (i, 0)))
            y_args.append(y)
            npts.append(None)
    in_specs = (y_specs
                + [pl.BlockSpec(w.shape, lambda i: (0, 0)) for w in ws]
                + [pl.BlockSpec((tm, D), lambda i: (i, 0)),
                   pl.BlockSpec((None, None, 6, D), lambda i: mod_map(i)),
                   pl.BlockSpec((1, D), lambda i: (0, 0)),
                   pl.BlockSpec((1, D), lambda i: (0, 0))])
    return pl.pallas_call(
        functools.partial(_outproj_body, npts=tuple(npts), g_row=g_row, alpha=alpha),
        grid=(T // tm,),
        in_specs=in_specs,
        out_specs=pl.BlockSpec((tm, D), lambda i: (i, 0)),
        out_shape=jax.ShapeDtypeStruct((T, D), F32),
        compiler_params=_cparams(1),
        name="outproj_ln",
    )(*y_args, *ws, x, mod4, ln_g.reshape(1, D), ln_b.reshape(1, D))


def _router_body(x_ref, m_ref, w_ref, b_ref, h_ref, ids_ref, gts_ref, *, sh_row, sc_row):
    h = x_ref[...] * (1.0 + m_ref[sc_row:sc_row + 1, :]) + m_ref[sh_row:sh_row + 1, :]
    _to_tok_blocks(h_ref, h)
    logits = lax.dot_general(w_ref[...], h, NT_DIMS, precision=HI, preferred_element_type=F32) + b_ref[...]
    E, tm = logits.shape
    row = lax.broadcasted_iota(jnp.int32, (E, tm), 0)
    mx = jnp.max(logits, axis=0, keepdims=True)
    ex = jnp.exp(logits - mx)
    p = ex / jnp.sum(ex, axis=0, keepdims=True)
    grp = row // (E // N_GROUPS)
    best = None
    for g in range(N_GROUPS):
        pg = jnp.where(grp == g, p, -1.0)
        v1 = jnp.max(pg, axis=0, keepdims=True)
        i1 = jnp.min(jnp.where(pg == v1, row, E), axis=0, keepdims=True)
        pg2 = jnp.where(row == i1, -1.0, pg)
        v2 = jnp.max(pg2, axis=0, keepdims=True)
        i2 = jnp.min(jnp.where(pg2 == v2, row, E), axis=0, keepdims=True)
        score = v1 + v2
        if best is None:
            best = (score, v1, i1, v2, i2)
        else:
            take = score > best[0]
            best = tuple(jnp.where(take, n, o) for n, o in zip((score, v1, i1, v2, i2), best))
    _, v1, i1, v2, i2 = best
    tot = v1 + v2
    slot = lax.broadcasted_iota(jnp.int32, (SUBLANE, tm), 0)
    ids_ref[...] = jnp.where(slot == 0, i1, jnp.where(slot == 1, i2, 0))
    gts_ref[...] = jnp.where(slot == 0, v1 / tot, jnp.where(slot == 1, v2 / tot, 0.0))


def _router(x, modt, router_w, router_b, *, sh_row, sc_row, tm):
    T, D = x.shape
    E = router_w.shape[1]
    mod4, mod_map = modt[0], modt[1](tm)
    return pl.pallas_call(
        functools.partial(_router_body, sh_row=sh_row, sc_row=sc_row),
        grid=(T // tm,),
        in_specs=[pl.BlockSpec((tm, D), lambda i: (i, 0)),
                  pl.BlockSpec((None, None, 6, D), lambda i: mod_map(i)),
                  pl.BlockSpec((E, D), lambda i: (0, 0)),
                  pl.BlockSpec((E, 1), lambda i: (0, 0))],
        out_specs=[pl.BlockSpec((tm * TOK_ROWS, LANE), lambda i: (i, 0)),
                   pl.BlockSpec((SUBLANE, tm), lambda i: (0, i)),
                   pl.BlockSpec((SUBLANE, tm), lambda i: (0, i))],
        out_shape=[jax.ShapeDtypeStruct((T * TOK_ROWS, LANE), F32),
                   jax.ShapeDtypeStruct((SUBLANE, T), jnp.int32),
                   jax.ShapeDtypeStruct((SUBLANE, T), F32)],
        compiler_params=_cparams(1),
        name="router",
    )(x, mod4, router_w.T, router_b.reshape(E, 1))


def _to_tok_blocks(ref, x):
    tm = x.shape[0]
    for s in range(TOK_ROWS):
        ref[pl.ds(s, tm, stride=TOK_ROWS), :] = x[:, s * LANE:(s + 1) * LANE]


def _tok_chunk(ref, s, tm):
    return ref[pl.ds(s, tm, stride=TOK_ROWS), :]


class _RowGather:
    def __init__(self, src_hbm, idx_hbm, idx_s, bufs, isem, gsem, tm):
        self.src, self.idx_hbm, self.idx_s, self.bufs = src_hbm, idx_hbm, idx_s, bufs
        self.isem, self.gsem, self.tm = isem, gsem, tm

    def _idx_copy(self, tile, slot):
        return pltpu.make_async_copy(self.idx_hbm.at[tile], self.idx_s.at[slot], self.isem.at[slot])

    def _row_copy(self, slot, a, r):
        row = pl.multiple_of(self.idx_s[slot, a, r], TOK_ROWS)
        dst = pl.multiple_of(r * TOK_ROWS, TOK_ROWS)
        return pltpu.make_async_copy(self.src.at[pl.ds(row, TOK_ROWS), :],
                                     self.bufs[a].at[slot, pl.ds(dst, TOK_ROWS), :], self.gsem.at[slot, a])

    def rows_loop(self, slot, start):
        def body(r, carry):
            for a in range(len(self.bufs)):
                cp = self._row_copy(slot, a, r)
                cp.start() if start else cp.wait()
            return carry
        lax.fori_loop(0, self.tm, body, 0, unroll=8)

    def rows_start_inline(self, slot):
        for r in range(self.tm):
            for a in range(len(self.bufs)):
                self._row_copy(slot, a, r).start(priority=(r + a) % 2)

    def prologue(self, n_tiles):
        self._idx_copy(0, 0).start()
        self._idx_copy(0, 0).wait()
        self.rows_loop(0, True)
        if n_tiles > 1:
            self._idx_copy(1, 1).start()

    def advance_indices(self, i, n_tiles):
        slot = i % 2

        @pl.when(i + 1 < n_tiles)
        def _():
            self._idx_copy(i + 1, 1 - slot).wait()

        @pl.when(i + 2 < n_tiles)
        def _():
            self._idx_copy(i + 2, slot).start()


def _ffn_body(te_ref, va_ref, idx_hbm, h_hbm, wi_ref, wo_ref, o_ref, xbuf, x16, idx_s, isem, gsem,
              *, F, tm, n_tiles):
    i = pl.program_id(0)
    slot = i % 2
    valid = va_ref[i] != 0
    first_invalid = jnp.logical_and(jnp.logical_not(valid), va_ref[jnp.maximum(i - 1, 0)] != 0)
    g = _RowGather(h_hbm, idx_hbm, idx_s, [xbuf], isem, gsem, tm)

    @pl.when(i == 0)
    def _():
        g.prologue(n_tiles)

    @pl.when(valid)
    def _():
        g.rows_loop(slot, False)
        g.advance_indices(i, n_tiles)

    @pl.when(valid)
    def _():
        g.rows_start_inline(1 - slot)
        xb = xbuf.at[slot]
        for s in range(TOK_ROWS):
            x16[:, s * LANE:(s + 1) * LANE] = _tok_chunk(xb, s, tm).astype(BF16)
        uw = jnp.dot(x16[...], wi_ref[...], preferred_element_type=F32)
        act = (_silu(uw[:, :F]) * uw[:, F:]).astype(BF16)
        _to_tok_blocks(o_ref, jnp.dot(act, wo_ref[...], preferred_element_type=F32))

    @pl.when(jnp.logical_not(valid))
    def _():
        o_ref[...] = jnp.zeros_like(o_ref)

    @pl.when(first_invalid)
    def _():
        g.rows_loop(slot, False)

        @pl.when(i + 1 < n_tiles)
        def _():
            g._idx_copy(i + 1, 1 - slot).wait()


def _expert_ffn(h, src_tok, w_in, w_out, layer, tile_e, tile_valid):
    n_tiles, _, tm = src_tok.shape
    F, D = w_out.shape[2:]
    return pl.pallas_call(
        functools.partial(_ffn_body, F=F, tm=tm, n_tiles=n_tiles),
        grid_spec=pltpu.PrefetchScalarGridSpec(
            num_scalar_prefetch=2,
            grid=(n_tiles,),
            in_specs=[pl.BlockSpec(memory_space=pl.ANY),
                      pl.BlockSpec(memory_space=pl.ANY),
                      pl.BlockSpec((None, None, D, 2 * F), lambda i, te, va: (layer, te[i], 0, 0)),
                      pl.BlockSpec((None, None, F, D), lambda i, te, va: (layer, te[i], 0, 0))],
            out_specs=pl.BlockSpec((tm * TOK_ROWS, LANE), lambda i, te, va: (i, 0)),
            scratch_shapes=[pltpu.VMEM((2, tm * TOK_ROWS, LANE), F32), pltpu.VMEM((tm, D), BF16),
                            pltpu.SMEM((2, 1, tm), jnp.int32),
                            pltpu.SemaphoreType.DMA((2,)), pltpu.SemaphoreType.DMA((2, 1))]),
        out_shape=jax.ShapeDtypeStruct((n_tiles * tm * TOK_ROWS, LANE), F32),
        compiler_params=_cparams(1),
        name="expert_ffn",
    )(tile_e, tile_valid, src_tok, h, w_in, w_out)


def _combine_body(x_ref, idx_hbm, y_hbm, gt_ref, m_ref, lg_ref, lb_ref, o_ref, y0buf, y1buf, idx_s, isem, gsem,
                  *, g_row, alpha, tm, n_tiles):
    i = pl.program_id(0)
    slot = i % 2
    g = _RowGather(y_hbm, idx_hbm, idx_s, [y0buf, y1buf], isem, gsem, tm)

    @pl.when(i == 0)
    def _():
        g.prologue(n_tiles)

    g.rows_loop(slot, False)
    g.advance_indices(i, n_tiles)
    if n_tiles > 1:
        g.rows_start_inline(1 - slot)
    gt = gt_ref[...]
    g0, g1 = gt[:, 0:1], gt[:, 1:2]
    y0b, y1b = y0buf.at[slot], y1buf.at[slot]
    moe = jnp.concatenate([g0 * _tok_chunk(y0b, s, tm) + g1 * _tok_chunk(y1b, s, tm) for s in range(TOK_ROWS)],
                          axis=1)
    z = alpha * x_ref[...] + m_ref[g_row:g_row + 1, :] * moe
    o_ref[...] = _layer_norm(z, lg_ref[...], lb_ref[...])

    if n_tiles > 1:
        @pl.when(i == n_tiles - 1)
        def _():
            g.rows_loop(1 - slot, False)


def _combine_ln(x, ys, dest, gts, modt, ln_g, ln_b, *, g_row, alpha):
    T, D = x.shape
    n_tiles, _, tm = dest.shape
    mod4, mod_map = modt[0], modt[1](tm)
    row = pl.BlockSpec((tm, D), lambda i: (i, 0))
    vec = pl.BlockSpec((1, D), lambda i: (0, 0))
    hbm = pl.BlockSpec(memory_space=pl.ANY)
    return pl.pallas_call(
        functools.partial(_combine_body, g_row=g_row, alpha=alpha, tm=tm, n_tiles=n_tiles),
        grid=(n_tiles,),
        in_specs=[row, hbm, hbm, pl.BlockSpec((tm, LANE), lambda i: (i, 0)),
                  pl.BlockSpec((None, None, 6, D), lambda i: mod_map(i)), vec, vec],
        out_specs=row,
        out_shape=jax.ShapeDtypeStruct((T, D), F32),
        scratch_shapes=[pltpu.VMEM((2, tm * TOK_ROWS, LANE), F32), pltpu.VMEM((2, tm * TOK_ROWS, LANE), F32),
                        pltpu.SMEM((2, 2, tm), jnp.int32),
                        pltpu.SemaphoreType.DMA((2,)), pltpu.SemaphoreType.DMA((2, 2))],
        compiler_params=_cparams(1),
        name="combine_ln",
    )(x, dest, ys, gts, mod4, ln_g.reshape(1, D), ln_b.reshape(1, D))


def _route_meta(ids, tm, tm_tok):
    T = ids.shape[0]
    E = N_EXPERTS
    e_flat = ids.reshape(-1)
    onehot = (e_flat[:, None] == jnp.arange(E, dtype=jnp.int32)[None, :]).astype(jnp.int32)
    csum = jnp.cumsum(onehot, axis=0)
    rank = jnp.sum(csum * onehot, axis=1) - 1
    counts = csum[-1]
    padded = ((counts + tm - 1) // tm) * tm
    ends = jnp.cumsum(padded)
    starts = ends - padded
    dest = jnp.sum(starts[None, :] * onehot, axis=1) + rank
    n_rows = 2 * T + E * tm
    src_tok = jnp.zeros((n_rows,), jnp.int32).at[dest].set(jnp.arange(2 * T, dtype=jnp.int32) // 2)
    tile_start = jnp.arange(n_rows // tm, dtype=jnp.int32) * tm
    valid = (tile_start < ends[-1]).astype(jnp.int32)
    probe = jnp.minimum(tile_start, ends[-1] - 1)
    tile_e = jnp.sum((ends[None, :] <= probe[:, None]).astype(jnp.int32), axis=1)
    dest_t = dest.reshape(T // tm_tok, tm_tok, 2).transpose(0, 2, 1)
    return (dest_t * TOK_ROWS, src_tok.reshape(n_rows // tm, 1, tm) * TOK_ROWS, jnp.minimum(tile_e, E - 1), valid)


def _pos_embed_2d(n_tokens, dim):
    rows = n_tokens // GRID_W
    quarter = dim // 4
    freqs = jnp.exp(-math.log(10000.0) * jnp.arange(quarter, dtype=F32) / quarter)
    r = jnp.broadcast_to(jnp.arange(rows, dtype=F32)[:, None], (rows, GRID_W)).reshape(-1)
    col = jnp.broadcast_to(jnp.arange(GRID_W, dtype=F32)[None, :], (rows, GRID_W)).reshape(-1)
    ar = r[:, None] * freqs
    ac = col[:, None] * freqs
    return jnp.concatenate([jnp.sin(ar), jnp.cos(ar), jnp.sin(ac), jnp.cos(ac)], -1)


def _pad_cols(w, n):
    return jnp.pad(w, ((0, 0),) * (w.ndim - 1) + ((0, n - w.shape[-1]),))


def kernel(x_prompt, x_sample, state_lru, state_mlstm_c, state_mlstm_n, state_mlstm_m, state_gla, c, c_ctx, mod_w, mod_b, ln_g, ln_b, even_w_in, even_w_out, lru_conv_w, lru_conv_b, lru_gate_w, lru_gate_b, lru_lambda, mlstm_gate_b, mlstm_norm_g, odd_w_in, odd_w_out, gla_gate_w, gla_gate_b, gla_norm_g, router_w, router_b, moe_w_in, moe_w_out):
    Bp, Lp, D = x_prompt.shape
    Bs, Ls, _ = x_sample.shape
    depth = mod_w.shape[0]
    Tp, Ts = Bp * Lp, Bs * Ls
    T = Tp + Ts
    alpha = (2 * depth) ** 0.25
    D_A = lru_conv_w.shape[-1]
    D_B = mlstm_norm_g.shape[-1]
    DK = gla_gate_w.shape[-1]
    DV = gla_norm_g.shape[-1]
    tm = next(t for t in (512, 256, 128) if Tp % t == 0 and Ls % t == 0)
    tn = 1024

    n_cond = 1 + Bs
    R = -(-n_cond // SUBLANE) * SUBLANE
    cond = jnp.zeros((R, D), F32).at[0].set(c_ctx).at[1:n_cond].set(c)
    mod_all = _modulation(cond, mod_w, mod_b)
    mod4 = mod_all.reshape(depth, R, 6, D)
    tm_proj = next(t for t in (1024, 512, 256, 128) if Tp % t == 0 and Ls % t == 0)
    assert D == TOK_ROWS * LANE

    x = _embed(x_prompt.reshape(Tp, D), x_sample.reshape(Ts, D), _pos_embed_2d(Ls, D), tm)

    moe_w_in_b = moe_w_in.astype(BF16)
    moe_w_out_b = moe_w_out.astype(BF16)
    groups = ((0, Bp, Lp), (Tp, Bs, Ls))

    s_lru, s_c, s_n, s_m, s_gla = [], [], [], [], []
    for l in range(depth):
        j = l // 2
        mt = (mod4, lambda t, l=l: (lambda i: (l, jnp.where(i < Tp // t, 0, 1 + (i - Tp // t) // (Ls // t)), 0, 0)))
        if l % 2 == 0:
            w_in = even_w_in[j]
            n_main = 2 * D_A + 4 * D_B
            proj = _proj(x, mt, w_in[:, :n_main].astype(BF16), sh_row=0, sc_row=1, tm=tm_proj, tn=tn)
            gates = _proj(x, mt, _pad_cols(w_in[:, n_main:], LANE), sh_row=0, sc_row=1, tm=tm, tn=LANE, hp=True)
            gate_bias = _pad_cols(mlstm_gate_b[j].reshape(1, -1), LANE)
            ya, hs = [], []
            for gi, (row0, nseq, L) in enumerate(groups):
                if gi == 0:
                    lru0 = jnp.zeros((nseq, 2, D_A), F32)
                    c0 = jnp.zeros((nseq, 2, B_HEADS, D_B // B_HEADS, D_B // B_HEADS), F32)
                    n0 = jnp.zeros((nseq, 2, B_HEADS, D_B // B_HEADS), F32)
                    m0 = jnp.zeros((nseq, 2, B_HEADS), F32)
                else:
                    lru0, c0, n0, m0 = state_lru[:, j], state_mlstm_c[:, j], state_mlstm_n[:, j], state_mlstm_m[:, j]
                y_a, st = _lru(proj, row0, nseq, L, lru_conv_w[j], lru_conv_b[j], lru_gate_w[j], lru_gate_b[j],
                               lru_lambda[j], lru0)
                h, C, n, m = _mlstm(proj, gates, gate_bias, row0, nseq, L, 2 * D_A, c0, n0, m0)
                ya.append(y_a)
                hs.append(h)
                if gi == 0:
                    s_lru.append(st)
                    s_c.append(C)
                    s_n.append(n.reshape(nseq, 2, B_HEADS, -1))
                    s_m.append(m.reshape(nseq, 2, B_HEADS))
            h_fwd, h_bwd = zip(*hs)
            y_b = _mlstm_out(h_fwd, h_bwd, proj, 2 * D_A + 3 * D_B, mlstm_norm_g[j], tm)
            w_out = even_w_out[j].astype(BF16)
            x = _outproj_ln([tuple(ya), y_b], [w_out[:D_A], w_out[D_A:]], x, mt, ln_g[l, 0], ln_b[l, 0],
                            g_row=2, alpha=alpha, tm=tm)
        else:
            w_in = odd_w_in[j]
            n_main = 2 * DK + 2 * DV
            proj = _proj(x, mt, w_in[:, :n_main].astype(BF16), sh_row=0, sc_row=1, tm=tm_proj, tn=tn)
            lr = _proj(x, mt, _pad_cols(w_in[:, n_main:], LANE), sh_row=0, sc_row=1, tm=tm, tn=LANE, hp=True)
            gw_pad = jnp.zeros((2, LANE, DK), F32)
            for dd in range(2):
                gw_pad = gw_pad.at[dd, dd * GLA_RANK:(dd + 1) * GLA_RANK].set(gla_gate_w[j, dd])
            os_ = []
            for gi, (row0, nseq, L) in enumerate(groups):
                s0 = (jnp.zeros((nseq, 2, C_HEADS, DK // C_HEADS, DV // C_HEADS), F32) if gi == 0
                      else state_gla[:, j])
                o, S = _gla(proj, lr, gw_pad, gla_gate_b[j], row0, nseq, L, s0)
                os_.append(o)
                if gi == 0:
                    s_gla.append(S)
            o_fwd, o_bwd = zip(*os_)
            y = _gla_out(o_fwd, o_bwd, proj, 2 * DK + DV, gla_norm_g[j], tm)
            x = _outproj_ln([y], [odd_w_out[j].astype(BF16)], x, mt, ln_g[l, 0], ln_b[l, 0],
                            g_row=2, alpha=alpha, tm=tm)

        hf, ids, gts = _router(x, mt, router_w, router_b, sh_row=3, sc_row=4, tm=tm)
        gts = _pad_cols(gts[:2].T, LANE)
        dest, src_tok, tile_e, tile_valid = _route_meta(ids[:2].T, MOE_TM, tm)
        ys = _expert_ffn(hf, src_tok, moe_w_in_b, moe_w_out_b, l, tile_e, tile_valid)
        x = _combine_ln(x, ys, dest, gts, mt, ln_g[l, 1], ln_b[l, 1], g_row=5, alpha=alpha)

    y_prompt = x[:Tp].reshape(Bp, Lp, D)
    y_sample = x[Tp:].reshape(Bs, Ls, D)
    return (y_prompt, y_sample, jnp.stack(s_lru, 1), jnp.stack(s_c, 1), jnp.stack(s_n, 1),
            jnp.stack(s_m, 1), jnp.stack(s_gla, 1))
```

```python
import functools
import math

import jax
import jax.numpy as jnp
from jax import lax
from jax.experimental import pallas as pl
from jax.experimental.pallas import tpu as pltpu

F32 = jnp.float32
BF16 = jnp.bfloat16
HI = lax.Precision.HIGHEST

LN_EPS = 1e-5
LRU_C = 8.0
A_BLOCKS = 8
B_HEADS = 4
C_HEADS = 4
GLA_RANK = 16
GLA_TAU = 16.0
N_EXPERTS = 16
N_GROUPS = 4
GRID_W = 64
LANE = 128
SUBLANE = 8
SCAN_CHUNK = 256
MOE_TM = 512
FFN_PARTS = 4
TOK_ROWS = 16
VMEM_LIMIT = 56 * 1024 * 1024

NT_DIMS = (((1,), (1,)), ((), ()))
TN_DIMS = (((0,), (0,)), ((), ()))


def _cparams(n_axes):
    return pltpu.CompilerParams(dimension_semantics=("arbitrary",) * n_axes,
                                vmem_limit_bytes=VMEM_LIMIT)


def _sigmoid(x):
    return 1.0 / (1.0 + jnp.exp(-x))


def _silu(x):
    return x * _sigmoid(x)


def _log_sigmoid(x):
    return jnp.minimum(x, 0.0) - jnp.log1p(jnp.exp(-jnp.abs(x)))


def _gelu_tanh(x):
    return 0.5 * x * (1.0 + jnp.tanh(math.sqrt(2.0 / math.pi) * (x + 0.044715 * (x * x * x))))


def _mod_body(c_ref, w_ref, b_ref, o_ref):
    c = c_ref[...]
    o_ref[...] = jnp.dot(_silu(c), w_ref[...], precision=HI, preferred_element_type=F32) + b_ref[...]


def _modulation(cond, mod_w, mod_b):
    R, D = cond.shape
    nl, _, N = mod_w.shape
    tn = 1024
    return pl.pallas_call(
        _mod_body,
        grid=(nl, N // tn),
        in_specs=[pl.BlockSpec((R, D), lambda l, j: (0, 0)),
                  pl.BlockSpec((None, D, tn), lambda l, j: (l, 0, j)),
                  pl.BlockSpec((None, 1, tn), lambda l, j: (l, 0, j))],
        out_specs=pl.BlockSpec((None, R, tn), lambda l, j: (l, 0, j)),
        out_shape=jax.ShapeDtypeStruct((nl, R, N), F32),
        compiler_params=_cparams(2),
        name="modulation",
    )(cond, mod_w, mod_b.reshape(nl, 1, N))


def _embed_body(xp_ref, xs_ref, pos_ref, o_ref, *, n_prompt_tiles):
    i = pl.program_id(0)

    @pl.when(i < n_prompt_tiles)
    def _():
        o_ref[...] = xp_ref[...]

    @pl.when(i >= n_prompt_tiles)
    def _():
        o_ref[...] = xs_ref[...] + pos_ref[...]


def _embed(xp, xs, pos, tm):
    Tp, D = xp.shape
    Ts = xs.shape[0]
    Ls = pos.shape[0]
    npt, nst, npos = Tp // tm, Ts // tm, Ls // tm
    return pl.pallas_call(
        functools.partial(_embed_body, n_prompt_tiles=npt),
        grid=(npt + nst,),
        in_specs=[pl.BlockSpec((tm, D), lambda i: (jnp.minimum(i, npt - 1), 0)),
                  pl.BlockSpec((tm, D), lambda i: (jnp.maximum(i - npt, 0), 0)),
                  pl.BlockSpec((tm, D), lambda i: (jnp.maximum(i - npt, 0) % npos, 0))],
        out_specs=pl.BlockSpec((tm, D), lambda i: (i, 0)),
        out_shape=jax.ShapeDtypeStruct((Tp + Ts, D), F32),
        compiler_params=_cparams(1),
        name="embed",
    )(xp, xs, pos)


def _proj_body(x_ref, m_ref, w_ref, o_ref, h_ref, *, sh_row, sc_row, hp):
    @pl.when(pl.program_id(1) == 0)
    def _():
        h = x_ref[...] * (1.0 + m_ref[sc_row:sc_row + 1, :]) + m_ref[sh_row:sh_row + 1, :]
        h_ref[...] = h.astype(h_ref.dtype)

    if hp:
        o = jnp.dot(h_ref[...], w_ref[...], precision=HI, preferred_element_type=F32)
    else:
        o = jnp.dot(h_ref[...], w_ref[...], preferred_element_type=F32)
    o_ref[...] = o.astype(o_ref.dtype)


def _proj(x, modt, w, *, sh_row, sc_row, tm, tn, hp=False):
    T, D = x.shape
    N = w.shape[1]
    mod4, mod_map = modt[0], modt[1](tm)
    return pl.pallas_call(
        functools.partial(_proj_body, sh_row=sh_row, sc_row=sc_row, hp=hp),
        grid=(T // tm, N // tn),
        in_specs=[pl.BlockSpec((tm, D), lambda i, j: (i, 0)),
                  pl.BlockSpec((None, None, 6, D), lambda i, j: mod_map(i)),
                  pl.BlockSpec((D, tn), lambda i, j: (0, j))],
        out_specs=pl.BlockSpec((tm, tn), lambda i, j: (i, j)),
        out_shape=jax.ShapeDtypeStruct((T, N), F32),
        scratch_shapes=[pltpu.VMEM((tm, D), F32 if hp else BF16)],
        compiler_params=_cparams(2),
        name="proj_hp" if hp else "proj",
    )(x, mod4, w)


def _lru_body(ag_ref, ax_ref, cw_ref, cb_ref, gw_ref, gb_ref, lam_ref, h0_ref, y_ref, st_ref,
              af_s, uf_s, ab_s, ub_s, hf_s, hb_s, *, L):
    x = ax_ref[...]
    row = lax.broadcasted_iota(jnp.int32, (L, LANE), 0)
    xm1 = jnp.where(row >= 1, pltpu.roll(x, 1, 0), 0.0)
    xp1 = jnp.where(row < L - 1, pltpu.roll(x, L - 1, 0), 0.0)
    xp2 = jnp.where(row < L - 2, pltpu.roll(x, L - 2, 0), 0.0)
    xc = cb_ref[...] + xm1 * cw_ref[0:1, :] + x * cw_ref[1:2, :] + xp1 * cw_ref[2:3, :] + xp2 * cw_ref[3:4, :]
    xcb = xc.astype(BF16)
    nj = L // SUBLANE
    sub = lax.broadcasted_iota(jnp.int32, (nj, SUBLANE, LANE), 1)

    for d, (a_s, u_s) in enumerate(((af_s, uf_s), (ab_s, ub_s))):
        g = jnp.dot(xcb, gw_ref[d], preferred_element_type=F32) + gb_ref[d]
        r = _sigmoid(g[:, :LANE])
        ig = _sigmoid(g[:, LANE:])
        nlam = -lam_ref[d]
        softplus = jnp.maximum(nlam, 0.0) + jnp.log1p(jnp.exp(-jnp.abs(nlam)))
        log_a = (-LRU_C * softplus) * r
        a = jnp.exp(log_a)
        u = jnp.sqrt(jnp.tanh(-log_a) * (1.0 + a * a)) * ig * xc
        a3 = a.reshape(nj, SUBLANE, LANE)
        u3 = u.reshape(nj, SUBLANE, LANE)
        for k in (1, 2, 4):
            if d == 0:
                sh, keep = k, sub >= k
            else:
                sh, keep = SUBLANE - k, sub < SUBLANE - k
            a_sh = pltpu.roll(a3, sh, 1)
            u_sh = pltpu.roll(u3, sh, 1)
            u3 = jnp.where(keep, a3 * u_sh + u3, u3)
            a3 = jnp.where(keep, a3 * a_sh, a3)
        a_s[...] = a3.reshape(L, LANE)
        u_s[...] = u3.reshape(L, LANE)

    def carry(j, hs):
        hf, hb = hs
        rf = pl.multiple_of(j * SUBLANE, SUBLANE)
        rb = pl.multiple_of((nj - 1 - j) * SUBLANE, SUBLANE)
        of = af_s[pl.ds(rf, SUBLANE), :] * hf + uf_s[pl.ds(rf, SUBLANE), :]
        ob = ab_s[pl.ds(rb, SUBLANE), :] * hb + ub_s[pl.ds(rb, SUBLANE), :]
        hf_s[pl.ds(rf, SUBLANE), :] = of
        hb_s[pl.ds(rb, SUBLANE), :] = ob
        return (jnp.broadcast_to(of[SUBLANE - 1:SUBLANE, :], (SUBLANE, LANE)),
                jnp.broadcast_to(ob[0:1, :], (SUBLANE, LANE)))

    h0f = jnp.broadcast_to(h0_ref[0:1, :], (SUBLANE, LANE))
    h0b = jnp.broadcast_to(h0_ref[1:2, :], (SUBLANE, LANE))
    lax.fori_loop(0, nj, carry, (h0f, h0b))
    y_ref[...] = (_gelu_tanh(ag_ref[...]) * (hf_s[...] + hb_s[...])).astype(y_ref.dtype)
    st_ref[0:1, :] = hf_s[L - 1:L, :]
    st_ref[1:2, :] = hb_s[0:1, :]


def _lru(proj, row0, nseq, L, conv_w, conv_b, gate_w, gate_b, lam, h0):
    D_A = conv_w.shape[1]
    nb = D_A // LANE
    rb0 = row0 // L
    in_specs = [pl.BlockSpec((L, LANE), lambda s, h: (rb0 + s, h)),
                pl.BlockSpec((L, LANE), lambda s, h: (rb0 + s, nb + h)),
                pl.BlockSpec((4, LANE), lambda s, h: (0, h)),
                pl.BlockSpec((1, LANE), lambda s, h: (0, h)),
                pl.BlockSpec((2, None, LANE, 2 * LANE), lambda s, h: (0, h, 0, 0)),
                pl.BlockSpec((2, None, 1, 2 * LANE), lambda s, h: (0, h, 0, 0)),
                pl.BlockSpec((2, None, 1, LANE), lambda s, h: (0, h, 0, 0)),
                pl.BlockSpec((None, 2, LANE), lambda s, h: (s, 0, h))]
    args = [proj, proj, conv_w, conv_b.reshape(1, D_A), gate_w.astype(BF16),
            gate_b.reshape(2, nb, 1, 2 * LANE), lam.reshape(2, nb, 1, LANE), h0]
    return pl.pallas_call(
        functools.partial(_lru_body, L=L),
        grid=(nseq, nb),
        in_specs=in_specs,
        out_specs=[pl.BlockSpec((L, LANE), lambda s, h: (s, h)),
                   pl.BlockSpec((None, 2, LANE), lambda s, h: (s, 0, h))],
        out_shape=[jax.ShapeDtypeStruct((nseq * L, D_A), BF16),
                   jax.ShapeDtypeStruct((nseq, 2, D_A), F32)],
        scratch_shapes=[pltpu.VMEM((L, LANE), F32)] * 6,
        compiler_params=_cparams(2),
        name="rglru",
    )(*args)


def _tri_mask(Tc, d):
    r = lax.broadcasted_iota(jnp.int32, (Tc, Tc), 0)
    c = lax.broadcasted_iota(jnp.int32, (Tc, Tc), 1)
    return (c - r) * (1 - 2 * d) <= 0


def _mlstm_dir(d, q_ref, k_ref, v_ref, g_ref, gb_ref, h_ref, C_s, n_s, m_s, *, Tc, k_scale, dh):
    H = B_HEADS
    last = Tc - 1 if d == 0 else 0
    lane = lax.broadcasted_iota(jnp.int32, (Tc, LANE), 1)
    G = g_ref[...] + gb_ref[...]
    tri = _tri_mask(Tc, d)
    Bm = jnp.dot(tri.astype(F32), _log_sigmoid(G), precision=HI, preferred_element_type=F32)
    col0 = d * (2 * H)
    X = jnp.where(lane >= col0 + H, Bm, G)
    srow = lax.broadcasted_iota(jnp.int32, (2 * H, LANE), 0)
    slane = lax.broadcasted_iota(jnp.int32, (2 * H, LANE), 1)
    sel = (slane == col0 + srow).astype(F32)
    R = lax.dot_general(sel, X, NT_DIMS, precision=HI, preferred_element_type=F32)
    for hd in range(H):
        i_col = jnp.sum(jnp.where(lane == col0 + hd, X, 0.0), axis=1, keepdims=True)
        b_col = jnp.sum(jnp.where(lane == col0 + H + hd, X, 0.0), axis=1, keepdims=True)
        i_row = R[hd:hd + 1, :]
        b_row = R[H + hd:H + hd + 1, :]
        dmat = jnp.where(tri, b_col - b_row + i_row, -jnp.inf)
        m_prev = m_s[d, hd]
        m_inter = b_col + m_prev
        m_t = jnp.maximum(m_inter, jnp.max(dmat, axis=1, keepdims=True))
        cs = slice(hd * dh, (hd + 1) * dh)
        qf = q_ref[:, cs]
        qb = qf.astype(BF16)
        kf = k_ref[:, cs] * k_scale
        kb = kf.astype(BF16)
        vb = v_ref[:, cs].astype(BF16)
        S = lax.dot_general(qb, kb, NT_DIMS, preferred_element_type=F32) * jnp.exp(dmat - m_t)
        inter_scale = jnp.exp(m_inter - m_t)
        Cm = C_s[d, hd]
        num =(jnp.dot(S.astype(BF16), vb, preferred_element_type=F32)
               + inter_scale * jnp.dot(qb, Cm.astype(BF16), preferred_element_type=F32))
        nv = n_s[d, hd]
        qn = jnp.sum(qf * nv, axis=1, keepdims=True)
        den = jnp.sum(S, axis=1, keepdims=True) + inter_scale * qn
        h_ref[:, cs] = num / jnp.maximum(jnp.abs(den), jnp.exp(-m_t))
        bL = b_col[last:last + 1, :]
        g_col = bL - b_col + i_col
        m_new = jnp.maximum(bL + m_prev, jnp.max(g_col, axis=0, keepdims=True))
        wk = jnp.exp(g_col - m_new)
        decay = jnp.exp(bL + m_prev - m_new)
        kw = kf * wk
        C_s[d, hd] = decay * Cm + lax.dot_general(kw.astype(BF16), vb, TN_DIMS, preferred_element_type=F32)
        n_s[d, hd] = decay * nv + jnp.sum(kw, axis=0, keepdims=True)
        m_s[d, hd] = m_new


def _mlstm_body(qf_ref, kf_ref, vf_ref, gf_ref, qb_ref, kb_ref, vb_ref, gbk_ref, gb_ref, c0_ref, n0_ref, m0_ref,
                hf_ref, hb_ref, co_ref, no_ref, mo_ref, C_s, n_s, m_s, *, nc, **kw):
    c = pl.program_id(1)

    @pl.when(c == 0)
    def _():
        C_s[...] = c0_ref[...]
        n_s[...] = n0_ref[...]
        m_s[...] = m0_ref[...]

    _mlstm_dir(0, qf_ref, kf_ref, vf_ref, gf_ref, gb_ref, hf_ref, C_s, n_s, m_s, **kw)
    _mlstm_dir(1, qb_ref, kb_ref, vb_ref, gbk_ref, gb_ref, hb_ref, C_s, n_s, m_s, **kw)

    @pl.when(c == nc - 1)
    def _():
        co_ref[...] = C_s[...]
        no_ref[...] = n_s[...]
        mo_ref[...] = m_s[...]


def _mlstm(proj, gates, gate_bias, row0, nseq, L, col0, c0, n0, m0):
    H = B_HEADS
    dh = c0.shape[-1]
    D_B = H * dh
    Tc = min(SCAN_CHUNK, L)
    nc = L // Tc
    rb0 = row0 // Tc
    cb0 = col0 // D_B
    fwd = lambda s, c: s * nc + c
    bwd = lambda s, c: s * nc + nc - 1 - c

    def chunk_specs(rb):
        return [pl.BlockSpec((Tc, D_B), lambda s, c, off=off: (rb0 + rb(s, c), cb0 + off)) for off in range(3)] + [
            pl.BlockSpec((Tc, LANE), lambda s, c: (rb0 + rb(s, c), 0))]

    def st_specs():
        st_map = lambda s, c: (s, 0, 0, 0, 0)
        return [pl.BlockSpec((None, 2, H, dh, dh), st_map), pl.BlockSpec((None, 2, H, 1, dh), st_map),
                pl.BlockSpec((None, 2, H, 1, 1), st_map)]

    hf, hb, C, n, m = pl.pallas_call(
        functools.partial(_mlstm_body, Tc=Tc, nc=nc, k_scale=dh ** -0.5, dh=dh),
        grid=(nseq, nc),
        in_specs=chunk_specs(fwd) + chunk_specs(bwd) + [pl.BlockSpec((1, LANE), lambda s, c: (0, 0))] + st_specs(),
        out_specs=[pl.BlockSpec((Tc, D_B), lambda s, c: (fwd(s, c), 0)),
                   pl.BlockSpec((Tc, D_B), lambda s, c: (bwd(s, c), 0))] + st_specs(),
        out_shape=[jax.ShapeDtypeStruct((nseq * L, D_B), F32),
                   jax.ShapeDtypeStruct((nseq * L, D_B), F32),
                   jax.ShapeDtypeStruct((nseq, 2, H, dh, dh), F32),
                   jax.ShapeDtypeStruct((nseq, 2, H, 1, dh), F32),
                   jax.ShapeDtypeStruct((nseq, 2, H, 1, 1), F32)],
        scratch_shapes=[pltpu.VMEM((2, H, dh, dh), F32), pltpu.VMEM((2, H, 1, dh), F32),
                        pltpu.VMEM((2, H, 1, 1), F32)],
        compiler_params=_cparams(2),
        name="mlstm",
    )(*([proj] * 3 + [gates]) * 2, gate_bias, c0, n0.reshape(nseq, 2, H, 1, dh), m0.reshape(nseq, 2, H, 1, 1))
    return (hf, hb), C, n, m


def _pair_specs(pair, block, tm):
    npt = pair[0].shape[-2] // tm
    (shape, _), mk = block(0), block
    return npt, [pl.BlockSpec(shape, lambda i, *rest: mk(jnp.minimum(i, npt - 1), *rest)[1]),
                 pl.BlockSpec(shape, lambda i, *rest: mk(jnp.maximum(i - npt, 0), *rest)[1])]


def _pick(p_ref, s_ref, npt):
    return jnp.where(pl.program_id(0) < npt, p_ref[...], s_ref[...])


def _mlstm_out_body(hfp_ref, hfs_ref, hbp_ref, hbs_ref, o_ref, g_ref, y_ref, *, npt):
    hm = _pick(hfp_ref, hfs_ref, npt) + _pick(hbp_ref, hbs_ref, npt)
    mu = jnp.mean(hm, axis=1, keepdims=True)
    xc = hm - mu
    var = jnp.mean(xc * xc, axis=1, keepdims=True)
    y_ref[...] = (_sigmoid(o_ref[...]) * (xc * lax.rsqrt(var + LN_EPS) * g_ref[...])).astype(y_ref.dtype)


def _mlstm_out(h_fwd, h_bwd, proj, col0, norm_g, tm):
    T = proj.shape[0]
    D_B = h_fwd[0].shape[-1]
    dh = D_B // B_HEADS
    cb0 = col0 // dh
    blk = lambda r, h=0: ((tm, dh), (r, h))
    npt, f_specs = _pair_specs(h_fwd, blk, tm)
    _, b_specs = _pair_specs(h_bwd, blk, tm)
    return pl.pallas_call(
        functools.partial(_mlstm_out_body, npt=npt),
        grid=(T // tm, B_HEADS),
        in_specs=f_specs + b_specs + [pl.BlockSpec((tm, dh), lambda i, h: (i, cb0 + h)),
                                      pl.BlockSpec((1, dh), lambda i, h: (0, h))],
        out_specs=pl.BlockSpec((tm, dh), lambda i, h: (i, h)),
        out_shape=jax.ShapeDtypeStruct((T, D_B), BF16),
        compiler_params=_cparams(2),
        name="mlstm_out",
    )(*h_fwd, *h_bwd, proj, norm_g.reshape(1, D_B))


def _split3(x):
    hi = x.astype(BF16)
    r1 = x - hi.astype(F32)
    mid = r1.astype(BF16)
    return hi, mid, (r1 - mid.astype(F32)).astype(BF16)


def _gla_dir(d, q_ref, k_ref, v_ref, lr_ref, gw_ref, gb_ref, o_ref, ST_s, *, Tc, q_scale, dk, dv):
    H = C_HEADS
    last = Tc - 1 if d == 0 else 0
    lr_hi, lr_mid, _ = _split3(lr_ref[...])
    gw_hi, gw_mid, _ = _split3(gw_ref[d])
    z = (jnp.dot(lr_hi, gw_hi, preferred_element_type=F32)
         + jnp.dot(lr_hi, gw_mid, preferred_element_type=F32)
         + jnp.dot(lr_mid, gw_hi, preferred_element_type=F32)) + gb_ref[d]
    loga = _log_sigmoid(z) * (1.0 / GLA_TAU)
    tri = _tri_mask(Tc, d)
    trib = tri.astype(F32).astype(BF16)
    b = sum(jnp.dot(trib, piece, preferred_element_type=F32) for piece in _split3(loga))
    qs = (q_ref[...] * q_scale * jnp.exp(b)).astype(BF16)
    kf = k_ref[...]
    ke = (kf * jnp.exp(-b)).astype(BF16)
    bL = b[last:last + 1, :]
    kd = (kf * jnp.exp(bL - b)).astype(BF16)
    ebL = jnp.exp(bL)
    for hd in range(H):
        ks = slice(hd * dk, (hd + 1) * dk)
        vs = slice(hd * dv, (hd + 1) * dv)
        vb = v_ref[:, vs].astype(BF16)
        att = jnp.where(tri, lax.dot_general(qs[:, ks], ke[:, ks], NT_DIMS, preferred_element_type=F32), 0.0)
        ST = ST_s[d, hd]
        inter = lax.dot_general(qs[:, ks], ST.astype(BF16), NT_DIMS, preferred_element_type=F32)
        o_ref[:, vs] = inter + jnp.dot(att.astype(BF16), vb, preferred_element_type=F32)
        ST_s[d, hd] = ST * ebL[:, ks] + lax.dot_general(vb, kd[:, ks], TN_DIMS, preferred_element_type=F32)


def _gla_body(qf_ref, kf_ref, vf_ref, lrf_ref, qb_ref, kb_ref, vb_ref, lrb_ref, gw_ref, gb_ref, s0_ref,
              of_ref, ob_ref, so_ref, ST_s, *, nc, **kw):
    c = pl.program_id(1)
    states = [(d, hd) for d in range(2) for hd in range(C_HEADS)]

    @pl.when(c == 0)
    def _():
        for d, hd in states:
            ST_s[d, hd] = s0_ref[d, hd].T

    _gla_dir(0, qf_ref, kf_ref, vf_ref, lrf_ref, gw_ref, gb_ref, of_ref, ST_s, **kw)
    _gla_dir(1, qb_ref, kb_ref, vb_ref, lrb_ref, gw_ref, gb_ref, ob_ref, ST_s, **kw)

    @pl.when(c == nc - 1)
    def _():
        for d, hd in states:
            so_ref[d, hd] = ST_s[d, hd].T


def _gla(proj, lr, gw_pad, gate_b, row0, nseq, L, s0):
    H = C_HEADS
    dk, dv = s0.shape[-2:]
    DK, DV = H * dk, H * dv
    Tc = min(SCAN_CHUNK, L)
    nc = L // Tc
    rb0 = row0 // Tc
    fwd = lambda s, c: s * nc + c
    bwd = lambda s, c: s * nc + nc - 1 - c

    def chunk_specs(rb):
        return [pl.BlockSpec((Tc, DK), lambda s, c: (rb0 + rb(s, c), 0)),
                pl.BlockSpec((Tc, DK), lambda s, c: (rb0 + rb(s, c), 1)),
                pl.BlockSpec((Tc, DV), lambda s, c: (rb0 + rb(s, c), (2 * DK) // DV)),
                pl.BlockSpec((Tc, LANE), lambda s, c: (rb0 + rb(s, c), 0))]

    st_map = lambda s, c: (s, 0, 0, 0, 0)
    of, ob, S = pl.pallas_call(
        functools.partial(_gla_body, Tc=Tc, nc=nc, q_scale=dk ** -0.5, dk=dk, dv=dv),
        grid=(nseq, nc),
        in_specs=chunk_specs(fwd) + chunk_specs(bwd) + [
            pl.BlockSpec((2, LANE, DK), lambda s, c: (0, 0, 0)),
            pl.BlockSpec((2, 1, DK), lambda s, c: (0, 0, 0)),
            pl.BlockSpec((None, 2, H, dk, dv), st_map)],
        out_specs=[pl.BlockSpec((Tc, DV), lambda s, c: (fwd(s, c), 0)),
                   pl.BlockSpec((Tc, DV), lambda s, c: (bwd(s, c), 0)),
                   pl.BlockSpec((None, 2, H, dk, dv), st_map)],
        out_shape=[jax.ShapeDtypeStruct((nseq * L, DV), F32),
                   jax.ShapeDtypeStruct((nseq * L, DV), F32),
                   jax.ShapeDtypeStruct((nseq, 2, H, dk, dv), F32)],
        scratch_shapes=[pltpu.VMEM((2, H, dv, dk), F32)],
        compiler_params=_cparams(2),
        name="gla",
    )(*([proj] * 3 + [lr]) * 2, gw_pad, gate_b.reshape(2, 1, DK), s0)
    return (of, ob), S


def _gla_out_body(ofp_ref, ofs_ref, obp_ref, obs_ref, g_ref, ng_ref, y_ref, *, npt):
    oo = _pick(ofp_ref, ofs_ref, npt) + _pick(obp_ref, obs_ref, npt)
    ms = jnp.mean(oo * oo, axis=1, keepdims=True)
    y_ref[...] = (oo * lax.rsqrt(ms + LN_EPS) * ng_ref[...] * _silu(g_ref[...])).astype(y_ref.dtype)


def _gla_out(o_fwd, o_bwd, proj, col0, norm_g, tm):
    T = proj.shape[0]
    DV = o_fwd[0].shape[-1]
    dv = DV // C_HEADS
    cb0 = col0 // dv
    blk = lambda r, h=0: ((tm, dv), (r, h))
    npt, f_specs = _pair_specs(o_fwd, blk, tm)
    _, b_specs = _pair_specs(o_bwd, blk, tm)
    return pl.pallas_call(
        functools.partial(_gla_out_body, npt=npt),
        grid=(T // tm, C_HEADS),
        in_specs=f_specs + b_specs + [pl.BlockSpec((tm, dv), lambda i, h: (i, cb0 + h)),
                                      pl.BlockSpec((1, dv), lambda i, h: (0, h))],
        out_specs=pl.BlockSpec((tm, dv), lambda i, h: (i, h)),
        out_shape=jax.ShapeDtypeStruct((T, DV), BF16),
        compiler_params=_cparams(2),
        name="gla_out",
    )(*o_fwd, *o_bwd, proj, norm_g.reshape(1, DV))


def _layer_norm(z, g, b):
    mu = jnp.mean(z, axis=1, keepdims=True)
    zc = z - mu
    var = jnp.mean(zc * zc, axis=1, keepdims=True)
    return zc * lax.rsqrt(var + LN_EPS) * g + b


def _outproj_body(*refs, npts, g_row, alpha):
    n = len(npts)
    n_y = sum(1 if p is None else 2 for p in npts)
    y_refs = list(refs[:n_y])
    ws = refs[n_y:n_y + n]
    x_ref, m_ref, lg_ref, lb_ref, rw_ref, rb_ref, o_ref, h_ref, ids_ref, gts_ref = refs[n_y + n:]
    acc = None
    for npt, w in zip(npts, ws):
        y = y_refs.pop(0)[...] if npt is None else _pick(y_refs.pop(0), y_refs.pop(0), npt)
        part = jnp.dot(y, w[...], preferred_element_type=F32)
        acc = part if acc is None else acc + part
    z = alpha * x_ref[...] + m_ref[g_row:g_row + 1, :] * acc
    x_new = _layer_norm(z, lg_ref[...], lb_ref[...])
    o_ref[...] = x_new
    _route(x_new, m_ref, rw_ref, rb_ref, h_ref, ids_ref, gts_ref, sh_row=g_row + 1, sc_row=g_row + 2)


def _outproj_ln(ys, ws, x, modt, ln_g, ln_b, router_w, router_b, *, g_row, alpha, tm):
    T, D = x.shape
    E = router_w.shape[1]
    mod4, mod_map = modt[0], modt[1](tm)
    y_specs, y_args, npts = [], [], []
    for y in ys:
        if isinstance(y, tuple):
            K = y[0].shape[1]
            npt, specs = _pair_specs(y, lambda r, K=K: ((tm, K), (r, 0)), tm)
            y_specs += specs
            y_args += list(y)
            npts.append(npt)
        else:
            y_specs.append(pl.BlockSpec((tm, y.shape[1]), lambda i: (i, 0)))
            y_args.append(y)
            npts.append(None)
    in_specs = (y_specs
                + [pl.BlockSpec(w.shape, lambda i: (0, 0)) for w in ws]
                + [pl.BlockSpec((tm, D), lambda i: (i, 0)),
                   pl.BlockSpec((None, None, 6, D), lambda i: mod_map(i)),
                   pl.BlockSpec((1, D), lambda i: (0, 0)),
                   pl.BlockSpec((1, D), lambda i: (0, 0)),
                   pl.BlockSpec((E, D), lambda i: (0, 0)),
                   pl.BlockSpec((E, 1), lambda i: (0, 0))])
    return pl.pallas_call(
        functools.partial(_outproj_body, npts=tuple(npts), g_row=g_row, alpha=alpha),
        grid=(T // tm,),
        in_specs=in_specs,
        out_specs=[pl.BlockSpec((tm, D), lambda i: (i, 0)),
                   pl.BlockSpec((tm * TOK_ROWS, LANE), lambda i: (i, 0)),
                   pl.BlockSpec((SUBLANE, tm), lambda i: (0, i)),
                   pl.BlockSpec((SUBLANE, tm), lambda i: (0, i))],
        out_shape=[jax.ShapeDtypeStruct((T, D), F32),
                   jax.ShapeDtypeStruct((T * TOK_ROWS, LANE), F32),
                   jax.ShapeDtypeStruct((SUBLANE, T), jnp.int32),
                   jax.ShapeDtypeStruct((SUBLANE, T), F32)],
        compiler_params=_cparams(1),
        name="outproj_ln",
    )(*y_args, *ws, x, mod4, ln_g.reshape(1, D), ln_b.reshape(1, D), router_w.T, router_b.reshape(E, 1))


def _route(x, m_ref, w_ref, b_ref, h_ref, ids_ref, gts_ref, *, sh_row, sc_row):
    h = x * (1.0 + m_ref[sc_row:sc_row + 1, :]) + m_ref[sh_row:sh_row + 1, :]
    _to_tok_blocks(h_ref, h)
    logits = lax.dot_general(w_ref[...], h, NT_DIMS, precision=HI, preferred_element_type=F32) + b_ref[...]
    E, tm = logits.shape
    row = lax.broadcasted_iota(jnp.int32, (E, tm), 0)
    mx = jnp.max(logits, axis=0, keepdims=True)
    ex = jnp.exp(logits - mx)
    p = ex / jnp.sum(ex, axis=0, keepdims=True)
    grp = row // (E // N_GROUPS)
    best = None
    for g in range(N_GROUPS):
        pg = jnp.where(grp == g, p, -1.0)
        v1 = jnp.max(pg, axis=0, keepdims=True)
        i1 = jnp.min(jnp.where(pg == v1, row, E), axis=0, keepdims=True)
        pg2 = jnp.where(row == i1, -1.0, pg)
        v2 = jnp.max(pg2, axis=0, keepdims=True)
        i2 = jnp.min(jnp.where(pg2 == v2, row, E), axis=0, keepdims=True)
        score = v1 + v2
        if best is None:
            best = (score, v1, i1, v2, i2)
        else:
            take = score > best[0]
            best = tuple(jnp.where(take, n, o) for n, o in zip((score, v1, i1, v2, i2), best))
    _, v1, i1, v2, i2 = best
    tot = v1 + v2
    slot = lax.broadcasted_iota(jnp.int32, (SUBLANE, tm), 0)
    ids_ref[...] = jnp.where(slot == 0, i1, jnp.where(slot == 1, i2, 0))
    gts_ref[...] = jnp.where(slot == 0, v1 / tot, jnp.where(slot == 1, v2 / tot, 0.0))


def _to_tok_blocks(ref, x):
    tm = x.shape[0]
    for s in range(TOK_ROWS):
        ref[pl.ds(s, tm, stride=TOK_ROWS), :] = x[:, s * LANE:(s + 1) * LANE]


def _tok_chunk(ref, s, tm):
    return ref[pl.ds(s, tm, stride=TOK_ROWS), :]


class _RowGather:
    def __init__(self, src_hbm, idx_hbm, idx_s, bufs, isem, gsem, tm):
        self.src, self.idx_hbm, self.idx_s, self.bufs = src_hbm, idx_hbm, idx_s, bufs
        self.isem, self.gsem, self.tm = isem, gsem, tm

    def _idx_copy(self, tile, slot):
        return pltpu.make_async_copy(self.idx_hbm.at[tile], self.idx_s.at[slot], self.isem.at[slot])

    def _row_copy(self, slot, a, r):
        row = pl.multiple_of(self.idx_s[slot, a, r], TOK_ROWS)
        dst = pl.multiple_of(r * TOK_ROWS, TOK_ROWS)
        return pltpu.make_async_copy(self.src.at[pl.ds(row, TOK_ROWS), :],
                                     self.bufs[a].at[slot, pl.ds(dst, TOK_ROWS), :], self.gsem.at[slot, a])

    def rows_loop(self, slot, start):
        def body(r, carry):
            for a in range(len(self.bufs)):
                cp = self._row_copy(slot, a, r)
                cp.start() if start else cp.wait()
            return carry
        lax.fori_loop(0, self.tm, body, 0, unroll=8)

    def rows_start_inline(self, slot):
        for r in range(self.tm):
            for a in range(len(self.bufs)):
                self._row_copy(slot, a, r).start(priority=(r + a) % 2)

    def prologue(self, n_tiles):
        self._idx_copy(0, 0).start()
        self._idx_copy(0, 0).wait()
        self.rows_loop(0, True)
        if n_tiles > 1:
            self._idx_copy(1, 1).start()

    def advance_indices(self, i, n_tiles):
        slot = i % 2

        @pl.when(i + 1 < n_tiles)
        def _():
            self._idx_copy(i + 1, 1 - slot).wait()

        @pl.when(i + 2 < n_tiles)
        def _():
            self._idx_copy(i + 2, slot).start()


def _ffn_body(te_ref, va_ref, idx_hbm, h_hbm, wi_ref, wo_ref, o_ref, xbuf, x16, a16, idx_s, isem, gsem,
              *, F, tm, n_tiles):
    i = pl.program_id(0)
    slot = i % 2
    valid = va_ref[i] != 0
    first_invalid = jnp.logical_and(jnp.logical_not(valid), va_ref[jnp.maximum(i - 1, 0)] != 0)
    g = _RowGather(h_hbm, idx_hbm, idx_s, [xbuf], isem, gsem, tm)

    @pl.when(i == 0)
    def _():
        g.prologue(n_tiles)

    @pl.when(valid)
    def _():
        g.rows_loop(slot, False)
        g.advance_indices(i, n_tiles)

    @pl.when(valid)
    def _():
        g.rows_start_inline(1 - slot)
        xb = xbuf.at[slot]
        for s in range(TOK_ROWS):
            x16[:, s * LANE:(s + 1) * LANE] = _tok_chunk(xb, s, tm).astype(BF16)
        x = x16[...]
        fc = F // FFN_PARTS
        for p in range(FFN_PARTS):
            u = jnp.dot(x, wi_ref[:, p * fc:(p + 1) * fc], preferred_element_type=F32)
            w = jnp.dot(x, wi_ref[:, F + p * fc:F + (p + 1) * fc], preferred_element_type=F32)
            a16[:, p * fc:(p + 1) * fc] = (_silu(u) * w).astype(BF16)
        _to_tok_blocks(o_ref, jnp.dot(a16[...], wo_ref[...], preferred_element_type=F32))

    @pl.when(jnp.logical_not(valid))
    def _():
        o_ref[...] = jnp.zeros_like(o_ref)

    @pl.when(first_invalid)
    def _():
        g.rows_loop(slot, False)

        @pl.when(i + 1 < n_tiles)
        def _():
            g._idx_copy(i + 1, 1 - slot).wait()


def _expert_ffn(h, src_tok, w_in, w_out, layer, tile_e, tile_valid):
    n_tiles, _, tm = src_tok.shape
    F, D = w_out.shape[2:]
    return pl.pallas_call(
        functools.partial(_ffn_body, F=F, tm=tm, n_tiles=n_tiles),
        grid_spec=pltpu.PrefetchScalarGridSpec(
            num_scalar_prefetch=2,
            grid=(n_tiles,),
            in_specs=[pl.BlockSpec(memory_space=pl.ANY),
                      pl.BlockSpec(memory_space=pl.ANY),
                      pl.BlockSpec((None, None, D, 2 * F), lambda i, te, va: (layer, te[i], 0, 0)),
                      pl.BlockSpec((None, None, F, D), lambda i, te, va: (layer, te[i], 0, 0))],
            out_specs=pl.BlockSpec((tm * TOK_ROWS, LANE), lambda i, te, va: (i, 0)),
            scratch_shapes=[pltpu.VMEM((2, tm * TOK_ROWS, LANE), F32), pltpu.VMEM((tm, D), BF16),
                            pltpu.VMEM((tm, F), BF16), pltpu.SMEM((2, 1, tm), jnp.int32),
                            pltpu.SemaphoreType.DMA((2,)), pltpu.SemaphoreType.DMA((2, 1))]),
        out_shape=jax.ShapeDtypeStruct((n_tiles * tm * TOK_ROWS, LANE), F32),
        compiler_params=_cparams(1),
        name="expert_ffn",
    )(tile_e, tile_valid, src_tok, h, w_in, w_out)


def _combine_body(x_ref, idx_hbm, y_hbm, gt_ref, m_ref, lg_ref, lb_ref, o_ref, y0buf, y1buf, idx_s, isem, gsem,
                  *, g_row, alpha, tm, n_tiles):
    i = pl.program_id(0)
    slot = i % 2
    g = _RowGather(y_hbm, idx_hbm, idx_s, [y0buf, y1buf], isem, gsem, tm)

    @pl.when(i == 0)
    def _():
        g.prologue(n_tiles)

    g.rows_loop(slot, False)
    g.advance_indices(i, n_tiles)
    if n_tiles > 1:
        g.rows_start_inline(1 - slot)
    gt = gt_ref[...]
    g0, g1 = gt[:, 0:1], gt[:, 1:2]
    y0b, y1b = y0buf.at[slot], y1buf.at[slot]
    moe = jnp.concatenate([g0 * _tok_chunk(y0b, s, tm) + g1 * _tok_chunk(y1b, s, tm) for s in range(TOK_ROWS)],
                          axis=1)
    z = alpha * x_ref[...] + m_ref[g_row:g_row + 1, :] * moe
    o_ref[...] = _layer_norm(z, lg_ref[...], lb_ref[...])

    if n_tiles > 1:
        @pl.when(i == n_tiles - 1)
        def _():
            g.rows_loop(1 - slot, False)


def _combine_ln(x, ys, dest, gts, modt, ln_g, ln_b, *, g_row, alpha):
    T, D = x.shape
    n_tiles, _, tm = dest.shape
    mod4, mod_map = modt[0], modt[1](tm)
    row = pl.BlockSpec((tm, D), lambda i: (i, 0))
    vec = pl.BlockSpec((1, D), lambda i: (0, 0))
    hbm = pl.BlockSpec(memory_space=pl.ANY)
    return pl.pallas_call(
        functools.partial(_combine_body, g_row=g_row, alpha=alpha, tm=tm, n_tiles=n_tiles),
        grid=(n_tiles,),
        in_specs=[row, hbm, hbm, pl.BlockSpec((tm, LANE), lambda i: (i, 0)),
                  pl.BlockSpec((None, None, 6, D), lambda i: mod_map(i)), vec, vec],
        out_specs=row,
        out_shape=jax.ShapeDtypeStruct((T, D), F32),
        scratch_shapes=[pltpu.VMEM((2, tm * TOK_ROWS, LANE), F32), pltpu.VMEM((2, tm * TOK_ROWS, LANE), F32),
                        pltpu.SMEM((2, 2, tm), jnp.int32),
                        pltpu.SemaphoreType.DMA((2,)), pltpu.SemaphoreType.DMA((2, 2))],
        compiler_params=_cparams(1),
        name="combine_ln",
    )(x, dest, ys, gts, mod4, ln_g.reshape(1, D), ln_b.reshape(1, D))


def _route_meta(ids, tm, tm_tok):
    T = ids.shape[0]
    E = N_EXPERTS
    e_flat = ids.reshape(-1)
    onehot = (e_flat[:, None] == jnp.arange(E, dtype=jnp.int32)[None, :]).astype(jnp.int32)
    csum = jnp.cumsum(onehot, axis=0)
    rank = jnp.sum(csum * onehot, axis=1) - 1
    counts = csum[-1]
    padded = ((counts + tm - 1) // tm) * tm
    ends = jnp.cumsum(padded)
    starts = ends - padded
    dest = jnp.sum(starts[None, :] * onehot, axis=1) + rank
    n_rows = 2 * T + E * tm
    src_tok = jnp.zeros((n_rows,), jnp.int32).at[dest].set(jnp.arange(2 * T, dtype=jnp.int32) // 2)
    tile_start = jnp.arange(n_rows // tm, dtype=jnp.int32) * tm
    valid = (tile_start < ends[-1]).astype(jnp.int32)
    probe = jnp.minimum(tile_start, ends[-1] - 1)
    tile_e = jnp.sum((ends[None, :] <= probe[:, None]).astype(jnp.int32), axis=1)
    dest_t = dest.reshape(T // tm_tok, tm_tok, 2).transpose(0, 2, 1)
    return (dest_t * TOK_ROWS, src_tok.reshape(n_rows // tm, 1, tm) * TOK_ROWS, jnp.minimum(tile_e, E - 1), valid)


def _pos_embed_2d(n_tokens, dim):
    rows = n_tokens // GRID_W
    quarter = dim // 4
    freqs = jnp.exp(-math.log(10000.0) * jnp.arange(quarter, dtype=F32) / quarter)
    r = jnp.broadcast_to(jnp.arange(rows, dtype=F32)[:, None], (rows, GRID_W)).reshape(-1)
    col = jnp.broadcast_to(jnp.arange(GRID_W, dtype=F32)[None, :], (rows, GRID_W)).reshape(-1)
    ar = r[:, None] * freqs
    ac = col[:, None] * freqs
    return jnp.concatenate([jnp.sin(ar), jnp.cos(ar), jnp.sin(ac), jnp.cos(ac)], -1)


def _pad_cols(w, n):
    return jnp.pad(w, ((0, 0),) * (w.ndim - 1) + ((0, n - w.shape[-1]),))


def kernel(x_prompt, x_sample, state_lru, state_mlstm_c, state_mlstm_n, state_mlstm_m, state_gla, c, c_ctx, mod_w, mod_b, ln_g, ln_b, even_w_in, even_w_out, lru_conv_w, lru_conv_b, lru_gate_w, lru_gate_b, lru_lambda, mlstm_gate_b, mlstm_norm_g, odd_w_in, odd_w_out, gla_gate_w, gla_gate_b, gla_norm_g, router_w, router_b, moe_w_in, moe_w_out):
    Bp, Lp, D = x_prompt.shape
    Bs, Ls, _ = x_sample.shape
    depth = mod_w.shape[0]
    Tp, Ts = Bp * Lp, Bs * Ls
    T = Tp + Ts
    alpha = (2 * depth) ** 0.25
    D_A = lru_conv_w.shape[-1]
    D_B = mlstm_norm_g.shape[-1]
    DK = gla_gate_w.shape[-1]
    DV = gla_norm_g.shape[-1]
    tm = next(t for t in (512, 256, 128) if Tp % t == 0 and Ls % t == 0)
    tn = 1024

    n_cond = 1 + Bs
    R = -(-n_cond // SUBLANE) * SUBLANE
    cond = jnp.zeros((R, D), F32).at[0].set(c_ctx).at[1:n_cond].set(c)
    mod_all = _modulation(cond, mod_w, mod_b)
    mod4 = mod_all.reshape(depth, R, 6, D)
    tm_proj = next(t for t in (1024, 512, 256, 128) if Tp % t == 0 and Ls % t == 0)
    assert D == TOK_ROWS * LANE

    x = _embed(x_prompt.reshape(Tp, D), x_sample.reshape(Ts, D), _pos_embed_2d(Ls, D), tm)

    moe_w_in_b = moe_w_in.astype(BF16)
    moe_w_out_b = moe_w_out.astype(BF16)
    groups = ((0, Bp, Lp), (Tp, Bs, Ls))

    s_lru, s_c, s_n, s_m, s_gla = [], [], [], [], []
    for l in range(depth):
        j = l // 2
        mt = (mod4, lambda t, l=l: (lambda i: (l, jnp.where(i < Tp // t, 0, 1 + (i - Tp // t) // (Ls // t)), 0, 0)))
        if l % 2 == 0:
            w_in = even_w_in[j]
            n_main = 2 * D_A + 4 * D_B
            proj = _proj(x, mt, w_in[:, :n_main].astype(BF16), sh_row=0, sc_row=1, tm=tm_proj, tn=tn)
            gates = _proj(x, mt, _pad_cols(w_in[:, n_main:], LANE), sh_row=0, sc_row=1, tm=tm, tn=LANE, hp=True)
            gate_bias = _pad_cols(mlstm_gate_b[j].reshape(1, -1), LANE)
            ya, hs = [], []
            for gi, (row0, nseq, L) in enumerate(groups):
                if gi == 0:
                    lru0 = jnp.zeros((nseq, 2, D_A), F32)
                    c0 = jnp.zeros((nseq, 2, B_HEADS, D_B // B_HEADS, D_B // B_HEADS), F32)
                    n0 = jnp.zeros((nseq, 2, B_HEADS, D_B // B_HEADS), F32)
                    m0 = jnp.zeros((nseq, 2, B_HEADS), F32)
                else:
                    lru0, c0, n0, m0 = state_lru[:, j], state_mlstm_c[:, j], state_mlstm_n[:, j], state_mlstm_m[:, j]
                y_a, st = _lru(proj, row0, nseq, L, lru_conv_w[j], lru_conv_b[j], lru_gate_w[j], lru_gate_b[j],
                               lru_lambda[j], lru0)
                h, C, n, m = _mlstm(proj, gates, gate_bias, row0, nseq, L, 2 * D_A, c0, n0, m0)
                ya.append(y_a)
                hs.append(h)
                if gi == 0:
                    s_lru.append(st)
                    s_c.append(C)
                    s_n.append(n.reshape(nseq, 2, B_HEADS, -1))
                    s_m.append(m.reshape(nseq, 2, B_HEADS))
            h_fwd, h_bwd = zip(*hs)
            y_b = _mlstm_out(h_fwd, h_bwd, proj, 2 * D_A + 3 * D_B, mlstm_norm_g[j], tm)
            w_out = even_w_out[j].astype(BF16)
            x, hf, ids, gts = _outproj_ln([tuple(ya), y_b], [w_out[:D_A], w_out[D_A:]], x, mt, ln_g[l, 0],
                                          ln_b[l, 0], router_w, router_b, g_row=2, alpha=alpha, tm=tm)
        else:
            w_in = odd_w_in[j]
            n_main = 2 * DK + 2 * DV
            proj = _proj(x, mt, w_in[:, :n_main].astype(BF16), sh_row=0, sc_row=1, tm=tm_proj, tn=tn)
            lr = _proj(x, mt, _pad_cols(w_in[:, n_main:], LANE), sh_row=0, sc_row=1, tm=tm, tn=LANE, hp=True)
            gw_pad = jnp.zeros((2, LANE, DK), F32)
            for dd in range(2):
                gw_pad = gw_pad.at[dd, dd * GLA_RANK:(dd + 1) * GLA_RANK].set(gla_gate_w[j, dd])
            os_ = []
            for gi, (row0, nseq, L) in enumerate(groups):
                s0 = (jnp.zeros((nseq, 2, C_HEADS, DK // C_HEADS, DV // C_HEADS), F32) if gi == 0
                      else state_gla[:, j])
                o, S = _gla(proj, lr, gw_pad, gla_gate_b[j], row0, nseq, L, s0)
                os_.append(o)
                if gi == 0:
                    s_gla.append(S)
            o_fwd, o_bwd = zip(*os_)
            y = _gla_out(o_fwd, o_bwd, proj, 2 * DK + DV, gla_norm_g[j], tm)
            x, hf, ids, gts = _outproj_ln([y], [odd_w_out[j].astype(BF16)], x, mt, ln_g[l, 0], ln_b[l, 0],
                                          router_w, router_b, g_row=2, alpha=alpha, tm=tm)

        gts = _pad_cols(gts[:2].T, LANE)
        dest, src_tok, tile_e, tile_valid = _route_meta(ids[:2].T, MOE_TM, tm)
        ys = _expert_ffn(hf, src_tok, moe_w_in_b, moe_w_out_b, l, tile_e, tile_valid)
        x = _combine_ln(x, ys, dest, gts, mt, ln_g[l, 1], ln_b[l, 1], g_row=5, alpha=alpha)

    y_prompt = x[:Tp].reshape(Bp, Lp, D)
    y_sample = x[Tp:].reshape(Bs, Ls, D)
    return (y_prompt, y_sample, jnp.stack(s_lru, 1), jnp.stack(s_c, 1), jnp.stack(s_n, 1),
            jnp.stack(s_m, 1), jnp.stack(s_gla, 1))
```

```python
import functools
import math

import jax
import jax.numpy as jnp
from jax import lax
from jax.experimental import pallas as pl
from jax.experimental.pallas import tpu as pltpu

F32 = jnp.float32
BF16 = jnp.bfloat16
HI = lax.Precision.HIGHEST

LN_EPS = 1e-5
LRU_C = 8.0
A_BLOCKS = 8
B_HEADS = 4
C_HEADS = 4
GLA_RANK = 16
GLA_TAU = 16.0
N_EXPERTS = 16
N_GROUPS = 4
GRID_W = 64
LANE = 128
SUBLANE = 8
SCAN_CHUNK = 256
MOE_TM = 512
FFN_PARTS = 4
FFN_INLINE_START_SLOTS = (0,)
TOK_ROWS = 16
VMEM_LIMIT = 56 * 1024 * 1024

NT_DIMS = (((1,), (1,)), ((), ()))
TN_DIMS = (((0,), (0,)), ((), ()))


def _cparams(n_axes):
    return pltpu.CompilerParams(dimension_semantics=("arbitrary",) * n_axes,
                                vmem_limit_bytes=VMEM_LIMIT)


def _sigmoid(x):
    return 1.0 / (1.0 + jnp.exp(-x))


def _silu(x):
    return x * _sigmoid(x)


def _log_sigmoid(x):
    return jnp.minimum(x, 0.0) - jnp.log1p(jnp.exp(-jnp.abs(x)))


def _gelu_tanh(x):
    return 0.5 * x * (1.0 + jnp.tanh(math.sqrt(2.0 / math.pi) * (x + 0.044715 * (x * x * x))))


def _mod_body(c_ref, w_ref, b_ref, o_ref):
    c = c_ref[...]
    o_ref[...] = jnp.dot(_silu(c), w_ref[...], precision=HI, preferred_element_type=F32) + b_ref[...]


def _modulation(cond, mod_w, mod_b):
    R, D = cond.shape
    nl, _, N = mod_w.shape
    tn = 1024
    return pl.pallas_call(
        _mod_body,
        grid=(nl, N // tn),
        in_specs=[pl.BlockSpec((R, D), lambda l, j: (0, 0)),
                  pl.BlockSpec((None, D, tn), lambda l, j: (l, 0, j)),
                  pl.BlockSpec((None, 1, tn), lambda l, j: (l, 0, j))],
        out_specs=pl.BlockSpec((None, R, tn), lambda l, j: (l, 0, j)),
        out_shape=jax.ShapeDtypeStruct((nl, R, N), F32),
        compiler_params=_cparams(2),
        name="modulation",
    )(cond, mod_w, mod_b.reshape(nl, 1, N))


def _embed_body(xp_ref, xs_ref, pos_ref, o_ref, *, n_prompt_tiles):
    i = pl.program_id(0)

    @pl.when(i < n_prompt_tiles)
    def _():
        o_ref[...] = xp_ref[...]

    @pl.when(i >= n_prompt_tiles)
    def _():
        o_ref[...] = xs_ref[...] + pos_ref[...]


def _embed(xp, xs, pos, tm):
    Tp, D = xp.shape
    Ts = xs.shape[0]
    Ls = pos.shape[0]
    npt, nst, npos = Tp // tm, Ts // tm, Ls // tm
    return pl.pallas_call(
        functools.partial(_embed_body, n_prompt_tiles=npt),
        grid=(npt + nst,),
        in_specs=[pl.BlockSpec((tm, D), lambda i: (jnp.minimum(i, npt - 1), 0)),
                  pl.BlockSpec((tm, D), lambda i: (jnp.maximum(i - npt, 0), 0)),
                  pl.BlockSpec((tm, D), lambda i: (jnp.maximum(i - npt, 0) % npos, 0))],
        out_specs=pl.BlockSpec((tm, D), lambda i: (i, 0)),
        out_shape=jax.ShapeDtypeStruct((Tp + Ts, D), F32),
        compiler_params=_cparams(1),
        name="embed",
    )(xp, xs, pos)


def _split3(x):
    hi = x.astype(BF16)
    r1 = x - hi.astype(F32)
    mid = r1.astype(BF16)
    return hi, mid, (r1 - mid.astype(F32)).astype(BF16)


def _proj_body(x_ref, m_ref, w_ref, wg_ref, o_ref, og_ref, h_ref, *, sh_row, sc_row):
    @pl.when(pl.program_id(1) == 0)
    def _():
        h = x_ref[...] * (1.0 + m_ref[sc_row:sc_row + 1, :]) + m_ref[sh_row:sh_row + 1, :]
        h_hi, h_mid, _ = _split3(h)
        h_ref[...] = h_hi
        wg_hi, wg_mid, _ = _split3(wg_ref[...])
        og_ref[...] = (jnp.dot(h_hi, wg_hi, preferred_element_type=F32)
                       + jnp.dot(h_hi, wg_mid, preferred_element_type=F32)
                       + jnp.dot(h_mid, wg_hi, preferred_element_type=F32))

    o_ref[...] = jnp.dot(h_ref[...], w_ref[...], preferred_element_type=F32)


def _proj(x, modt, w, w_gate, *, sh_row, sc_row, tm, tn):
    T, D = x.shape
    N = w.shape[1]
    mod4, mod_map = modt[0], modt[1](tm)
    return pl.pallas_call(
        functools.partial(_proj_body, sh_row=sh_row, sc_row=sc_row),
        grid=(T // tm, N // tn),
        in_specs=[pl.BlockSpec((tm, D), lambda i, j: (i, 0)),
                  pl.BlockSpec((None, None, 6, D), lambda i, j: mod_map(i)),
                  pl.BlockSpec((D, tn), lambda i, j: (0, j)),
                  pl.BlockSpec((D, LANE), lambda i, j: (0, 0))],
        out_specs=[pl.BlockSpec((tm, tn), lambda i, j: (i, j)),
                   pl.BlockSpec((tm, LANE), lambda i, j: (i, 0))],
        out_shape=[jax.ShapeDtypeStruct((T, N), F32), jax.ShapeDtypeStruct((T, LANE), F32)],
        scratch_shapes=[pltpu.VMEM((tm, D), BF16)],
        compiler_params=_cparams(2),
        name="proj",
    )(x, mod4, w, w_gate)


def _lru_body(ag_ref, ax_ref, cw_ref, cb_ref, gw_ref, gb_ref, lam_ref, h0_ref, y_ref, st_ref,
              af_s, uf_s, ab_s, ub_s, hf_s, hb_s, *, L):
    x = ax_ref[...]
    row = lax.broadcasted_iota(jnp.int32, (L, LANE), 0)
    xm1 = jnp.where(row >= 1, pltpu.roll(x, 1, 0), 0.0)
    xp1 = jnp.where(row < L - 1, pltpu.roll(x, L - 1, 0), 0.0)
    xp2 = jnp.where(row < L - 2, pltpu.roll(x, L - 2, 0), 0.0)
    xc = cb_ref[...] + xm1 * cw_ref[0:1, :] + x * cw_ref[1:2, :] + xp1 * cw_ref[2:3, :] + xp2 * cw_ref[3:4, :]
    xcb = xc.astype(BF16)
    nj = L // SUBLANE
    sub = lax.broadcasted_iota(jnp.int32, (nj, SUBLANE, LANE), 1)

    for d, (a_s, u_s) in enumerate(((af_s, uf_s), (ab_s, ub_s))):
        g = jnp.dot(xcb, gw_ref[d], preferred_element_type=F32) + gb_ref[d]
        r = _sigmoid(g[:, :LANE])
        ig = _sigmoid(g[:, LANE:])
        nlam = -lam_ref[d]
        softplus = jnp.maximum(nlam, 0.0) + jnp.log1p(jnp.exp(-jnp.abs(nlam)))
        log_a = (-LRU_C * softplus) * r
        a = jnp.exp(log_a)
        u = jnp.sqrt(jnp.tanh(-log_a) * (1.0 + a * a)) * ig * xc
        a3 = a.reshape(nj, SUBLANE, LANE)
        u3 = u.reshape(nj, SUBLANE, LANE)
        for k in (1, 2, 4):
            if d == 0:
                sh, keep = k, sub >= k
            else:
                sh, keep = SUBLANE - k, sub < SUBLANE - k
            a_sh = pltpu.roll(a3, sh, 1)
            u_sh = pltpu.roll(u3, sh, 1)
            u3 = jnp.where(keep, a3 * u_sh + u3, u3)
            a3 = jnp.where(keep, a3 * a_sh, a3)
        a_s[...] = a3.reshape(L, LANE)
        u_s[...] = u3.reshape(L, LANE)

    def carry(j, hs):
        hf, hb = hs
        rf = pl.multiple_of(j * SUBLANE, SUBLANE)
        rb = pl.multiple_of((nj - 1 - j) * SUBLANE, SUBLANE)
        of = af_s[pl.ds(rf, SUBLANE), :] * hf + uf_s[pl.ds(rf, SUBLANE), :]
        ob = ab_s[pl.ds(rb, SUBLANE), :] * hb + ub_s[pl.ds(rb, SUBLANE), :]
        hf_s[pl.ds(rf, SUBLANE), :] = of
        hb_s[pl.ds(rb, SUBLANE), :] = ob
        return (jnp.broadcast_to(of[SUBLANE - 1:SUBLANE, :], (SUBLANE, LANE)),
                jnp.broadcast_to(ob[0:1, :], (SUBLANE, LANE)))

    h0f = jnp.broadcast_to(h0_ref[0:1, :], (SUBLANE, LANE))
    h0b = jnp.broadcast_to(h0_ref[1:2, :], (SUBLANE, LANE))
    lax.fori_loop(0, nj, carry, (h0f, h0b))
    y_ref[...] = (_gelu_tanh(ag_ref[...]) * (hf_s[...] + hb_s[...])).astype(y_ref.dtype)
    st_ref[0:1, :] = hf_s[L - 1:L, :]
    st_ref[1:2, :] = hb_s[0:1, :]


def _lru(proj, row0, nseq, L, conv_w, conv_b, gate_w, gate_b, lam, h0):
    D_A = conv_w.shape[1]
    nb = D_A // LANE
    rb0 = row0 // L
    in_specs = [pl.BlockSpec((L, LANE), lambda s, h: (rb0 + s, h)),
                pl.BlockSpec((L, LANE), lambda s, h: (rb0 + s, nb + h)),
                pl.BlockSpec((4, LANE), lambda s, h: (0, h)),
                pl.BlockSpec((1, LANE), lambda s, h: (0, h)),
                pl.BlockSpec((2, None, LANE, 2 * LANE), lambda s, h: (0, h, 0, 0)),
                pl.BlockSpec((2, None, 1, 2 * LANE), lambda s, h: (0, h, 0, 0)),
                pl.BlockSpec((2, None, 1, LANE), lambda s, h: (0, h, 0, 0)),
                pl.BlockSpec((None, 2, LANE), lambda s, h: (s, 0, h))]
    args = [proj, proj, conv_w, conv_b.reshape(1, D_A), gate_w.astype(BF16),
            gate_b.reshape(2, nb, 1, 2 * LANE), lam.reshape(2, nb, 1, LANE), h0]
    return pl.pallas_call(
        functools.partial(_lru_body, L=L),
        grid=(nseq, nb),
        in_specs=in_specs,
        out_specs=[pl.BlockSpec((L, LANE), lambda s, h: (s, h)),
                   pl.BlockSpec((None, 2, LANE), lambda s, h: (s, 0, h))],
        out_shape=[jax.ShapeDtypeStruct((nseq * L, D_A), BF16),
                   jax.ShapeDtypeStruct((nseq, 2, D_A), F32)],
        scratch_shapes=[pltpu.VMEM((L, LANE), F32)] * 6,
        compiler_params=_cparams(2),
        name="rglru",
    )(*args)


def _tri_mask(Tc, d):
    r = lax.broadcasted_iota(jnp.int32, (Tc, Tc), 0)
    c = lax.broadcasted_iota(jnp.int32, (Tc, Tc), 1)
    return (c - r) * (1 - 2 * d) <= 0


def _mlstm_dir(d, q_ref, k_ref, v_ref, g_ref, gb_ref, h_ref, C_s, n_s, m_s, *, Tc, k_scale, dh):
    H = B_HEADS
    last = Tc - 1 if d == 0 else 0
    lane = lax.broadcasted_iota(jnp.int32, (Tc, LANE), 1)
    G = g_ref[...] + gb_ref[...]
    tri = _tri_mask(Tc, d)
    Bm = jnp.dot(tri.astype(F32), _log_sigmoid(G), precision=HI, preferred_element_type=F32)
    col0 = d * (2 * H)
    X = jnp.where(lane >= col0 + H, Bm, G)
    srow = lax.broadcasted_iota(jnp.int32, (2 * H, LANE), 0)
    slane = lax.broadcasted_iota(jnp.int32, (2 * H, LANE), 1)
    sel = (slane == col0 + srow).astype(F32)
    R = lax.dot_general(sel, X, NT_DIMS, precision=HI, preferred_element_type=F32)
    for hd in range(H):
        i_col = jnp.sum(jnp.where(lane == col0 + hd, X, 0.0), axis=1, keepdims=True)
        b_col = jnp.sum(jnp.where(lane == col0 + H + hd, X, 0.0), axis=1, keepdims=True)
        i_row = R[hd:hd + 1, :]
        b_row = R[H + hd:H + hd + 1, :]
        dmat = jnp.where(tri, b_col - b_row + i_row, -jnp.inf)
        m_prev = m_s[d, hd]
        m_inter = b_col + m_prev
        m_t = jnp.maximum(m_inter, jnp.max(dmat, axis=1, keepdims=True))
        cs = slice(hd * dh, (hd + 1) * dh)
        qf = q_ref[:, cs]
        qb = qf.astype(BF16)
        kf = k_ref[:, cs] * k_scale
        kb = kf.astype(BF16)
        vb = v_ref[:, cs].astype(BF16)
        S = lax.dot_general(qb, kb, NT_DIMS, preferred_element_type=F32) * jnp.exp(dmat - m_t)
        inter_scale = jnp.exp(m_inter - m_t)
        Cm = C_s[d, hd]
        num =(jnp.dot(S.astype(BF16), vb, preferred_element_type=F32)
               + inter_scale * jnp.dot(qb, Cm.astype(BF16), preferred_element_type=F32))
        nv = n_s[d, hd]
        qn = jnp.sum(qf * nv, axis=1, keepdims=True)
        den = jnp.sum(S, axis=1, keepdims=True) + inter_scale * qn
        h_ref[:, cs] = num / jnp.maximum(jnp.abs(den), jnp.exp(-m_t))
        bL = b_col[last:last + 1, :]
        g_col = bL - b_col + i_col
        m_new = jnp.maximum(bL + m_prev, jnp.max(g_col, axis=0, keepdims=True))
        wk = jnp.exp(g_col - m_new)
        decay = jnp.exp(bL + m_prev - m_new)
        kw = kf * wk
        C_s[d, hd] = decay * Cm + lax.dot_general(kw.astype(BF16), vb, TN_DIMS, preferred_element_type=F32)
        n_s[d, hd] = decay * nv + jnp.sum(kw, axis=0, keepdims=True)
        m_s[d, hd] = m_new


def _mlstm_body(qf_ref, kf_ref, vf_ref, gf_ref, qb_ref, kb_ref, vb_ref, gbk_ref, gb_ref, c0_ref, n0_ref, m0_ref,
                hf_ref, hb_ref, co_ref, no_ref, mo_ref, C_s, n_s, m_s, *, nc, **kw):
    c = pl.program_id(1)

    @pl.when(c == 0)
    def _():
        C_s[...] = c0_ref[...]
        n_s[...] = n0_ref[...]
        m_s[...] = m0_ref[...]

    _mlstm_dir(0, qf_ref, kf_ref, vf_ref, gf_ref, gb_ref, hf_ref, C_s, n_s, m_s, **kw)
    _mlstm_dir(1, qb_ref, kb_ref, vb_ref, gbk_ref, gb_ref, hb_ref, C_s, n_s, m_s, **kw)

    @pl.when(c == nc - 1)
    def _():
        co_ref[...] = C_s[...]
        no_ref[...] = n_s[...]
        mo_ref[...] = m_s[...]


def _mlstm(proj, gates, gate_bias, row0, nseq, L, col0, c0, n0, m0):
    H = B_HEADS
    dh = c0.shape[-1]
    D_B = H * dh
    Tc = min(SCAN_CHUNK, L)
    nc = L // Tc
    rb0 = row0 // Tc
    cb0 = col0 // D_B
    fwd = lambda s, c: s * nc + c
    bwd = lambda s, c: s * nc + nc - 1 - c

    def chunk_specs(rb):
        return [pl.BlockSpec((Tc, D_B), lambda s, c, off=off: (rb0 + rb(s, c), cb0 + off)) for off in range(3)] + [
            pl.BlockSpec((Tc, LANE), lambda s, c: (rb0 + rb(s, c), 0))]

    def st_specs():
        st_map = lambda s, c: (s, 0, 0, 0, 0)
        return [pl.BlockSpec((None, 2, H, dh, dh), st_map), pl.BlockSpec((None, 2, H, 1, dh), st_map),
                pl.BlockSpec((None, 2, H, 1, 1), st_map)]

    hf, hb, C, n, m = pl.pallas_call(
        functools.partial(_mlstm_body, Tc=Tc, nc=nc, k_scale=dh ** -0.5, dh=dh),
        grid=(nseq, nc),
        in_specs=chunk_specs(fwd) + chunk_specs(bwd) + [pl.BlockSpec((1, LANE), lambda s, c: (0, 0))] + st_specs(),
        out_specs=[pl.BlockSpec((Tc, D_B), lambda s, c: (fwd(s, c), 0)),
                   pl.BlockSpec((Tc, D_B), lambda s, c: (bwd(s, c), 0))] + st_specs(),
        out_shape=[jax.ShapeDtypeStruct((nseq * L, D_B), F32),
                   jax.ShapeDtypeStruct((nseq * L, D_B), F32),
                   jax.ShapeDtypeStruct((nseq, 2, H, dh, dh), F32),
                   jax.ShapeDtypeStruct((nseq, 2, H, 1, dh), F32),
                   jax.ShapeDtypeStruct((nseq, 2, H, 1, 1), F32)],
        scratch_shapes=[pltpu.VMEM((2, H, dh, dh), F32), pltpu.VMEM((2, H, 1, dh), F32),
                        pltpu.VMEM((2, H, 1, 1), F32)],
        compiler_params=_cparams(2),
        name="mlstm",
    )(*([proj] * 3 + [gates]) * 2, gate_bias, c0, n0.reshape(nseq, 2, H, 1, dh), m0.reshape(nseq, 2, H, 1, 1))
    return (hf, hb), C, n, m


def _pair_specs(pair, block, tm):
    npt = pair[0].shape[-2] // tm
    (shape, _), mk = block(0), block
    return npt, [pl.BlockSpec(shape, lambda i, *rest: mk(jnp.minimum(i, npt - 1), *rest)[1]),
                 pl.BlockSpec(shape, lambda i, *rest: mk(jnp.maximum(i - npt, 0), *rest)[1])]


def _pick(p_ref, s_ref, npt):
    return jnp.where(pl.program_id(0) < npt, p_ref[...], s_ref[...])


def _mlstm_out_body(hfp_ref, hfs_ref, hbp_ref, hbs_ref, o_ref, g_ref, y_ref, *, npt):
    hm = _pick(hfp_ref, hfs_ref, npt) + _pick(hbp_ref, hbs_ref, npt)
    mu = jnp.mean(hm, axis=1, keepdims=True)
    xc = hm - mu
    var = jnp.mean(xc * xc, axis=1, keepdims=True)
    y_ref[...] = (_sigmoid(o_ref[...]) * (xc * lax.rsqrt(var + LN_EPS) * g_ref[...])).astype(y_ref.dtype)


def _mlstm_out(h_fwd, h_bwd, proj, col0, norm_g, tm):
    T = proj.shape[0]
    D_B = h_fwd[0].shape[-1]
    dh = D_B // B_HEADS
    cb0 = col0 // dh
    blk = lambda r, h=0: ((tm, dh), (r, h))
    npt, f_specs = _pair_specs(h_fwd, blk, tm)
    _, b_specs = _pair_specs(h_bwd, blk, tm)
    return pl.pallas_call(
        functools.partial(_mlstm_out_body, npt=npt),
        grid=(T // tm, B_HEADS),
        in_specs=f_specs + b_specs + [pl.BlockSpec((tm, dh), lambda i, h: (i, cb0 + h)),
                                      pl.BlockSpec((1, dh), lambda i, h: (0, h))],
        out_specs=pl.BlockSpec((tm, dh), lambda i, h: (i, h)),
        out_shape=jax.ShapeDtypeStruct((T, D_B), BF16),
        compiler_params=_cparams(2),
        name="mlstm_out",
    )(*h_fwd, *h_bwd, proj, norm_g.reshape(1, D_B))


def _gla_dir(d, q_ref, k_ref, v_ref, lr_ref, gw_ref, gb_ref, o_ref, ST_s, *, Tc, q_scale, dk, dv):
    H = C_HEADS
    last = Tc - 1 if d == 0 else 0
    lr_hi, lr_mid, _ = _split3(lr_ref[...])
    gw_hi, gw_mid, _ = _split3(gw_ref[d])
    z = (jnp.dot(lr_hi, gw_hi, preferred_element_type=F32)
         + jnp.dot(lr_hi, gw_mid, preferred_element_type=F32)
         + jnp.dot(lr_mid, gw_hi, preferred_element_type=F32)) + gb_ref[d]
    loga = _log_sigmoid(z) * (1.0 / GLA_TAU)
    tri = _tri_mask(Tc, d)
    trib = tri.astype(F32).astype(BF16)
    b = sum(jnp.dot(trib, piece, preferred_element_type=F32) for piece in _split3(loga))
    qs = (q_ref[...] * q_scale * jnp.exp(b)).astype(BF16)
    kf = k_ref[...]
    ke = (kf * jnp.exp(-b)).astype(BF16)
    bL = b[last:last + 1, :]
    kd = (kf * jnp.exp(bL - b)).astype(BF16)
    ebL = jnp.exp(bL)
    for hd in range(H):
        ks = slice(hd * dk, (hd + 1) * dk)
        vs = slice(hd * dv, (hd + 1) * dv)
        vb = v_ref[:, vs].astype(BF16)
        att = jnp.where(tri, lax.dot_general(qs[:, ks], ke[:, ks], NT_DIMS, preferred_element_type=F32), 0.0)
        ST = ST_s[d, hd]
        inter = lax.dot_general(qs[:, ks], ST.astype(BF16), NT_DIMS, preferred_element_type=F32)
        o_ref[:, vs] = inter + jnp.dot(att.astype(BF16), vb, preferred_element_type=F32)
        ST_s[d, hd] = ST * ebL[:, ks] + lax.dot_general(vb, kd[:, ks], TN_DIMS, preferred_element_type=F32)


def _gla_body(qf_ref, kf_ref, vf_ref, lrf_ref, qb_ref, kb_ref, vb_ref, lrb_ref, gw_ref, gb_ref, s0_ref,
              of_ref, ob_ref, so_ref, ST_s, *, nc, **kw):
    c = pl.program_id(1)
    states = [(d, hd) for d in range(2) for hd in range(C_HEADS)]

    @pl.when(c == 0)
    def _():
        for d, hd in states:
            ST_s[d, hd] = s0_ref[d, hd].T

    _gla_dir(0, qf_ref, kf_ref, vf_ref, lrf_ref, gw_ref, gb_ref, of_ref, ST_s, **kw)
    _gla_dir(1, qb_ref, kb_ref, vb_ref, lrb_ref, gw_ref, gb_ref, ob_ref, ST_s, **kw)

    @pl.when(c == nc - 1)
    def _():
        for d, hd in states:
            so_ref[d, hd] = ST_s[d, hd].T


def _gla(proj, lr, gw_pad, gate_b, row0, nseq, L, s0):
    H = C_HEADS
    dk, dv = s0.shape[-2:]
    DK, DV = H * dk, H * dv
    Tc = min(SCAN_CHUNK, L)
    nc = L // Tc
    rb0 = row0 // Tc
    fwd = lambda s, c: s * nc + c
    bwd = lambda s, c: s * nc + nc - 1 - c

    def chunk_specs(rb):
        return [pl.BlockSpec((Tc, DK), lambda s, c: (rb0 + rb(s, c), 0)),
                pl.BlockSpec((Tc, DK), lambda s, c: (rb0 + rb(s, c), 1)),
                pl.BlockSpec((Tc, DV), lambda s, c: (rb0 + rb(s, c), (2 * DK) // DV)),
                pl.BlockSpec((Tc, LANE), lambda s, c: (rb0 + rb(s, c), 0))]

    st_map = lambda s, c: (s, 0, 0, 0, 0)
    of, ob, S = pl.pallas_call(
        functools.partial(_gla_body, Tc=Tc, nc=nc, q_scale=dk ** -0.5, dk=dk, dv=dv),
        grid=(nseq, nc),
        in_specs=chunk_specs(fwd) + chunk_specs(bwd) + [
            pl.BlockSpec((2, LANE, DK), lambda s, c: (0, 0, 0)),
            pl.BlockSpec((2, 1, DK), lambda s, c: (0, 0, 0)),
            pl.BlockSpec((None, 2, H, dk, dv), st_map)],
        out_specs=[pl.BlockSpec((Tc, DV), lambda s, c: (fwd(s, c), 0)),
                   pl.BlockSpec((Tc, DV), lambda s, c: (bwd(s, c), 0)),
                   pl.BlockSpec((None, 2, H, dk, dv), st_map)],
        out_shape=[jax.ShapeDtypeStruct((nseq * L, DV), F32),
                   jax.ShapeDtypeStruct((nseq * L, DV), F32),
                   jax.ShapeDtypeStruct((nseq, 2, H, dk, dv), F32)],
        scratch_shapes=[pltpu.VMEM((2, H, dv, dk), F32)],
        compiler_params=_cparams(2),
        name="gla",
    )(*([proj] * 3 + [lr]) * 2, gw_pad, gate_b.reshape(2, 1, DK), s0)
    return (of, ob), S


def _gla_out_body(ofp_ref, ofs_ref, obp_ref, obs_ref, g_ref, ng_ref, y_ref, *, npt):
    oo = _pick(ofp_ref, ofs_ref, npt) + _pick(obp_ref, obs_ref, npt)
    ms = jnp.mean(oo * oo, axis=1, keepdims=True)
    y_ref[...] = (oo * lax.rsqrt(ms + LN_EPS) * ng_ref[...] * _silu(g_ref[...])).astype(y_ref.dtype)


def _gla_out(o_fwd, o_bwd, proj, col0, norm_g, tm):
    T = proj.shape[0]
    DV = o_fwd[0].shape[-1]
    dv = DV // C_HEADS
    cb0 = col0 // dv
    blk = lambda r, h=0: ((tm, dv), (r, h))
    npt, f_specs = _pair_specs(o_fwd, blk, tm)
    _, b_specs = _pair_specs(o_bwd, blk, tm)
    return pl.pallas_call(
        functools.partial(_gla_out_body, npt=npt),
        grid=(T // tm, C_HEADS),
        in_specs=f_specs + b_specs + [pl.BlockSpec((tm, dv), lambda i, h: (i, cb0 + h)),
                                      pl.BlockSpec((1, dv), lambda i, h: (0, h))],
        out_specs=pl.BlockSpec((tm, dv), lambda i, h: (i, h)),
        out_shape=jax.ShapeDtypeStruct((T, DV), BF16),
        compiler_params=_cparams(2),
        name="gla_out",
    )(*o_fwd, *o_bwd, proj, norm_g.reshape(1, DV))


def _layer_norm(z, g, b):
    mu = jnp.mean(z, axis=1, keepdims=True)
    zc = z - mu
    var = jnp.mean(zc * zc, axis=1, keepdims=True)
    return zc * lax.rsqrt(var + LN_EPS) * g + b


def _outproj_body(*refs, npts, g_row, alpha):
    n = len(npts)
    n_y = sum(1 if p is None else 2 for p in npts)
    y_refs = list(refs[:n_y])
    ws = refs[n_y:n_y + n]
    x_ref, m_ref, lg_ref, lb_ref, rw_ref, rb_ref, o_ref, h_ref, ids_ref, gts_ref = refs[n_y + n:]
    acc = None
    for npt, w in zip(npts, ws):
        y = y_refs.pop(0)[...] if npt is None else _pick(y_refs.pop(0), y_refs.pop(0), npt)
        part = jnp.dot(y, w[...], preferred_element_type=F32)
        acc = part if acc is None else acc + part
    z = alpha * x_ref[...] + m_ref[g_row:g_row + 1, :] * acc
    x_new = _layer_norm(z, lg_ref[...], lb_ref[...])
    o_ref[...] = x_new
    _route(x_new, m_ref, rw_ref, rb_ref, h_ref, ids_ref, gts_ref, sh_row=g_row + 1, sc_row=g_row + 2)


def _outproj_ln(ys, ws, x, modt, ln_g, ln_b, router_w, router_b, *, g_row, alpha, tm):
    T, D = x.shape
    E = router_w.shape[1]
    mod4, mod_map = modt[0], modt[1](tm)
    y_specs, y_args, npts = [], [], []
    for y in ys:
        if isinstance(y, tuple):
            K = y[0].shape[1]
            npt, specs = _pair_specs(y, lambda r, K=K: ((tm, K), (r, 0)), tm)
            y_specs += specs
            y_args += list(y)
            npts.append(npt)
        else:
            y_specs.append(pl.BlockSpec((tm, y.shape[1]), lambda i: (i, 0)))
            y_args.append(y)
            npts.append(None)
    in_specs = (y_specs
                + [pl.BlockSpec(w.shape, lambda i: (0, 0)) for w in ws]
                + [pl.BlockSpec((tm, D), lambda i: (i, 0)),
                   pl.BlockSpec((None, None, 6, D), lambda i: mod_map(i)),
                   pl.BlockSpec((1, D), lambda i: (0, 0)),
                   pl.BlockSpec((1, D), lambda i: (0, 0)),
                   pl.BlockSpec((E, D), lambda i: (0, 0)),
                   pl.BlockSpec((E, 1), lambda i: (0, 0))])
    return pl.pallas_call(
        functools.partial(_outproj_body, npts=tuple(npts), g_row=g_row, alpha=alpha),
        grid=(T // tm,),
        in_specs=in_specs,
        out_specs=[pl.BlockSpec((tm, D), lambda i: (i, 0)),
                   pl.BlockSpec((tm * TOK_ROWS, LANE), lambda i: (i, 0)),
                   pl.BlockSpec((SUBLANE, tm), lambda i: (0, i)),
                   pl.BlockSpec((SUBLANE, tm), lambda i: (0, i))],
        out_shape=[jax.ShapeDtypeStruct((T, D), F32),
                   jax.ShapeDtypeStruct((T * TOK_ROWS, LANE), F32),
                   jax.ShapeDtypeStruct((SUBLANE, T), jnp.int32),
                   jax.ShapeDtypeStruct((SUBLANE, T), F32)],
        compiler_params=_cparams(1),
        name="outproj_ln",
    )(*y_args, *ws, x, mod4, ln_g.reshape(1, D), ln_b.reshape(1, D), router_w.T, router_b.reshape(E, 1))


def _route(x, m_ref, w_ref, b_ref, h_ref, ids_ref, gts_ref, *, sh_row, sc_row):
    h = x * (1.0 + m_ref[sc_row:sc_row + 1, :]) + m_ref[sh_row:sh_row + 1, :]
    _to_tok_blocks(h_ref, h)
    logits = lax.dot_general(w_ref[...], h, NT_DIMS, precision=HI, preferred_element_type=F32) + b_ref[...]
    E, tm = logits.shape
    row = lax.broadcasted_iota(jnp.int32, (E, tm), 0)
    mx = jnp.max(logits, axis=0, keepdims=True)
    ex = jnp.exp(logits - mx)
    p = ex / jnp.sum(ex, axis=0, keepdims=True)
    grp = row // (E // N_GROUPS)
    best = None
    for g in range(N_GROUPS):
        pg = jnp.where(grp == g, p, -1.0)
        v1 = jnp.max(pg, axis=0, keepdims=True)
        i1 = jnp.min(jnp.where(pg == v1, row, E), axis=0, keepdims=True)
        pg2 = jnp.where(row == i1, -1.0, pg)
        v2 = jnp.max(pg2, axis=0, keepdims=True)
        i2 = jnp.min(jnp.where(pg2 == v2, row, E), axis=0, keepdims=True)
        score = v1 + v2
        if best is None:
            best = (score, v1, i1, v2, i2)
        else:
            take = score > best[0]
            best = tuple(jnp.where(take, n, o) for n, o in zip((score, v1, i1, v2, i2), best))
    _, v1, i1, v2, i2 = best
    tot = v1 + v2
    slot = lax.broadcasted_iota(jnp.int32, (SUBLANE, tm), 0)
    ids_ref[...] = jnp.where(slot == 0, i1, jnp.where(slot == 1, i2, 0))
    gts_ref[...] = jnp.where(slot == 0, v1 / tot, jnp.where(slot == 1, v2 / tot, 0.0))


def _to_tok_blocks(ref, x):
    tm = x.shape[0]
    for s in range(TOK_ROWS):
        ref[pl.ds(s, tm, stride=TOK_ROWS), :] = x[:, s * LANE:(s + 1) * LANE]


def _tok_chunk(ref, s, tm):
    return ref[pl.ds(s, tm, stride=TOK_ROWS), :]


class _RowGather:
    def __init__(self, src_hbm, idx_hbm, idx_s, bufs, isem, gsem, tm):
        self.src, self.idx_hbm, self.idx_s, self.bufs = src_hbm, idx_hbm, idx_s, bufs
        self.isem, self.gsem, self.tm = isem, gsem, tm

    def _idx_copy(self, tile, slot):
        return pltpu.make_async_copy(self.idx_hbm.at[tile], self.idx_s.at[slot], self.isem.at[slot])

    def _row_copy(self, slot, a, r):
        row = pl.multiple_of(self.idx_s[slot, a, r], TOK_ROWS)
        dst = pl.multiple_of(r * TOK_ROWS, TOK_ROWS)
        return pltpu.make_async_copy(self.src.at[pl.ds(row, TOK_ROWS), :],
                                     self.bufs[a][slot].at[pl.ds(dst, TOK_ROWS), :], self.gsem.at[slot, a])

    def rows_loop(self, slot, start):
        def body(r, carry):
            for a in range(len(self.bufs)):
                cp = self._row_copy(slot, a, r)
                cp.start() if start else cp.wait()
            return carry
        lax.fori_loop(0, self.tm, body, 0, unroll=8)

    def rows_start_inline(self, slot):
        for r in range(self.tm):
            for a in range(len(self.bufs)):
                self._row_copy(slot, a, r).start(priority=(r + a) % 2)

    def prologue(self, n_tiles):
        self._idx_copy(0, 0).start()
        self._idx_copy(0, 0).wait()
        self.rows_loop(0, True)
        if n_tiles > 1:
            self._idx_copy(1, 1).start()

    def advance_indices(self, i, n_tiles):
        slot = i % 2

        @pl.when(i + 1 < n_tiles)
        def _():
            self._idx_copy(i + 1, 1 - slot).wait()

        @pl.when(i + 2 < n_tiles)
        def _():
            self._idx_copy(i + 2, slot).start()


def _ffn_body(te_ref, va_ref, idx_hbm, h_hbm, wi_ref, wo_ref, o_ref, xbuf0, xbuf1, x16, a16, idx_s, isem, gsem,
              *, F, tm, n_tiles):
    i = pl.program_id(0)
    valid = va_ref[i] != 0
    first_invalid = jnp.logical_and(jnp.logical_not(valid), va_ref[jnp.maximum(i - 1, 0)] != 0)
    xbufs = (xbuf0, xbuf1)
    g = _RowGather(h_hbm, idx_hbm, idx_s, [xbufs], isem, gsem, tm)

    @pl.when(i == 0)
    def _():
        g.prologue(n_tiles)

    for slot in (0, 1):
        here = jnp.logical_and(valid, i % 2 == slot)

        @pl.when(here)
        def _(slot=slot):
            g.rows_loop(slot, False)
            g.advance_indices(i, n_tiles)

        @pl.when(here)
        def _(slot=slot):
            if slot in FFN_INLINE_START_SLOTS:
                g.rows_start_inline(1 - slot)
            else:
                g.rows_loop(1 - slot, True)
            for s in range(TOK_ROWS):
                x16[:, s * LANE:(s + 1) * LANE] = _tok_chunk(xbufs[slot], s, tm).astype(BF16)
            x = x16[...]
            fc = F // FFN_PARTS
            for p in range(FFN_PARTS):
                u = jnp.dot(x, wi_ref[:, p * fc:(p + 1) * fc], preferred_element_type=F32)
                w = jnp.dot(x, wi_ref[:, F + p * fc:F + (p + 1) * fc], preferred_element_type=F32)
                a16[:, p * fc:(p + 1) * fc] = (_silu(u) * w).astype(BF16)
            _to_tok_blocks(o_ref, jnp.dot(a16[...], wo_ref[...], preferred_element_type=F32))

        @pl.when(jnp.logical_and(first_invalid, i % 2 == slot))
        def _(slot=slot):
            g.rows_loop(slot, False)

            @pl.when(i + 1 < n_tiles)
            def _():
                g._idx_copy(i + 1, 1 - slot).wait()

    @pl.when(jnp.logical_not(valid))
    def _():
        o_ref[...] = jnp.zeros_like(o_ref)


def _expert_ffn(h, src_tok, w_in, w_out, layer, tile_e, tile_valid):
    n_tiles, _, tm = src_tok.shape
    F, D = w_out.shape[2:]
    return pl.pallas_call(
        functools.partial(_ffn_body, F=F, tm=tm, n_tiles=n_tiles),
        grid_spec=pltpu.PrefetchScalarGridSpec(
            num_scalar_prefetch=2,
            grid=(n_tiles,),
            in_specs=[pl.BlockSpec(memory_space=pl.ANY),
                      pl.BlockSpec(memory_space=pl.ANY),
                      pl.BlockSpec((None, None, D, 2 * F), lambda i, te, va: (layer, te[i], 0, 0)),
                      pl.BlockSpec((None, None, F, D), lambda i, te, va: (layer, te[i], 0, 0))],
            out_specs=pl.BlockSpec((tm * TOK_ROWS, LANE), lambda i, te, va: (i, 0)),
            scratch_shapes=[pltpu.VMEM((tm * TOK_ROWS, LANE), F32), pltpu.VMEM((tm * TOK_ROWS, LANE), F32),
                            pltpu.VMEM((tm, D), BF16), pltpu.VMEM((tm, F), BF16), pltpu.SMEM((2, 1, tm), jnp.int32),
                            pltpu.SemaphoreType.DMA((2,)), pltpu.SemaphoreType.DMA((2, 1))]),
        out_shape=jax.ShapeDtypeStruct((n_tiles * tm * TOK_ROWS, LANE), F32),
        compiler_params=_cparams(1),
        name="expert_ffn",
    )(tile_e, tile_valid, src_tok, h, w_in, w_out)


def _combine_body(x_ref, idx_hbm, y_hbm, gt_ref, m_ref, lg_ref, lb_ref, o_ref, y0a, y0b, y1a, y1b, idx_s, isem, gsem,
                  *, g_row, alpha, tm, n_tiles):
    i = pl.program_id(0)
    y0, y1 = (y0a, y0b), (y1a, y1b)
    g = _RowGather(y_hbm, idx_hbm, idx_s, [y0, y1], isem, gsem, tm)

    @pl.when(i == 0)
    def _():
        g.prologue(n_tiles)

    for slot in (0, 1):
        @pl.when(i % 2 == slot)
        def _(slot=slot):
            g.rows_loop(slot, False)
            g.advance_indices(i, n_tiles)
            if n_tiles > 1:
                g.rows_start_inline(1 - slot)
            gt = gt_ref[...]
            g0, g1 = gt[:, 0:1], gt[:, 1:2]
            moe = jnp.concatenate([g0 * _tok_chunk(y0[slot], s, tm) + g1 * _tok_chunk(y1[slot], s, tm)
                                   for s in range(TOK_ROWS)], axis=1)
            z = alpha * x_ref[...] + m_ref[g_row:g_row + 1, :] * moe
            o_ref[...] = _layer_norm(z, lg_ref[...], lb_ref[...])

            if n_tiles > 1:
                @pl.when(i == n_tiles - 1)
                def _():
                    g.rows_loop(1 - slot, False)


def _combine_ln(x, ys, dest, gts, modt, ln_g, ln_b, *, g_row, alpha):
    T, D = x.shape
    n_tiles, _, tm = dest.shape
    mod4, mod_map = modt[0], modt[1](tm)
    row = pl.BlockSpec((tm, D), lambda i: (i, 0))
    vec = pl.BlockSpec((1, D), lambda i: (0, 0))
    hbm = pl.BlockSpec(memory_space=pl.ANY)
    return pl.pallas_call(
        functools.partial(_combine_body, g_row=g_row, alpha=alpha, tm=tm, n_tiles=n_tiles),
        grid=(n_tiles,),
        in_specs=[row, hbm, hbm, pl.BlockSpec((tm, LANE), lambda i: (i, 0)),
                  pl.BlockSpec((None, None, 6, D), lambda i: mod_map(i)), vec, vec],
        out_specs=row,
        out_shape=jax.ShapeDtypeStruct((T, D), F32),
        scratch_shapes=[pltpu.VMEM((tm * TOK_ROWS, LANE), F32)] * 4 + [
                        pltpu.SMEM((2, 2, tm), jnp.int32),
                        pltpu.SemaphoreType.DMA((2,)), pltpu.SemaphoreType.DMA((2, 2))],
        compiler_params=_cparams(1),
        name="combine_ln",
    )(x, dest, ys, gts, mod4, ln_g.reshape(1, D), ln_b.reshape(1, D))


def _route_meta(ids, tm, tm_tok):
    T = ids.shape[0]
    E = N_EXPERTS
    e_flat = ids.reshape(-1)
    onehot = (e_flat[:, None] == jnp.arange(E, dtype=jnp.int32)[None, :]).astype(jnp.int32)
    csum = jnp.cumsum(onehot, axis=0)
    rank = jnp.sum(csum * onehot, axis=1) - 1
    counts = csum[-1]
    padded = ((counts + tm - 1) // tm) * tm
    ends = jnp.cumsum(padded)
    starts = ends - padded
    dest = jnp.sum(starts[None, :] * onehot, axis=1) + rank
    n_rows = 2 * T + E * tm
    src_tok = jnp.zeros((n_rows,), jnp.int32).at[dest].set(jnp.arange(2 * T, dtype=jnp.int32) // 2,
                                                            unique_indices=True, mode="promise_in_bounds")
    tile_start = jnp.arange(n_rows // tm, dtype=jnp.int32) * tm
    valid = (tile_start < ends[-1]).astype(jnp.int32)
    probe = jnp.minimum(tile_start, ends[-1] - 1)
    tile_e = jnp.sum((ends[None, :] <= probe[:, None]).astype(jnp.int32), axis=1)
    dest_t = dest.reshape(T // tm_tok, tm_tok, 2).transpose(0, 2, 1)
    return (dest_t * TOK_ROWS, src_tok.reshape(n_rows // tm, 1, tm) * TOK_ROWS, jnp.minimum(tile_e, E - 1), valid)


def _pos_embed_2d(n_tokens, dim):
    rows = n_tokens // GRID_W
    quarter = dim // 4
    freqs = jnp.exp(-math.log(10000.0) * jnp.arange(quarter, dtype=F32) / quarter)
    r = jnp.broadcast_to(jnp.arange(rows, dtype=F32)[:, None], (rows, GRID_W)).reshape(-1)
    col = jnp.broadcast_to(jnp.arange(GRID_W, dtype=F32)[None, :], (rows, GRID_W)).reshape(-1)
    ar = r[:, None] * freqs
    ac = col[:, None] * freqs
    return jnp.concatenate([jnp.sin(ar), jnp.cos(ar), jnp.sin(ac), jnp.cos(ac)], -1)


def _pad_cols(w, n):
    return jnp.pad(w, ((0, 0),) * (w.ndim - 1) + ((0, n - w.shape[-1]),))


def kernel(x_prompt, x_sample, state_lru, state_mlstm_c, state_mlstm_n, state_mlstm_m, state_gla, c, c_ctx, mod_w, mod_b, ln_g, ln_b, even_w_in, even_w_out, lru_conv_w, lru_conv_b, lru_gate_w, lru_gate_b, lru_lambda, mlstm_gate_b, mlstm_norm_g, odd_w_in, odd_w_out, gla_gate_w, gla_gate_b, gla_norm_g, router_w, router_b, moe_w_in, moe_w_out):
    Bp, Lp, D = x_prompt.shape
    Bs, Ls, _ = x_sample.shape
    depth = mod_w.shape[0]
    Tp, Ts = Bp * Lp, Bs * Ls
    T = Tp + Ts
    alpha = (2 * depth) ** 0.25
    D_A = lru_conv_w.shape[-1]
    D_B = mlstm_norm_g.shape[-1]
    DK = gla_gate_w.shape[-1]
    DV = gla_norm_g.shape[-1]
    tm = next(t for t in (512, 256, 128) if Tp % t == 0 and Ls % t == 0)
    tn = 1024

    n_cond = 1 + Bs
    R = -(-n_cond // SUBLANE) * SUBLANE
    cond = jnp.zeros((R, D), F32).at[0].set(c_ctx).at[1:n_cond].set(c)
    mod_all = _modulation(cond, mod_w, mod_b)
    mod4 = mod_all.reshape(depth, R, 6, D)
    tm_proj = next(t for t in (1024, 512, 256, 128) if Tp % t == 0 and Ls % t == 0)
    assert D == TOK_ROWS * LANE

    x = _embed(x_prompt.reshape(Tp, D), x_sample.reshape(Ts, D), _pos_embed_2d(Ls, D), tm)

    moe_w_in_b = moe_w_in.astype(BF16)
    moe_w_out_b = moe_w_out.astype(BF16)
    groups = ((0, Bp, Lp), (Tp, Bs, Ls))

    s_lru, s_c, s_n, s_m, s_gla = [], [], [], [], []
    for l in range(depth):
        j = l // 2
        mt = (mod4, lambda t, l=l: (lambda i: (l, jnp.where(i < Tp // t, 0, 1 + (i - Tp // t) // (Ls // t)), 0, 0)))
        if l % 2 == 0:
            w_in = even_w_in[j]
            n_main = 2 * D_A + 4 * D_B
            proj, gates = _proj(x, mt, w_in[:, :n_main].astype(BF16), _pad_cols(w_in[:, n_main:], LANE),
                                sh_row=0, sc_row=1, tm=tm_proj, tn=tn)
            gate_bias = _pad_cols(mlstm_gate_b[j].reshape(1, -1), LANE)
            ya, hs = [], []
            for gi, (row0, nseq, L) in enumerate(groups):
                if gi == 0:
                    lru0 = jnp.zeros((nseq, 2, D_A), F32)
                    c0 = jnp.zeros((nseq, 2, B_HEADS, D_B // B_HEADS, D_B // B_HEADS), F32)
                    n0 = jnp.zeros((nseq, 2, B_HEADS, D_B // B_HEADS), F32)
                    m0 = jnp.zeros((nseq, 2, B_HEADS), F32)
                else:
                    lru0, c0, n0, m0 = state_lru[:, j], state_mlstm_c[:, j], state_mlstm_n[:, j], state_mlstm_m[:, j]
                y_a, st = _lru(proj, row0, nseq, L, lru_conv_w[j], lru_conv_b[j], lru_gate_w[j], lru_gate_b[j],
                               lru_lambda[j], lru0)
                h, C, n, m = _mlstm(proj, gates, gate_bias, row0, nseq, L, 2 * D_A, c0, n0, m0)
                ya.append(y_a)
                hs.append(h)
                if gi == 0:
                    s_lru.append(st)
                    s_c.append(C)
                    s_n.append(n.reshape(nseq, 2, B_HEADS, -1))
                    s_m.append(m.reshape(nseq, 2, B_HEADS))
            h_fwd, h_bwd = zip(*hs)
            y_b = _mlstm_out(h_fwd, h_bwd, proj, 2 * D_A + 3 * D_B, mlstm_norm_g[j], tm)
            w_out = even_w_out[j].astype(BF16)
            x, hf, ids, gts = _outproj_ln([tuple(ya), y_b], [w_out[:D_A], w_out[D_A:]], x, mt, ln_g[l, 0],
                                          ln_b[l, 0], router_w, router_b, g_row=2, alpha=alpha, tm=tm)
        else:
            w_in = odd_w_in[j]
            n_main = 2 * DK + 2 * DV
            proj, lr = _proj(x, mt, w_in[:, :n_main].astype(BF16), _pad_cols(w_in[:, n_main:], LANE),
                             sh_row=0, sc_row=1, tm=tm_proj, tn=tn)
            gw_pad = jnp.zeros((2, LANE, DK), F32)
            for dd in range(2):
                gw_pad = gw_pad.at[dd, dd * GLA_RANK:(dd + 1) * GLA_RANK].set(gla_gate_w[j, dd])
            os_ = []
            for gi, (row0, nseq, L) in enumerate(groups):
                s0 = (jnp.zeros((nseq, 2, C_HEADS, DK // C_HEADS, DV // C_HEADS), F32) if gi == 0
                      else state_gla[:, j])
                o, S = _gla(proj, lr, gw_pad, gla_gate_b[j], row0, nseq, L, s0)
                os_.append(o)
                if gi == 0:
                    s_gla.append(S)
            o_fwd, o_bwd = zip(*os_)
            y = _gla_out(o_fwd, o_bwd, proj, 2 * DK + DV, gla_norm_g[j], tm)
            x, hf, ids, gts = _outproj_ln([y], [odd_w_out[j].astype(BF16)], x, mt, ln_g[l, 0], ln_b[l, 0],
                                          router_w, router_b, g_row=2, alpha=alpha, tm=tm)

        gts = _pad_cols(gts[:2].T, LANE)
        dest, src_tok, tile_e, tile_valid = _route_meta(ids[:2].T, MOE_TM, tm)
        ys = _expert_ffn(hf, src_tok, moe_w_in_b, moe_w_out_b, l, tile_e, tile_valid)
        x = _combine_ln(x, ys, dest, gts, mt, ln_g[l, 1], ln_b[l, 1], g_row=5, alpha=alpha)

    y_prompt = x[:Tp].reshape(Bp, Lp, D)
    y_sample = x[Tp:].reshape(Bs, Ls, D)
    return (y_prompt, y_sample, jnp.stack(s_lru, 1), jnp.stack(s_c, 1), jnp.stack(s_n, 1),
            jnp.stack(s_m, 1), jnp.stack(s_gla, 1))
```

```python
import functools
import math

import jax
import jax.numpy as jnp
from jax import lax
from jax.experimental import pallas as pl
from jax.experimental.pallas import tpu as pltpu

F32 = jnp.float32
BF16 = jnp.bfloat16
HI = lax.Precision.HIGHEST

LN_EPS = 1e-5
LRU_C = 8.0
A_BLOCKS = 8
B_HEADS = 4
C_HEADS = 4
GLA_RANK = 16
GLA_TAU = 16.0
N_EXPERTS = 16
N_GROUPS = 4
GRID_W = 64
LANE = 128
SUBLANE = 8
SCAN_CHUNK = 256
MOE_TM = 512
FFN_PARTS = 4
FFN_INLINE_START_SLOTS = (0,)
TOK_ROWS = 16
VMEM_LIMIT = 56 * 1024 * 1024

NT_DIMS = (((1,), (1,)), ((), ()))
TN_DIMS = (((0,), (0,)), ((), ()))


def _cparams(n_axes):
    return pltpu.CompilerParams(dimension_semantics=("arbitrary",) * n_axes,
                                vmem_limit_bytes=VMEM_LIMIT)


def _sigmoid(x):
    return 1.0 / (1.0 + jnp.exp(-x))


def _silu(x):
    return x * _sigmoid(x)


def _log_sigmoid(x):
    return jnp.minimum(x, 0.0) - jnp.log1p(jnp.exp(-jnp.abs(x)))


def _gelu_tanh(x):
    return 0.5 * x * (1.0 + jnp.tanh(math.sqrt(2.0 / math.pi) * (x + 0.044715 * (x * x * x))))


def _mod_body(c_ref, w_ref, b_ref, o_ref):
    c = c_ref[...]
    o_ref[...] = jnp.dot(_silu(c), w_ref[...], precision=HI, preferred_element_type=F32) + b_ref[...]


def _modulation(cond, mod_w, mod_b):
    R, D = cond.shape
    nl, _, N = mod_w.shape
    tn = 1024
    return pl.pallas_call(
        _mod_body,
        grid=(nl, N // tn),
        in_specs=[pl.BlockSpec((R, D), lambda l, j: (0, 0)),
                  pl.BlockSpec((None, D, tn), lambda l, j: (l, 0, j)),
                  pl.BlockSpec((None, 1, tn), lambda l, j: (l, 0, j))],
        out_specs=pl.BlockSpec((None, R, tn), lambda l, j: (l, 0, j)),
        out_shape=jax.ShapeDtypeStruct((nl, R, N), F32),
        compiler_params=_cparams(2),
        name="modulation",
    )(cond, mod_w, mod_b.reshape(nl, 1, N))


def _embed_body(xp_ref, xs_ref, pos_ref, o_ref, *, n_prompt_tiles):
    i = pl.program_id(0)

    @pl.when(i < n_prompt_tiles)
    def _():
        o_ref[...] = xp_ref[...]

    @pl.when(i >= n_prompt_tiles)
    def _():
        o_ref[...] = xs_ref[...] + pos_ref[...]


def _embed(xp, xs, pos, tm):
    Tp, D = xp.shape
    Ts = xs.shape[0]
    Ls = pos.shape[0]
    npt, nst, npos = Tp // tm, Ts // tm, Ls // tm
    return pl.pallas_call(
        functools.partial(_embed_body, n_prompt_tiles=npt),
        grid=(npt + nst,),
        in_specs=[pl.BlockSpec((tm, D), lambda i: (jnp.minimum(i, npt - 1), 0)),
                  pl.BlockSpec((tm, D), lambda i: (jnp.maximum(i - npt, 0), 0)),
                  pl.BlockSpec((tm, D), lambda i: (jnp.maximum(i - npt, 0) % npos, 0))],
        out_specs=pl.BlockSpec((tm, D), lambda i: (i, 0)),
        out_shape=jax.ShapeDtypeStruct((Tp + Ts, D), F32),
        compiler_params=_cparams(1),
        name="embed",
    )(xp, xs, pos)


def _split3(x):
    hi = x.astype(BF16)
    r1 = x - hi.astype(F32)
    mid = r1.astype(BF16)
    return hi, mid, (r1 - mid.astype(F32)).astype(BF16)


def _proj_body(x_ref, m_ref, w_ref, wg_ref, o_ref, og_ref, h_ref, *, sh_row, sc_row):
    @pl.when(pl.program_id(1) == 0)
    def _():
        h = x_ref[...] * (1.0 + m_ref[sc_row:sc_row + 1, :]) + m_ref[sh_row:sh_row + 1, :]
        h_hi, h_mid, _ = _split3(h)
        h_ref[...] = h_hi
        wg_hi, wg_mid, _ = _split3(wg_ref[...])
        og_ref[...] = (jnp.dot(h_hi, wg_hi, preferred_element_type=F32)
                       + jnp.dot(h_hi, wg_mid, preferred_element_type=F32)
                       + jnp.dot(h_mid, wg_hi, preferred_element_type=F32))

    o_ref[...] = jnp.dot(h_ref[...], w_ref[...], preferred_element_type=F32)


def _proj(x, modt, w, w_gate, *, sh_row, sc_row, tm, tn):
    T, D = x.shape
    N = w.shape[1]
    mod4, mod_map = modt[0], modt[1](tm)
    return pl.pallas_call(
        functools.partial(_proj_body, sh_row=sh_row, sc_row=sc_row),
        grid=(T // tm, N // tn),
        in_specs=[pl.BlockSpec((tm, D), lambda i, j: (i, 0)),
                  pl.BlockSpec((None, None, 6, D), lambda i, j: mod_map(i)),
                  pl.BlockSpec((D, tn), lambda i, j: (0, j)),
                  pl.BlockSpec((D, LANE), lambda i, j: (0, 0))],
        out_specs=[pl.BlockSpec((tm, tn), lambda i, j: (i, j)),
                   pl.BlockSpec((tm, LANE), lambda i, j: (i, 0))],
        out_shape=[jax.ShapeDtypeStruct((T, N), F32), jax.ShapeDtypeStruct((T, LANE), F32)],
        scratch_shapes=[pltpu.VMEM((tm, D), BF16)],
        compiler_params=_cparams(2),
        name="proj",
    )(x, mod4, w, w_gate)


def _lru_body(ag_ref, ax_ref, cw_ref, cb_ref, gw_ref, gb_ref, lam_ref, h0_ref, y_ref, st_ref,
              af_s, uf_s, ab_s, ub_s, hf_s, hb_s, *, L):
    x = ax_ref[...]
    row = lax.broadcasted_iota(jnp.int32, (L, LANE), 0)
    xm1 = jnp.where(row >= 1, pltpu.roll(x, 1, 0), 0.0)
    xp1 = jnp.where(row < L - 1, pltpu.roll(x, L - 1, 0), 0.0)
    xp2 = jnp.where(row < L - 2, pltpu.roll(x, L - 2, 0), 0.0)
    xc = cb_ref[...] + xm1 * cw_ref[0:1, :] + x * cw_ref[1:2, :] + xp1 * cw_ref[2:3, :] + xp2 * cw_ref[3:4, :]
    xcb = xc.astype(BF16)
    nj = L // SUBLANE
    sub = lax.broadcasted_iota(jnp.int32, (nj, SUBLANE, LANE), 1)

    for d, (a_s, u_s) in enumerate(((af_s, uf_s), (ab_s, ub_s))):
        g = jnp.dot(xcb, gw_ref[d], preferred_element_type=F32) + gb_ref[d]
        r = _sigmoid(g[:, :LANE])
        ig = _sigmoid(g[:, LANE:])
        nlam = -lam_ref[d]
        softplus = jnp.maximum(nlam, 0.0) + jnp.log1p(jnp.exp(-jnp.abs(nlam)))
        log_a = (-LRU_C * softplus) * r
        a = jnp.exp(log_a)
        u = jnp.sqrt(jnp.tanh(-log_a) * (1.0 + a * a)) * ig * xc
        a3 = a.reshape(nj, SUBLANE, LANE)
        u3 = u.reshape(nj, SUBLANE, LANE)
        for k in (1, 2, 4):
            if d == 0:
                sh, keep = k, sub >= k
            else:
                sh, keep = SUBLANE - k, sub < SUBLANE - k
            a_sh = pltpu.roll(a3, sh, 1)
            u_sh = pltpu.roll(u3, sh, 1)
            u3 = jnp.where(keep, a3 * u_sh + u3, u3)
            a3 = jnp.where(keep, a3 * a_sh, a3)
        a_s[...] = a3.reshape(L, LANE)
        u_s[...] = u3.reshape(L, LANE)

    def carry(j, hs):
        hf, hb = hs
        rf = pl.multiple_of(j * SUBLANE, SUBLANE)
        rb = pl.multiple_of((nj - 1 - j) * SUBLANE, SUBLANE)
        of = af_s[pl.ds(rf, SUBLANE), :] * hf + uf_s[pl.ds(rf, SUBLANE), :]
        ob = ab_s[pl.ds(rb, SUBLANE), :] * hb + ub_s[pl.ds(rb, SUBLANE), :]
        hf_s[pl.ds(rf, SUBLANE), :] = of
        hb_s[pl.ds(rb, SUBLANE), :] = ob
        return (jnp.broadcast_to(of[SUBLANE - 1:SUBLANE, :], (SUBLANE, LANE)),
                jnp.broadcast_to(ob[0:1, :], (SUBLANE, LANE)))

    h0f = jnp.broadcast_to(h0_ref[0:1, :], (SUBLANE, LANE))
    h0b = jnp.broadcast_to(h0_ref[1:2, :], (SUBLANE, LANE))
    lax.fori_loop(0, nj, carry, (h0f, h0b))
    y_ref[...] = (_gelu_tanh(ag_ref[...]) * (hf_s[...] + hb_s[...])).astype(y_ref.dtype)
    st_ref[0:1, :] = hf_s[L - 1:L, :]
    st_ref[1:2, :] = hb_s[0:1, :]


def _lru(proj, row0, nseq, L, conv_w, conv_b, gate_w, gate_b, lam, h0):
    D_A = conv_w.shape[1]
    nb = D_A // LANE
    rb0 = row0 // L
    in_specs = [pl.BlockSpec((L, LANE), lambda s, h: (rb0 + s, h)),
                pl.BlockSpec((L, LANE), lambda s, h: (rb0 + s, nb + h)),
                pl.BlockSpec((4, LANE), lambda s, h: (0, h)),
                pl.BlockSpec((1, LANE), lambda s, h: (0, h)),
                pl.BlockSpec((2, None, LANE, 2 * LANE), lambda s, h: (0, h, 0, 0)),
                pl.BlockSpec((2, None, 1, 2 * LANE), lambda s, h: (0, h, 0, 0)),
                pl.BlockSpec((2, None, 1, LANE), lambda s, h: (0, h, 0, 0)),
                pl.BlockSpec((None, 2, LANE), lambda s, h: (s, 0, h))]
    args = [proj, proj, conv_w, conv_b.reshape(1, D_A), gate_w.astype(BF16),
            gate_b.reshape(2, nb, 1, 2 * LANE), lam.reshape(2, nb, 1, LANE), h0]
    return pl.pallas_call(
        functools.partial(_lru_body, L=L),
        grid=(nseq, nb),
        in_specs=in_specs,
        out_specs=[pl.BlockSpec((L, LANE), lambda s, h: (s, h)),
                   pl.BlockSpec((None, 2, LANE), lambda s, h: (s, 0, h))],
        out_shape=[jax.ShapeDtypeStruct((nseq * L, D_A), BF16),
                   jax.ShapeDtypeStruct((nseq, 2, D_A), F32)],
        scratch_shapes=[pltpu.VMEM((L, LANE), F32)] * 6,
        compiler_params=_cparams(2),
        name="rglru",
    )(*args)


def _tri_mask(Tc, d):
    r = lax.broadcasted_iota(jnp.int32, (Tc, Tc), 0)
    c = lax.broadcasted_iota(jnp.int32, (Tc, Tc), 1)
    return (c - r) * (1 - 2 * d) <= 0


def _mlstm_dir(d, q_ref, k_ref, v_ref, g_ref, gb_ref, h_ref, C_s, n_s, m_s, *, Tc, k_scale, dh):
    H = B_HEADS
    last = Tc - 1 if d == 0 else 0
    lane = lax.broadcasted_iota(jnp.int32, (Tc, LANE), 1)
    G = g_ref[...] + gb_ref[...]
    tri = _tri_mask(Tc, d)
    Bm = jnp.dot(tri.astype(F32), _log_sigmoid(G), precision=HI, preferred_element_type=F32)
    col0 = d * (2 * H)
    X = jnp.where(lane >= col0 + H, Bm, G)
    srow = lax.broadcasted_iota(jnp.int32, (2 * H, LANE), 0)
    slane = lax.broadcasted_iota(jnp.int32, (2 * H, LANE), 1)
    sel = (slane == col0 + srow).astype(F32)
    R = lax.dot_general(sel, X, NT_DIMS, precision=HI, preferred_element_type=F32)
    for hd in range(H):
        i_col = jnp.sum(jnp.where(lane == col0 + hd, X, 0.0), axis=1, keepdims=True)
        b_col = jnp.sum(jnp.where(lane == col0 + H + hd, X, 0.0), axis=1, keepdims=True)
        i_row = R[hd:hd + 1, :]
        b_row = R[H + hd:H + hd + 1, :]
        dmat = jnp.where(tri, b_col - b_row + i_row, -jnp.inf)
        m_prev = m_s[d, hd]
        m_inter = b_col + m_prev
        m_t = jnp.maximum(m_inter, jnp.max(dmat, axis=1, keepdims=True))
        cs = slice(hd * dh, (hd + 1) * dh)
        qf = q_ref[:, cs]
        qb = qf.astype(BF16)
        kf = k_ref[:, cs] * k_scale
        kb = kf.astype(BF16)
        vb = v_ref[:, cs].astype(BF16)
        S = lax.dot_general(qb, kb, NT_DIMS, preferred_element_type=F32) * jnp.exp(dmat - m_t)
        inter_scale = jnp.exp(m_inter - m_t)
        Cm = C_s[d, hd]
        num =(jnp.dot(S.astype(BF16), vb, preferred_element_type=F32)
               + inter_scale * jnp.dot(qb, Cm.astype(BF16), preferred_element_type=F32))
        nv = n_s[d, hd]
        qn = jnp.sum(qf * nv, axis=1, keepdims=True)
        den = jnp.sum(S, axis=1, keepdims=True) + inter_scale * qn
        h_ref[:, cs] = num / jnp.maximum(jnp.abs(den), jnp.exp(-m_t))
        bL = b_col[last:last + 1, :]
        g_col = bL - b_col + i_col
        m_new = jnp.maximum(bL + m_prev, jnp.max(g_col, axis=0, keepdims=True))
        wk = jnp.exp(g_col - m_new)
        decay = jnp.exp(bL + m_prev - m_new)
        kw = kf * wk
        C_s[d, hd] = decay * Cm + lax.dot_general(kw.astype(BF16), vb, TN_DIMS, preferred_element_type=F32)
        n_s[d, hd] = decay * nv + jnp.sum(kw, axis=0, keepdims=True)
        m_s[d, hd] = m_new


def _mlstm_body(qf_ref, kf_ref, vf_ref, gf_ref, qb_ref, kb_ref, vb_ref, gbk_ref, gb_ref, c0_ref, n0_ref, m0_ref,
                hf_ref, hb_ref, co_ref, no_ref, mo_ref, C_s, n_s, m_s, *, nc, **kw):
    c = pl.program_id(1)

    @pl.when(c == 0)
    def _():
        C_s[...] = c0_ref[...]
        n_s[...] = n0_ref[...]
        m_s[...] = m0_ref[...]

    _mlstm_dir(0, qf_ref, kf_ref, vf_ref, gf_ref, gb_ref, hf_ref, C_s, n_s, m_s, **kw)
    _mlstm_dir(1, qb_ref, kb_ref, vb_ref, gbk_ref, gb_ref, hb_ref, C_s, n_s, m_s, **kw)

    @pl.when(c == nc - 1)
    def _():
        co_ref[...] = C_s[...]
        no_ref[...] = n_s[...]
        mo_ref[...] = m_s[...]


def _mlstm(proj, gates, gate_bias, row0, nseq, L, col0, c0, n0, m0):
    H = B_HEADS
    dh = c0.shape[-1]
    D_B = H * dh
    Tc = min(SCAN_CHUNK, L)
    nc = L // Tc
    rb0 = row0 // Tc
    cb0 = col0 // D_B
    fwd = lambda s, c: s * nc + c
    bwd = lambda s, c: s * nc + nc - 1 - c

    def chunk_specs(rb):
        return [pl.BlockSpec((Tc, D_B), lambda s, c, off=off: (rb0 + rb(s, c), cb0 + off)) for off in range(3)] + [
            pl.BlockSpec((Tc, LANE), lambda s, c: (rb0 + rb(s, c), 0))]

    def st_specs():
        st_map = lambda s, c: (s, 0, 0, 0, 0)
        return [pl.BlockSpec((None, 2, H, dh, dh), st_map), pl.BlockSpec((None, 2, H, 1, dh), st_map),
                pl.BlockSpec((None, 2, H, 1, 1), st_map)]

    hf, hb, C, n, m = pl.pallas_call(
        functools.partial(_mlstm_body, Tc=Tc, nc=nc, k_scale=dh ** -0.5, dh=dh),
        grid=(nseq, nc),
        in_specs=chunk_specs(fwd) + chunk_specs(bwd) + [pl.BlockSpec((1, LANE), lambda s, c: (0, 0))] + st_specs(),
        out_specs=[pl.BlockSpec((Tc, D_B), lambda s, c: (fwd(s, c), 0)),
                   pl.BlockSpec((Tc, D_B), lambda s, c: (bwd(s, c), 0))] + st_specs(),
        out_shape=[jax.ShapeDtypeStruct((nseq * L, D_B), F32),
                   jax.ShapeDtypeStruct((nseq * L, D_B), F32),
                   jax.ShapeDtypeStruct((nseq, 2, H, dh, dh), F32),
                   jax.ShapeDtypeStruct((nseq, 2, H, 1, dh), F32),
                   jax.ShapeDtypeStruct((nseq, 2, H, 1, 1), F32)],
        scratch_shapes=[pltpu.VMEM((2, H, dh, dh), F32), pltpu.VMEM((2, H, 1, dh), F32),
                        pltpu.VMEM((2, H, 1, 1), F32)],
        compiler_params=_cparams(2),
        name="mlstm",
    )(*([proj] * 3 + [gates]) * 2, gate_bias, c0, n0.reshape(nseq, 2, H, 1, dh), m0.reshape(nseq, 2, H, 1, 1))
    return (hf, hb), C, n, m


def _pair_specs(pair, block, tm):
    npt = pair[0].shape[-2] // tm
    (shape, _), mk = block(0), block
    return npt, [pl.BlockSpec(shape, lambda i, *rest: mk(jnp.minimum(i, npt - 1), *rest)[1]),
                 pl.BlockSpec(shape, lambda i, *rest: mk(jnp.maximum(i - npt, 0), *rest)[1])]


def _pick(p_ref, s_ref, npt):
    return jnp.where(pl.program_id(0) < npt, p_ref[...], s_ref[...])


def _mlstm_out_body(hfp_ref, hfs_ref, hbp_ref, hbs_ref, o_ref, g_ref, y_ref, *, npt):
    hm = _pick(hfp_ref, hfs_ref, npt) + _pick(hbp_ref, hbs_ref, npt)
    mu = jnp.mean(hm, axis=1, keepdims=True)
    xc = hm - mu
    var = jnp.mean(xc * xc, axis=1, keepdims=True)
    y_ref[...] = (_sigmoid(o_ref[...]) * (xc * lax.rsqrt(var + LN_EPS) * g_ref[...])).astype(y_ref.dtype)


def _mlstm_out(h_fwd, h_bwd, proj, col0, norm_g, tm):
    T = proj.shape[0]
    D_B = h_fwd[0].shape[-1]
    dh = D_B // B_HEADS
    cb0 = col0 // dh
    blk = lambda r, h=0: ((tm, dh), (r, h))
    npt, f_specs = _pair_specs(h_fwd, blk, tm)
    _, b_specs = _pair_specs(h_bwd, blk, tm)
    return pl.pallas_call(
        functools.partial(_mlstm_out_body, npt=npt),
        grid=(T // tm, B_HEADS),
        in_specs=f_specs + b_specs + [pl.BlockSpec((tm, dh), lambda i, h: (i, cb0 + h)),
                                      pl.BlockSpec((1, dh), lambda i, h: (0, h))],
        out_specs=pl.BlockSpec((tm, dh), lambda i, h: (i, h)),
        out_shape=jax.ShapeDtypeStruct((T, D_B), BF16),
        compiler_params=_cparams(2),
        name="mlstm_out",
    )(*h_fwd, *h_bwd, proj, norm_g.reshape(1, D_B))


def _gla_dir(d, q_ref, k_ref, v_ref, lr_ref, gw_ref, gb_ref, o_ref, ST_s, *, Tc, q_scale, dk, dv):
    H = C_HEADS
    last = Tc - 1 if d == 0 else 0
    lr_hi, lr_mid, _ = _split3(lr_ref[...])
    gw_hi, gw_mid, _ = _split3(gw_ref[d])
    z = (jnp.dot(lr_hi, gw_hi, preferred_element_type=F32)
         + jnp.dot(lr_hi, gw_mid, preferred_element_type=F32)
         + jnp.dot(lr_mid, gw_hi, preferred_element_type=F32)) + gb_ref[d]
    loga = _log_sigmoid(z) * (1.0 / GLA_TAU)
    tri = _tri_mask(Tc, d)
    trib = tri.astype(F32).astype(BF16)
    b = sum(jnp.dot(trib, piece, preferred_element_type=F32) for piece in _split3(loga))
    qs = (q_ref[...] * q_scale * jnp.exp(b)).astype(BF16)
    kf = k_ref[...]
    ke = (kf * jnp.exp(-b)).astype(BF16)
    bL = b[last:last + 1, :]
    kd = (kf * jnp.exp(bL - b)).astype(BF16)
    ebL = jnp.exp(bL)
    for hd in range(H):
        ks = slice(hd * dk, (hd + 1) * dk)
        vs = slice(hd * dv, (hd + 1) * dv)
        vb = v_ref[:, vs].astype(BF16)
        att = jnp.where(tri, lax.dot_general(qs[:, ks], ke[:, ks], NT_DIMS, preferred_element_type=F32), 0.0)
        ST = ST_s[d, hd]
        inter = lax.dot_general(qs[:, ks], ST.astype(BF16), NT_DIMS, preferred_element_type=F32)
        o_ref[:, vs] = inter + jnp.dot(att.astype(BF16), vb, preferred_element_type=F32)
        ST_s[d, hd] = ST * ebL[:, ks] + lax.dot_general(vb, kd[:, ks], TN_DIMS, preferred_element_type=F32)


def _gla_body(qf_ref, kf_ref, vf_ref, lrf_ref, qb_ref, kb_ref, vb_ref, lrb_ref, gw_ref, gb_ref, s0_ref,
              of_ref, ob_ref, so_ref, ST_s, *, nc, **kw):
    c = pl.program_id(1)
    states = [(d, hd) for d in range(2) for hd in range(C_HEADS)]

    @pl.when(c == 0)
    def _():
        for d, hd in states:
            ST_s[d, hd] = s0_ref[d, hd].T

    _gla_dir(0, qf_ref, kf_ref, vf_ref, lrf_ref, gw_ref, gb_ref, of_ref, ST_s, **kw)
    _gla_dir(1, qb_ref, kb_ref, vb_ref, lrb_ref, gw_ref, gb_ref, ob_ref, ST_s, **kw)

    @pl.when(c == nc - 1)
    def _():
        for d, hd in states:
            so_ref[d, hd] = ST_s[d, hd].T


def _gla(proj, lr, gw_pad, gate_b, row0, nseq, L, s0):
    H = C_HEADS
    dk, dv = s0.shape[-2:]
    DK, DV = H * dk, H * dv
    Tc = min(SCAN_CHUNK, L)
    nc = L // Tc
    rb0 = row0 // Tc
    fwd = lambda s, c: s * nc + c
    bwd = lambda s, c: s * nc + nc - 1 - c

    def chunk_specs(rb):
        return [pl.BlockSpec((Tc, DK), lambda s, c: (rb0 + rb(s, c), 0)),
                pl.BlockSpec((Tc, DK), lambda s, c: (rb0 + rb(s, c), 1)),
                pl.BlockSpec((Tc, DV), lambda s, c: (rb0 + rb(s, c), (2 * DK) // DV)),
                pl.BlockSpec((Tc, LANE), lambda s, c: (rb0 + rb(s, c), 0))]

    st_map = lambda s, c: (s, 0, 0, 0, 0)
    of, ob, S = pl.pallas_call(
        functools.partial(_gla_body, Tc=Tc, nc=nc, q_scale=dk ** -0.5, dk=dk, dv=dv),
        grid=(nseq, nc),
        in_specs=chunk_specs(fwd) + chunk_specs(bwd) + [
            pl.BlockSpec((2, LANE, DK), lambda s, c: (0, 0, 0)),
            pl.BlockSpec((2, 1, DK), lambda s, c: (0, 0, 0)),
            pl.BlockSpec((None, 2, H, dk, dv), st_map)],
        out_specs=[pl.BlockSpec((Tc, DV), lambda s, c: (fwd(s, c), 0)),
                   pl.BlockSpec((Tc, DV), lambda s, c: (bwd(s, c), 0)),
                   pl.BlockSpec((None, 2, H, dk, dv), st_map)],
        out_shape=[jax.ShapeDtypeStruct((nseq * L, DV), F32),
                   jax.ShapeDtypeStruct((nseq * L, DV), F32),
                   jax.ShapeDtypeStruct((nseq, 2, H, dk, dv), F32)],
        scratch_shapes=[pltpu.VMEM((2, H, dv, dk), F32)],
        compiler_params=_cparams(2),
        name="gla",
    )(*([proj] * 3 + [lr]) * 2, gw_pad, gate_b.reshape(2, 1, DK), s0)
    return (of, ob), S


def _gla_out_body(ofp_ref, ofs_ref, obp_ref, obs_ref, g_ref, ng_ref, y_ref, *, npt):
    oo = _pick(ofp_ref, ofs_ref, npt) + _pick(obp_ref, obs_ref, npt)
    ms = jnp.mean(oo * oo, axis=1, keepdims=True)
    y_ref[...] = (oo * lax.rsqrt(ms + LN_EPS) * ng_ref[...] * _silu(g_ref[...])).astype(y_ref.dtype)


def _gla_out(o_fwd, o_bwd, proj, col0, norm_g, tm):
    T = proj.shape[0]
    DV = o_fwd[0].shape[-1]
    dv = DV // C_HEADS
    cb0 = col0 // dv
    blk = lambda r, h=0: ((tm, dv), (r, h))
    npt, f_specs = _pair_specs(o_fwd, blk, tm)
    _, b_specs = _pair_specs(o_bwd, blk, tm)
    return pl.pallas_call(
        functools.partial(_gla_out_body, npt=npt),
        grid=(T // tm, C_HEADS),
        in_specs=f_specs + b_specs + [pl.BlockSpec((tm, dv), lambda i, h: (i, cb0 + h)),
                                      pl.BlockSpec((1, dv), lambda i, h: (0, h))],
        out_specs=pl.BlockSpec((tm, dv), lambda i, h: (i, h)),
        out_shape=jax.ShapeDtypeStruct((T, DV), BF16),
        compiler_params=_cparams(2),
        name="gla_out",
    )(*o_fwd, *o_bwd, proj, norm_g.reshape(1, DV))


def _layer_norm(z, g, b):
    mu = jnp.mean(z, axis=1, keepdims=True)
    zc = z - mu
    var = jnp.mean(zc * zc, axis=1, keepdims=True)
    return zc * lax.rsqrt(var + LN_EPS) * g + b


def _outproj_body(*refs, npts, g_row, alpha):
    n = len(npts)
    n_y = sum(1 if p is None else 2 for p in npts)
    y_refs = list(refs[:n_y])
    ws = refs[n_y:n_y + n]
    x_ref, m_ref, lg_ref, lb_ref, rw_ref, rb_ref, o_ref, h_ref, ids_ref, gts_ref = refs[n_y + n:]
    acc = None
    for npt, w in zip(npts, ws):
        y = y_refs.pop(0)[...] if npt is None else _pick(y_refs.pop(0), y_refs.pop(0), npt)
        part = jnp.dot(y, w[...], preferred_element_type=F32)
        acc = part if acc is None else acc + part
    z = alpha * x_ref[...] + m_ref[g_row:g_row + 1, :] * acc
    x_new = _layer_norm(z, lg_ref[...], lb_ref[...])
    o_ref[...] = x_new
    _route(x_new, m_ref, rw_ref, rb_ref, h_ref, ids_ref, gts_ref, sh_row=g_row + 1, sc_row=g_row + 2)


def _outproj_ln(ys, ws, x, modt, ln_g, ln_b, router_w, router_b, *, g_row, alpha, tm):
    T, D = x.shape
    E = router_w.shape[1]
    mod4, mod_map = modt[0], modt[1](tm)
    y_specs, y_args, npts = [], [], []
    for y in ys:
        if isinstance(y, tuple):
            K = y[0].shape[1]
            npt, specs = _pair_specs(y, lambda r, K=K: ((tm, K), (r, 0)), tm)
            y_specs += specs
            y_args += list(y)
            npts.append(npt)
        else:
            y_specs.append(pl.BlockSpec((tm, y.shape[1]), lambda i: (i, 0)))
            y_args.append(y)
            npts.append(None)
    in_specs = (y_specs
                + [pl.BlockSpec(w.shape, lambda i: (0, 0)) for w in ws]
                + [pl.BlockSpec((tm, D), lambda i: (i, 0)),
                   pl.BlockSpec((None, None, 6, D), lambda i: mod_map(i)),
                   pl.BlockSpec((1, D), lambda i: (0, 0)),
                   pl.BlockSpec((1, D), lambda i: (0, 0)),
                   pl.BlockSpec((E, D), lambda i: (0, 0)),
                   pl.BlockSpec((E, 1), lambda i: (0, 0))])
    return pl.pallas_call(
        functools.partial(_outproj_body, npts=tuple(npts), g_row=g_row, alpha=alpha),
        grid=(T // tm,),
        in_specs=in_specs,
        out_specs=[pl.BlockSpec((tm, D), lambda i: (i, 0)),
                   pl.BlockSpec((tm * TOK_ROWS, LANE), lambda i: (i, 0)),
                   pl.BlockSpec((SUBLANE, tm), lambda i: (0, i)),
                   pl.BlockSpec((SUBLANE, tm), lambda i: (0, i))],
        out_shape=[jax.ShapeDtypeStruct((T, D), F32),
                   jax.ShapeDtypeStruct((T * TOK_ROWS, LANE), F32),
                   jax.ShapeDtypeStruct((SUBLANE, T), jnp.int32),
                   jax.ShapeDtypeStruct((SUBLANE, T), F32)],
        compiler_params=_cparams(1),
        name="outproj_ln",
    )(*y_args, *ws, x, mod4, ln_g.reshape(1, D), ln_b.reshape(1, D), router_w.T, router_b.reshape(E, 1))


def _route(x, m_ref, w_ref, b_ref, h_ref, ids_ref, gts_ref, *, sh_row, sc_row):
    h = x * (1.0 + m_ref[sc_row:sc_row + 1, :]) + m_ref[sh_row:sh_row + 1, :]
    _to_tok_blocks(h_ref, h)
    logits = lax.dot_general(w_ref[...], h, NT_DIMS, precision=HI, preferred_element_type=F32) + b_ref[...]
    E, tm = logits.shape
    row = lax.broadcasted_iota(jnp.int32, (E, tm), 0)
    mx = jnp.max(logits, axis=0, keepdims=True)
    ex = jnp.exp(logits - mx)
    p = ex / jnp.sum(ex, axis=0, keepdims=True)
    grp = row // (E // N_GROUPS)
    best = None
    for g in range(N_GROUPS):
        pg = jnp.where(grp == g, p, -1.0)
        v1 = jnp.max(pg, axis=0, keepdims=True)
        i1 = jnp.min(jnp.where(pg == v1, row, E), axis=0, keepdims=True)
        pg2 = jnp.where(row == i1, -1.0, pg)
        v2 = jnp.max(pg2, axis=0, keepdims=True)
        i2 = jnp.min(jnp.where(pg2 == v2, row, E), axis=0, keepdims=True)
        score = v1 + v2
        if best is None:
            best = (score, v1, i1, v2, i2)
        else:
            take = score > best[0]
            best = tuple(jnp.where(take, n, o) for n, o in zip((score, v1, i1, v2, i2), best))
    _, v1, i1, v2, i2 = best
    tot = v1 + v2
    slot = lax.broadcasted_iota(jnp.int32, (SUBLANE, tm), 0)
    ids_ref[...] = jnp.where(slot == 0, i1, jnp.where(slot == 1, i2, 0))
    gts_ref[...] = jnp.where(slot == 0, v1 / tot, jnp.where(slot == 1, v2 / tot, 0.0))


def _to_tok_blocks(ref, x):
    tm = x.shape[0]
    for s in range(TOK_ROWS):
        ref[pl.ds(s, tm, stride=TOK_ROWS), :] = x[:, s * LANE:(s + 1) * LANE]


def _tok_chunk(ref, s, tm):
    return ref[pl.ds(s, tm, stride=TOK_ROWS), :]


class _RowGather:
    def __init__(self, src_hbm, idx_hbm, idx_s, bufs, isem, gsem, tm):
        self.src, self.idx_hbm, self.idx_s, self.bufs = src_hbm, idx_hbm, idx_s, bufs
        self.isem, self.gsem, self.tm = isem, gsem, tm

    def _idx_copy(self, tile, slot):
        return pltpu.make_async_copy(self.idx_hbm.at[tile], self.idx_s.at[slot], self.isem.at[slot])

    def _row_copy(self, slot, a, r):
        row = pl.multiple_of(self.idx_s[slot, a, r], TOK_ROWS)
        dst = pl.multiple_of(r * TOK_ROWS, TOK_ROWS)
        return pltpu.make_async_copy(self.src.at[pl.ds(row, TOK_ROWS), :],
                                     self.bufs[a][slot].at[pl.ds(dst, TOK_ROWS), :], self.gsem.at[slot, a])

    def rows_loop(self, slot, start):
        def body(r, carry):
            for a in range(len(self.bufs)):
                cp = self._row_copy(slot, a, r)
                cp.start() if start else cp.wait()
            return carry
        lax.fori_loop(0, self.tm, body, 0, unroll=8)

    def rows_start_inline(self, slot):
        for r in range(self.tm):
            for a in range(len(self.bufs)):
                self._row_copy(slot, a, r).start()

    def prologue(self, n_tiles):
        self._idx_copy(0, 0).start()
        self._idx_copy(0, 0).wait()
        self.rows_loop(0, True)
        if n_tiles > 1:
            self._idx_copy(1, 1).start()

    def advance_indices(self, i, n_tiles):
        slot = i % 2

        @pl.when(i + 1 < n_tiles)
        def _():
            self._idx_copy(i + 1, 1 - slot).wait()

        @pl.when(i + 2 < n_tiles)
        def _():
            self._idx_copy(i + 2, slot).start()


def _ffn_body(te_ref, va_ref, idx_hbm, h_hbm, wi_ref, wo_ref, o_ref, xbuf0, xbuf1, x16, a16, idx_s, isem, gsem,
              *, F, tm, n_tiles):
    i = pl.program_id(0)
    valid = va_ref[i] != 0
    first_invalid = jnp.logical_and(jnp.logical_not(valid), va_ref[jnp.maximum(i - 1, 0)] != 0)
    xbufs = (xbuf0, xbuf1)
    g = _RowGather(h_hbm, idx_hbm, idx_s, [xbufs], isem, gsem, tm)

    @pl.when(i == 0)
    def _():
        g.prologue(n_tiles)

    for slot in (0, 1):
        here = jnp.logical_and(valid, i % 2 == slot)

        @pl.when(here)
        def _(slot=slot):
            g.rows_loop(slot, False)
            g.advance_indices(i, n_tiles)

        @pl.when(here)
        def _(slot=slot):
            if slot in FFN_INLINE_START_SLOTS:
                g.rows_start_inline(1 - slot)
            else:
                g.rows_loop(1 - slot, True)
            for s in range(TOK_ROWS):
                x16[:, s * LANE:(s + 1) * LANE] = _tok_chunk(xbufs[slot], s, tm).astype(BF16)
            x = x16[...]
            fc = F // FFN_PARTS
            for p in range(FFN_PARTS):
                u = jnp.dot(x, wi_ref[:, p * fc:(p + 1) * fc], preferred_element_type=F32)
                w = jnp.dot(x, wi_ref[:, F + p * fc:F + (p + 1) * fc], preferred_element_type=F32)
                a16[:, p * fc:(p + 1) * fc] = (_silu(u) * w).astype(BF16)
            _to_tok_blocks(o_ref, jnp.dot(a16[...], wo_ref[...], preferred_element_type=F32))

        @pl.when(jnp.logical_and(first_invalid, i % 2 == slot))
        def _(slot=slot):
            g.rows_loop(slot, False)

            @pl.when(i + 1 < n_tiles)
            def _():
                g._idx_copy(i + 1, 1 - slot).wait()

    @pl.when(jnp.logical_not(valid))
    def _():
        o_ref[...] = jnp.zeros_like(o_ref)


def _expert_ffn(h, src_tok, w_in, w_out, layer, tile_e, tile_valid):
    n_tiles, _, tm = src_tok.shape
    F, D = w_out.shape[2:]
    return pl.pallas_call(
        functools.partial(_ffn_body, F=F, tm=tm, n_tiles=n_tiles),
        grid_spec=pltpu.PrefetchScalarGridSpec(
            num_scalar_prefetch=2,
            grid=(n_tiles,),
            in_specs=[pl.BlockSpec(memory_space=pl.ANY),
                      pl.BlockSpec(memory_space=pl.ANY),
                      pl.BlockSpec((None, None, D, 2 * F), lambda i, te, va: (layer, te[i], 0, 0)),
                      pl.BlockSpec((None, None, F, D), lambda i, te, va: (layer, te[i], 0, 0))],
            out_specs=pl.BlockSpec((tm * TOK_ROWS, LANE), lambda i, te, va: (i, 0)),
            scratch_shapes=[pltpu.VMEM((tm * TOK_ROWS, LANE), F32), pltpu.VMEM((tm * TOK_ROWS, LANE), F32),
                            pltpu.VMEM((tm, D), BF16), pltpu.VMEM((tm, F), BF16), pltpu.SMEM((2, 1, tm), jnp.int32),
                            pltpu.SemaphoreType.DMA((2,)), pltpu.SemaphoreType.DMA((2, 1))]),
        out_shape=jax.ShapeDtypeStruct((n_tiles * tm * TOK_ROWS, LANE), F32),
        compiler_params=_cparams(1),
        name="expert_ffn",
    )(tile_e, tile_valid, src_tok, h, w_in, w_out)


def _combine_body(*refs, g_row, alpha, tm, n_tiles, npt):
    x_ref, idx_hbm, y_hbm, gt_ref, m_ref, lg_ref, lb_ref = refs[:7]
    out_refs = refs[7:-7]
    y0a, y0b, y1a, y1b, idx_s, isem, gsem = refs[-7:]
    i = pl.program_id(0)
    y0, y1 = (y0a, y0b), (y1a, y1b)
    g = _RowGather(y_hbm, idx_hbm, idx_s, [y0, y1], isem, gsem, tm)

    @pl.when(i == 0)
    def _():
        g.prologue(n_tiles)

    for slot in (0, 1):
        @pl.when(i % 2 == slot)
        def _(slot=slot):
            g.rows_loop(slot, False)
            g.advance_indices(i, n_tiles)
            if n_tiles > 1:
                g.rows_start_inline(1 - slot)
            gt = gt_ref[...]
            g0, g1 = gt[:, 0:1], gt[:, 1:2]
            moe = jnp.concatenate([g0 * _tok_chunk(y0[slot], s, tm) + g1 * _tok_chunk(y1[slot], s, tm)
                                   for s in range(TOK_ROWS)], axis=1)
            z = alpha * x_ref[...] + m_ref[g_row:g_row + 1, :] * moe
            res = _layer_norm(z, lg_ref[...], lb_ref[...])
            if npt is None:
                out_refs[0][...] = res
            else:
                @pl.when(i < npt)
                def _():
                    out_refs[0][...] = res

                @pl.when(i >= npt)
                def _():
                    out_refs[1][...] = res

            if n_tiles > 1:
                @pl.when(i == n_tiles - 1)
                def _():
                    g.rows_loop(1 - slot, False)


def _combine_ln(x, ys, dest, gts, modt, ln_g, ln_b, *, g_row, alpha, split_rows=None):
    T, D = x.shape
    n_tiles, _, tm = dest.shape
    mod4, mod_map = modt[0], modt[1](tm)
    row = pl.BlockSpec((tm, D), lambda i: (i, 0))
    vec = pl.BlockSpec((1, D), lambda i: (0, 0))
    hbm = pl.BlockSpec(memory_space=pl.ANY)
    if split_rows is None:
        npt, out_specs, out_shape = None, row, jax.ShapeDtypeStruct((T, D), F32)
    else:
        npt = split_rows // tm
        out_specs = [pl.BlockSpec((tm, D), lambda i: (jnp.minimum(i, npt - 1), 0)),
                     pl.BlockSpec((tm, D), lambda i: (jnp.maximum(i - npt, 0), 0))]
        out_shape = [jax.ShapeDtypeStruct((split_rows, D), F32), jax.ShapeDtypeStruct((T - split_rows, D), F32)]
    return pl.pallas_call(
        functools.partial(_combine_body, g_row=g_row, alpha=alpha, tm=tm, n_tiles=n_tiles, npt=npt),
        grid=(n_tiles,),
        in_specs=[row, hbm, hbm, pl.BlockSpec((tm, LANE), lambda i: (i, 0)),
                  pl.BlockSpec((None, None, 6, D), lambda i: mod_map(i)), vec, vec],
        out_specs=out_specs,
        out_shape=out_shape,
        scratch_shapes=[pltpu.VMEM((tm * TOK_ROWS, LANE), F32)] * 4 + [
                        pltpu.SMEM((2, 2, tm), jnp.int32),
                        pltpu.SemaphoreType.DMA((2,)), pltpu.SemaphoreType.DMA((2, 2))],
        compiler_params=_cparams(1),
        name="combine_ln",
    )(x, dest, ys, gts, mod4, ln_g.reshape(1, D), ln_b.reshape(1, D))


def _route_meta(ids, tm, tm_tok):
    T = ids.shape[0]
    E = N_EXPERTS
    e_flat = ids.reshape(-1)
    onehot = (e_flat[:, None] == jnp.arange(E, dtype=jnp.int32)[None, :]).astype(jnp.int32)
    csum = jnp.cumsum(onehot, axis=0)
    rank = jnp.sum(csum * onehot, axis=1) - 1
    counts = csum[-1]
    padded = ((counts + tm - 1) // tm) * tm
    ends = jnp.cumsum(padded)
    starts = ends - padded
    dest = jnp.sum(starts[None, :] * onehot, axis=1) + rank
    n_rows = 2 * T + E * tm
    src_tok = jnp.zeros((n_rows,), jnp.int32).at[dest].set(jnp.arange(2 * T, dtype=jnp.int32) // 2,
                                                            unique_indices=True, mode="promise_in_bounds")
    tile_start = jnp.arange(n_rows // tm, dtype=jnp.int32) * tm
    valid = (tile_start < ends[-1]).astype(jnp.int32)
    probe = jnp.minimum(tile_start, ends[-1] - 1)
    tile_e = jnp.sum((ends[None, :] <= probe[:, None]).astype(jnp.int32), axis=1)
    dest_t = dest.reshape(T // tm_tok, tm_tok, 2).transpose(0, 2, 1)
    return (dest_t * TOK_ROWS, src_tok.reshape(n_rows // tm, 1, tm) * TOK_ROWS, jnp.minimum(tile_e, E - 1), valid)


def _pos_embed_2d(n_tokens, dim):
    rows = n_tokens // GRID_W
    quarter = dim // 4
    freqs = jnp.exp(-math.log(10000.0) * jnp.arange(quarter, dtype=F32) / quarter)
    r = jnp.broadcast_to(jnp.arange(rows, dtype=F32)[:, None], (rows, GRID_W)).reshape(-1)
    col = jnp.broadcast_to(jnp.arange(GRID_W, dtype=F32)[None, :], (rows, GRID_W)).reshape(-1)
    ar = r[:, None] * freqs
    ac = col[:, None] * freqs
    return jnp.concatenate([jnp.sin(ar), jnp.cos(ar), jnp.sin(ac), jnp.cos(ac)], -1)


def _pad_cols(w, n):
    return jnp.pad(w, ((0, 0),) * (w.ndim - 1) + ((0, n - w.shape[-1]),))


def kernel(x_prompt, x_sample, state_lru, state_mlstm_c, state_mlstm_n, state_mlstm_m, state_gla, c, c_ctx, mod_w, mod_b, ln_g, ln_b, even_w_in, even_w_out, lru_conv_w, lru_conv_b, lru_gate_w, lru_gate_b, lru_lambda, mlstm_gate_b, mlstm_norm_g, odd_w_in, odd_w_out, gla_gate_w, gla_gate_b, gla_norm_g, router_w, router_b, moe_w_in, moe_w_out):
    Bp, Lp, D = x_prompt.shape
    Bs, Ls, _ = x_sample.shape
    depth = mod_w.shape[0]
    Tp, Ts = Bp * Lp, Bs * Ls
    T = Tp + Ts
    alpha = (2 * depth) ** 0.25
    D_A = lru_conv_w.shape[-1]
    D_B = mlstm_norm_g.shape[-1]
    DK = gla_gate_w.shape[-1]
    DV = gla_norm_g.shape[-1]
    tm = next(t for t in (512, 256, 128) if Tp % t == 0 and Ls % t == 0)
    tn = 1024

    n_cond = 1 + Bs
    R = -(-n_cond // SUBLANE) * SUBLANE
    cond = jnp.zeros((R, D), F32).at[0].set(c_ctx).at[1:n_cond].set(c)
    mod_all = _modulation(cond, mod_w, mod_b)
    mod4 = mod_all.reshape(depth, R, 6, D)
    tm_proj = next(t for t in (1024, 512, 256, 128) if Tp % t == 0 and Ls % t == 0)
    assert D == TOK_ROWS * LANE

    x = _embed(x_prompt.reshape(Tp, D), x_sample.reshape(Ts, D), _pos_embed_2d(Ls, D), tm)

    moe_w_in_b = moe_w_in.astype(BF16)
    moe_w_out_b = moe_w_out.astype(BF16)
    groups = ((0, Bp, Lp), (Tp, Bs, Ls))

    s_lru, s_c, s_n, s_m, s_gla = [], [], [], [], []
    for l in range(depth):
        j = l // 2
        mt = (mod4, lambda t, l=l: (lambda i: (l, jnp.where(i < Tp // t, 0, 1 + (i - Tp // t) // (Ls // t)), 0, 0)))
        if l % 2 == 0:
            w_in = even_w_in[j]
            n_main = 2 * D_A + 4 * D_B
            proj, gates = _proj(x, mt, w_in[:, :n_main].astype(BF16), _pad_cols(w_in[:, n_main:], LANE),
                                sh_row=0, sc_row=1, tm=tm_proj, tn=tn)
            gate_bias = _pad_cols(mlstm_gate_b[j].reshape(1, -1), LANE)
            ya, hs = [], []
            for gi, (row0, nseq, L) in enumerate(groups):
                if gi == 0:
                    lru0 = jnp.zeros((nseq, 2, D_A), F32)
                    c0 = jnp.zeros((nseq, 2, B_HEADS, D_B // B_HEADS, D_B // B_HEADS), F32)
                    n0 = jnp.zeros((nseq, 2, B_HEADS, D_B // B_HEADS), F32)
                    m0 = jnp.zeros((nseq, 2, B_HEADS), F32)
                else:
                    lru0, c0, n0, m0 = state_lru[:, j], state_mlstm_c[:, j], state_mlstm_n[:, j], state_mlstm_m[:, j]
                y_a, st = _lru(proj, row0, nseq, L, lru_conv_w[j], lru_conv_b[j], lru_gate_w[j], lru_gate_b[j],
                               lru_lambda[j], lru0)
                h, C, n, m = _mlstm(proj, gates, gate_bias, row0, nseq, L, 2 * D_A, c0, n0, m0)
                ya.append(y_a)
                hs.append(h)
                if gi == 0:
                    s_lru.append(st)
                    s_c.append(C)
                    s_n.append(n.reshape(nseq, 2, B_HEADS, -1))
                    s_m.append(m.reshape(nseq, 2, B_HEADS))
            h_fwd, h_bwd = zip(*hs)
            y_b = _mlstm_out(h_fwd, h_bwd, proj, 2 * D_A + 3 * D_B, mlstm_norm_g[j], tm)
            w_out = even_w_out[j].astype(BF16)
            x, hf, ids, gts = _outproj_ln([tuple(ya), y_b], [w_out[:D_A], w_out[D_A:]], x, mt, ln_g[l, 0],
                                          ln_b[l, 0], router_w, router_b, g_row=2, alpha=alpha, tm=tm)
        else:
            w_in = odd_w_in[j]
            n_main = 2 * DK + 2 * DV
            proj, lr = _proj(x, mt, w_in[:, :n_main].astype(BF16), _pad_cols(w_in[:, n_main:], LANE),
                             sh_row=0, sc_row=1, tm=tm_proj, tn=tn)
            gw_pad = jnp.zeros((2, LANE, DK), F32)
            for dd in range(2):
                gw_pad = gw_pad.at[dd, dd * GLA_RANK:(dd + 1) * GLA_RANK].set(gla_gate_w[j, dd])
            os_ = []
            for gi, (row0, nseq, L) in enumerate(groups):
                s0 = (jnp.zeros((nseq, 2, C_HEADS, DK // C_HEADS, DV // C_HEADS), F32) if gi == 0
                      else state_gla[:, j])
                o, S = _gla(proj, lr, gw_pad, gla_gate_b[j], row0, nseq, L, s0)
                os_.append(o)
                if gi == 0:
                    s_gla.append(S)
            o_fwd, o_bwd = zip(*os_)
            y = _gla_out(o_fwd, o_bwd, proj, 2 * DK + DV, gla_norm_g[j], tm)
            x, hf, ids, gts = _outproj_ln([y], [odd_w_out[j].astype(BF16)], x, mt, ln_g[l, 0], ln_b[l, 0],
                                          router_w, router_b, g_row=2, alpha=alpha, tm=tm)

        gts = _pad_cols(gts[:2].T, LANE)
        dest, src_tok, tile_e, tile_valid = _route_meta(ids[:2].T, MOE_TM, tm)
        ys = _expert_ffn(hf, src_tok, moe_w_in_b, moe_w_out_b, l, tile_e, tile_valid)
        x = _combine_ln(x, ys, dest, gts, mt, ln_g[l, 1], ln_b[l, 1], g_row=5, alpha=alpha,
                        split_rows=Tp if l == depth - 1 else None)

    y_prompt = x[0].reshape(Bp, Lp, D)
    y_sample = x[1].reshape(Bs, Ls, D)
    return (y_prompt, y_sample, jnp.stack(s_lru, 1), jnp.stack(s_c, 1), jnp.stack(s_n, 1),
            jnp.stack(s_m, 1), jnp.stack(s_gla, 1))
```

```python
import functools
import math

import jax
import jax.numpy as jnp
from jax import lax
from jax.experimental import pallas as pl
from jax.experimental.pallas import tpu as pltpu

F32 = jnp.float32
BF16 = jnp.bfloat16
HI = lax.Precision.HIGHEST

LN_EPS = 1e-5
LRU_C = 8.0
A_BLOCKS = 8
B_HEADS = 4
C_HEADS = 4
GLA_RANK = 16
GLA_TAU = 16.0
N_EXPERTS = 16
N_GROUPS = 4
GRID_W = 64
LANE = 128
SUBLANE = 8
SCAN_CHUNK = 256
SCAN_OUT_DTYPE = jnp.bfloat16
MOE_TM = 512
FFN_PARTS = 4
FFN_INLINE_START_SLOTS = (0,)
TOK_ROWS = 16
VMEM_LIMIT = 56 * 1024 * 1024

NT_DIMS = (((1,), (1,)), ((), ()))
TN_DIMS = (((0,), (0,)), ((), ()))


def _cparams(n_axes):
    return pltpu.CompilerParams(dimension_semantics=("arbitrary",) * n_axes,
                                vmem_limit_bytes=VMEM_LIMIT)


def _sigmoid(x):
    return 1.0 / (1.0 + jnp.exp(-x))


def _silu(x):
    return x * _sigmoid(x)


def _log_sigmoid(x):
    return jnp.minimum(x, 0.0) - jnp.log1p(jnp.exp(-jnp.abs(x)))


def _gelu_tanh(x):
    return 0.5 * x * (1.0 + jnp.tanh(math.sqrt(2.0 / math.pi) * (x + 0.044715 * (x * x * x))))


def _mod_body(c_ref, w_ref, b_ref, o_ref):
    c = c_ref[...]
    o_ref[...] = jnp.dot(_silu(c), w_ref[...], precision=HI, preferred_element_type=F32) + b_ref[...]


def _modulation(cond, mod_w, mod_b):
    R, D = cond.shape
    nl, _, N = mod_w.shape
    tn = 1024
    return pl.pallas_call(
        _mod_body,
        grid=(nl, N // tn),
        in_specs=[pl.BlockSpec((R, D), lambda l, j: (0, 0)),
                  pl.BlockSpec((None, D, tn), lambda l, j: (l, 0, j)),
                  pl.BlockSpec((None, 1, tn), lambda l, j: (l, 0, j))],
        out_specs=pl.BlockSpec((None, R, tn), lambda l, j: (l, 0, j)),
        out_shape=jax.ShapeDtypeStruct((nl, R, N), F32),
        compiler_params=_cparams(2),
        name="modulation",
    )(cond, mod_w, mod_b.reshape(nl, 1, N))


def _embed_body(xp_ref, xs_ref, pos_ref, o_ref, *, n_prompt_tiles):
    i = pl.program_id(0)

    @pl.when(i < n_prompt_tiles)
    def _():
        o_ref[...] = xp_ref[...]

    @pl.when(i >= n_prompt_tiles)
    def _():
        o_ref[...] = xs_ref[...] + pos_ref[...]


def _embed(xp, xs, pos, tm):
    Tp, D = xp.shape
    Ts = xs.shape[0]
    Ls = pos.shape[0]
    npt, nst, npos = Tp // tm, Ts // tm, Ls // tm
    return pl.pallas_call(
        functools.partial(_embed_body, n_prompt_tiles=npt),
        grid=(npt + nst,),
        in_specs=[pl.BlockSpec((tm, D), lambda i: (jnp.minimum(i, npt - 1), 0)),
                  pl.BlockSpec((tm, D), lambda i: (jnp.maximum(i - npt, 0), 0)),
                  pl.BlockSpec((tm, D), lambda i: (jnp.maximum(i - npt, 0) % npos, 0))],
        out_specs=pl.BlockSpec((tm, D), lambda i: (i, 0)),
        out_shape=jax.ShapeDtypeStruct((Tp + Ts, D), F32),
        compiler_params=_cparams(1),
        name="embed",
    )(xp, xs, pos)


def _split3(x):
    hi = x.astype(BF16)
    r1 = x - hi.astype(F32)
    mid = r1.astype(BF16)
    return hi, mid, (r1 - mid.astype(F32)).astype(BF16)


def _proj_body(x_ref, m_ref, w_ref, wg_ref, o_ref, og_ref, h_ref, *, sh_row, sc_row):
    @pl.when(pl.program_id(1) == 0)
    def _():
        h = x_ref[...] * (1.0 + m_ref[sc_row:sc_row + 1, :]) + m_ref[sh_row:sh_row + 1, :]
        h_hi, h_mid, _ = _split3(h)
        h_ref[...] = h_hi
        wg_hi, wg_mid, _ = _split3(wg_ref[...])
        og_ref[...] = (jnp.dot(h_hi, wg_hi, preferred_element_type=F32)
                       + jnp.dot(h_hi, wg_mid, preferred_element_type=F32)
                       + jnp.dot(h_mid, wg_hi, preferred_element_type=F32))

    o_ref[...] = jnp.dot(h_ref[...], w_ref[...], preferred_element_type=F32)


def _proj(x, modt, w, w_gate, *, sh_row, sc_row, tm, tn):
    T, D = x.shape
    N = w.shape[1]
    mod4, mod_map = modt[0], modt[1](tm)
    return pl.pallas_call(
        functools.partial(_proj_body, sh_row=sh_row, sc_row=sc_row),
        grid=(T // tm, N // tn),
        in_specs=[pl.BlockSpec((tm, D), lambda i, j: (i, 0)),
                  pl.BlockSpec((None, None, 6, D), lambda i, j: mod_map(i)),
                  pl.BlockSpec((D, tn), lambda i, j: (0, j)),
                  pl.BlockSpec((D, LANE), lambda i, j: (0, 0))],
        out_specs=[pl.BlockSpec((tm, tn), lambda i, j: (i, j)),
                   pl.BlockSpec((tm, LANE), lambda i, j: (i, 0))],
        out_shape=[jax.ShapeDtypeStruct((T, N), F32), jax.ShapeDtypeStruct((T, LANE), F32)],
        scratch_shapes=[pltpu.VMEM((tm, D), BF16)],
        compiler_params=_cparams(2),
        name="proj",
    )(x, mod4, w, w_gate)


def _lru_body(ag_ref, ax_ref, cw_ref, cb_ref, gw_ref, gb_ref, lam_ref, h0_ref, y_ref, st_ref,
              af_s, uf_s, ab_s, ub_s, hf_s, hb_s, *, L):
    x = ax_ref[...]
    row = lax.broadcasted_iota(jnp.int32, (L, LANE), 0)
    xm1 = jnp.where(row >= 1, pltpu.roll(x, 1, 0), 0.0)
    xp1 = jnp.where(row < L - 1, pltpu.roll(x, L - 1, 0), 0.0)
    xp2 = jnp.where(row < L - 2, pltpu.roll(x, L - 2, 0), 0.0)
    xc = cb_ref[...] + xm1 * cw_ref[0:1, :] + x * cw_ref[1:2, :] + xp1 * cw_ref[2:3, :] + xp2 * cw_ref[3:4, :]
    xcb = xc.astype(BF16)
    nj = L // SUBLANE
    sub = lax.broadcasted_iota(jnp.int32, (nj, SUBLANE, LANE), 1)

    for d, (a_s, u_s) in enumerate(((af_s, uf_s), (ab_s, ub_s))):
        g = jnp.dot(xcb, gw_ref[d], preferred_element_type=F32) + gb_ref[d]
        r = _sigmoid(g[:, :LANE])
        ig = _sigmoid(g[:, LANE:])
        nlam = -lam_ref[d]
        softplus = jnp.maximum(nlam, 0.0) + jnp.log1p(jnp.exp(-jnp.abs(nlam)))
        log_a = (-LRU_C * softplus) * r
        a = jnp.exp(log_a)
        u = jnp.sqrt(jnp.tanh(-log_a) * (1.0 + a * a)) * ig * xc
        a3 = a.reshape(nj, SUBLANE, LANE)
        u3 = u.reshape(nj, SUBLANE, LANE)
        for k in (1, 2, 4):
            if d == 0:
                sh, keep = k, sub >= k
            else:
                sh, keep = SUBLANE - k, sub < SUBLANE - k
            a_sh = pltpu.roll(a3, sh, 1)
            u_sh = pltpu.roll(u3, sh, 1)
            u3 = jnp.where(keep, a3 * u_sh + u3, u3)
            a3 = jnp.where(keep, a3 * a_sh, a3)
        a_s[...] = a3.reshape(L, LANE)
        u_s[...] = u3.reshape(L, LANE)

    def carry(j, hs):
        hf, hb = hs
        rf = pl.multiple_of(j * SUBLANE, SUBLANE)
        rb = pl.multiple_of((nj - 1 - j) * SUBLANE, SUBLANE)
        of = af_s[pl.ds(rf, SUBLANE), :] * hf + uf_s[pl.ds(rf, SUBLANE), :]
        ob = ab_s[pl.ds(rb, SUBLANE), :] * hb + ub_s[pl.ds(rb, SUBLANE), :]
        hf_s[pl.ds(rf, SUBLANE), :] = of
        hb_s[pl.ds(rb, SUBLANE), :] = ob
        return (jnp.broadcast_to(of[SUBLANE - 1:SUBLANE, :], (SUBLANE, LANE)),
                jnp.broadcast_to(ob[0:1, :], (SUBLANE, LANE)))

    h0f = jnp.broadcast_to(h0_ref[0:1, :], (SUBLANE, LANE))
    h0b = jnp.broadcast_to(h0_ref[1:2, :], (SUBLANE, LANE))
    lax.fori_loop(0, nj, carry, (h0f, h0b))
    y_ref[...] = (_gelu_tanh(ag_ref[...]) * (hf_s[...] + hb_s[...])).astype(y_ref.dtype)
    st_ref[0:1, :] = hf_s[L - 1:L, :]
    st_ref[1:2, :] = hb_s[0:1, :]


def _lru(proj, row0, nseq, L, conv_w, conv_b, gate_w, gate_b, lam, h0):
    D_A = conv_w.shape[1]
    nb = D_A // LANE
    rb0 = row0 // L
    in_specs = [pl.BlockSpec((L, LANE), lambda s, h: (rb0 + s, h)),
                pl.BlockSpec((L, LANE), lambda s, h: (rb0 + s, nb + h)),
                pl.BlockSpec((4, LANE), lambda s, h: (0, h)),
                pl.BlockSpec((1, LANE), lambda s, h: (0, h)),
                pl.BlockSpec((2, None, LANE, 2 * LANE), lambda s, h: (0, h, 0, 0)),
                pl.BlockSpec((2, None, 1, 2 * LANE), lambda s, h: (0, h, 0, 0)),
                pl.BlockSpec((2, None, 1, LANE), lambda s, h: (0, h, 0, 0)),
                pl.BlockSpec((None, 2, LANE), lambda s, h: (s, 0, h))]
    args = [proj, proj, conv_w, conv_b.reshape(1, D_A), gate_w.astype(BF16),
            gate_b.reshape(2, nb, 1, 2 * LANE), lam.reshape(2, nb, 1, LANE), h0]
    return pl.pallas_call(
        functools.partial(_lru_body, L=L),
        grid=(nseq, nb),
        in_specs=in_specs,
        out_specs=[pl.BlockSpec((L, LANE), lambda s, h: (s, h)),
                   pl.BlockSpec((None, 2, LANE), lambda s, h: (s, 0, h))],
        out_shape=[jax.ShapeDtypeStruct((nseq * L, D_A), BF16),
                   jax.ShapeDtypeStruct((nseq, 2, D_A), F32)],
        scratch_shapes=[pltpu.VMEM((L, LANE), F32)] * 6,
        compiler_params=_cparams(2),
        name="rglru",
    )(*args)


def _tri_mask(Tc, d):
    r = lax.broadcasted_iota(jnp.int32, (Tc, Tc), 0)
    c = lax.broadcasted_iota(jnp.int32, (Tc, Tc), 1)
    return (c - r) * (1 - 2 * d) <= 0


def _mlstm_dir(d, q_ref, k_ref, v_ref, g_ref, gb_ref, h_ref, C_s, n_s, m_s, *, Tc, k_scale, dh):
    H = B_HEADS
    last = Tc - 1 if d == 0 else 0
    lane = lax.broadcasted_iota(jnp.int32, (Tc, LANE), 1)
    G = g_ref[...] + gb_ref[...]
    tri = _tri_mask(Tc, d)
    Bm = jnp.dot(tri.astype(F32), _log_sigmoid(G), precision=HI, preferred_element_type=F32)
    col0 = d * (2 * H)
    X = jnp.where(lane >= col0 + H, Bm, G)
    srow = lax.broadcasted_iota(jnp.int32, (2 * H, LANE), 0)
    slane = lax.broadcasted_iota(jnp.int32, (2 * H, LANE), 1)
    sel = (slane == col0 + srow).astype(F32)
    R = lax.dot_general(sel, X, NT_DIMS, precision=HI, preferred_element_type=F32)
    for hd in range(H):
        i_col = jnp.sum(jnp.where(lane == col0 + hd, X, 0.0), axis=1, keepdims=True)
        b_col = jnp.sum(jnp.where(lane == col0 + H + hd, X, 0.0), axis=1, keepdims=True)
        i_row = R[hd:hd + 1, :]
        b_row = R[H + hd:H + hd + 1, :]
        dmat = jnp.where(tri, b_col - b_row + i_row, -jnp.inf)
        m_prev = m_s[d, hd]
        m_inter = b_col + m_prev
        m_t = jnp.maximum(m_inter, jnp.max(dmat, axis=1, keepdims=True))
        cs = slice(hd * dh, (hd + 1) * dh)
        qf = q_ref[:, cs]
        qb = qf.astype(BF16)
        kf = k_ref[:, cs] * k_scale
        kb = kf.astype(BF16)
        vb = v_ref[:, cs].astype(BF16)
        S = lax.dot_general(qb, kb, NT_DIMS, preferred_element_type=F32) * jnp.exp(dmat - m_t)
        inter_scale = jnp.exp(m_inter - m_t)
        Cm = C_s[d, hd]
        num =(jnp.dot(S.astype(BF16), vb, preferred_element_type=F32)
               + inter_scale * jnp.dot(qb, Cm.astype(BF16), preferred_element_type=F32))
        nv = n_s[d, hd]
        qn = jnp.sum(qf * nv, axis=1, keepdims=True)
        den = jnp.sum(S, axis=1, keepdims=True) + inter_scale * qn
        h_ref[:, cs] = (num / jnp.maximum(jnp.abs(den), jnp.exp(-m_t))).astype(h_ref.dtype)
        bL = b_col[last:last + 1, :]
        g_col = bL - b_col + i_col
        m_new = jnp.maximum(bL + m_prev, jnp.max(g_col, axis=0, keepdims=True))
        wk = jnp.exp(g_col - m_new)
        decay = jnp.exp(bL + m_prev - m_new)
        kw = kf * wk
        C_s[d, hd] = decay * Cm + lax.dot_general(kw.astype(BF16), vb, TN_DIMS, preferred_element_type=F32)
        n_s[d, hd] = decay * nv + jnp.sum(kw, axis=0, keepdims=True)
        m_s[d, hd] = m_new


def _mlstm_body(qf_ref, kf_ref, vf_ref, gf_ref, qb_ref, kb_ref, vb_ref, gbk_ref, gb_ref, c0_ref, n0_ref, m0_ref,
                hf_ref, hb_ref, co_ref, no_ref, mo_ref, C_s, n_s, m_s, *, nc, **kw):
    c = pl.program_id(1)

    @pl.when(c == 0)
    def _():
        C_s[...] = c0_ref[...]
        n_s[...] = n0_ref[...]
        m_s[...] = m0_ref[...]

    _mlstm_dir(0, qf_ref, kf_ref, vf_ref, gf_ref, gb_ref, hf_ref, C_s, n_s, m_s, **kw)
    _mlstm_dir(1, qb_ref, kb_ref, vb_ref, gbk_ref, gb_ref, hb_ref, C_s, n_s, m_s, **kw)

    @pl.when(c == nc - 1)
    def _():
        co_ref[...] = C_s[...]
        no_ref[...] = n_s[...]
        mo_ref[...] = m_s[...]


def _mlstm(proj, gates, gate_bias, row0, nseq, L, col0, c0, n0, m0):
    H = B_HEADS
    dh = c0.shape[-1]
    D_B = H * dh
    Tc = min(SCAN_CHUNK, L)
    nc = L // Tc
    rb0 = row0 // Tc
    cb0 = col0 // D_B
    fwd = lambda s, c: s * nc + c
    bwd = lambda s, c: s * nc + nc - 1 - c

    def chunk_specs(rb):
        return [pl.BlockSpec((Tc, D_B), lambda s, c, off=off: (rb0 + rb(s, c), cb0 + off)) for off in range(3)] + [
            pl.BlockSpec((Tc, LANE), lambda s, c: (rb0 + rb(s, c), 0))]

    def st_specs():
        st_map = lambda s, c: (s, 0, 0, 0, 0)
        return [pl.BlockSpec((None, 2, H, dh, dh), st_map), pl.BlockSpec((None, 2, H, 1, dh), st_map),
                pl.BlockSpec((None, 2, H, 1, 1), st_map)]

    hf, hb, C, n, m = pl.pallas_call(
        functools.partial(_mlstm_body, Tc=Tc, nc=nc, k_scale=dh ** -0.5, dh=dh),
        grid=(nseq, nc),
        in_specs=chunk_specs(fwd) + chunk_specs(bwd) + [pl.BlockSpec((1, LANE), lambda s, c: (0, 0))] + st_specs(),
        out_specs=[pl.BlockSpec((Tc, D_B), lambda s, c: (fwd(s, c), 0)),
                   pl.BlockSpec((Tc, D_B), lambda s, c: (bwd(s, c), 0))] + st_specs(),
        out_shape=[jax.ShapeDtypeStruct((nseq * L, D_B), SCAN_OUT_DTYPE),
                   jax.ShapeDtypeStruct((nseq * L, D_B), SCAN_OUT_DTYPE),
                   jax.ShapeDtypeStruct((nseq, 2, H, dh, dh), F32),
                   jax.ShapeDtypeStruct((nseq, 2, H, 1, dh), F32),
                   jax.ShapeDtypeStruct((nseq, 2, H, 1, 1), F32)],
        scratch_shapes=[pltpu.VMEM((2, H, dh, dh), F32), pltpu.VMEM((2, H, 1, dh), F32),
                        pltpu.VMEM((2, H, 1, 1), F32)],
        compiler_params=_cparams(2),
        name="mlstm",
    )(*([proj] * 3 + [gates]) * 2, gate_bias, c0, n0.reshape(nseq, 2, H, 1, dh), m0.reshape(nseq, 2, H, 1, 1))
    return (hf, hb), C, n, m


def _pair_specs(pair, block, tm):
    npt = pair[0].shape[-2] // tm
    (shape, _), mk = block(0), block
    return npt, [pl.BlockSpec(shape, lambda i, *rest: mk(jnp.minimum(i, npt - 1), *rest)[1]),
                 pl.BlockSpec(shape, lambda i, *rest: mk(jnp.maximum(i - npt, 0), *rest)[1])]


def _pick(p_ref, s_ref, npt):
    return jnp.where(pl.program_id(0) < npt, p_ref[...], s_ref[...])


def _mlstm_out_body(hfp_ref, hfs_ref, hbp_ref, hbs_ref, o_ref, g_ref, y_ref, *, npt):
    hm = _pick(hfp_ref, hfs_ref, npt).astype(F32) + _pick(hbp_ref, hbs_ref, npt).astype(F32)
    mu = jnp.mean(hm, axis=1, keepdims=True)
    xc = hm - mu
    var = jnp.mean(xc * xc, axis=1, keepdims=True)
    y_ref[...] = (_sigmoid(o_ref[...]) * (xc * lax.rsqrt(var + LN_EPS) * g_ref[...])).astype(y_ref.dtype)


def _mlstm_out(h_fwd, h_bwd, proj, col0, norm_g, tm):
    T = proj.shape[0]
    D_B = h_fwd[0].shape[-1]
    dh = D_B // B_HEADS
    cb0 = col0 // dh
    blk = lambda r, h=0: ((tm, dh), (r, h))
    npt, f_specs = _pair_specs(h_fwd, blk, tm)
    _, b_specs = _pair_specs(h_bwd, blk, tm)
    return pl.pallas_call(
        functools.partial(_mlstm_out_body, npt=npt),
        grid=(T // tm, B_HEADS),
        in_specs=f_specs + b_specs + [pl.BlockSpec((tm, dh), lambda i, h: (i, cb0 + h)),
                                      pl.BlockSpec((1, dh), lambda i, h: (0, h))],
        out_specs=pl.BlockSpec((tm, dh), lambda i, h: (i, h)),
        out_shape=jax.ShapeDtypeStruct((T, D_B), BF16),
        compiler_params=_cparams(2),
        name="mlstm_out",
    )(*h_fwd, *h_bwd, proj, norm_g.reshape(1, D_B))


def _gla_dir(d, q_ref, k_ref, v_ref, lr_ref, gw_ref, gb_ref, o_ref, ST_s, *, Tc, q_scale, dk, dv):
    H = C_HEADS
    last = Tc - 1 if d == 0 else 0
    lr_hi, lr_mid, _ = _split3(lr_ref[...])
    gw_hi, gw_mid, _ = _split3(gw_ref[d])
    z = (jnp.dot(lr_hi, gw_hi, preferred_element_type=F32)
         + jnp.dot(lr_hi, gw_mid, preferred_element_type=F32)
         + jnp.dot(lr_mid, gw_hi, preferred_element_type=F32)) + gb_ref[d]
    loga = _log_sigmoid(z) * (1.0 / GLA_TAU)
    tri = _tri_mask(Tc, d)
    trib = tri.astype(F32).astype(BF16)
    b = sum(jnp.dot(trib, piece, preferred_element_type=F32) for piece in _split3(loga))
    qs = (q_ref[...] * q_scale * jnp.exp(b)).astype(BF16)
    kf = k_ref[...]
    ke = (kf * jnp.exp(-b)).astype(BF16)
    bL = b[last:last + 1, :]
    kd = (kf * jnp.exp(bL - b)).astype(BF16)
    ebL = jnp.exp(bL)
    for hd in range(H):
        ks = slice(hd * dk, (hd + 1) * dk)
        vs = slice(hd * dv, (hd + 1) * dv)
        vb = v_ref[:, vs].astype(BF16)
        att = jnp.where(tri, lax.dot_general(qs[:, ks], ke[:, ks], NT_DIMS, preferred_element_type=F32), 0.0)
        ST = ST_s[d, hd]
        inter = lax.dot_general(qs[:, ks], ST.astype(BF16), NT_DIMS, preferred_element_type=F32)
        o_ref[:, vs] = (inter + jnp.dot(att.astype(BF16), vb, preferred_element_type=F32)).astype(o_ref.dtype)
        ST_s[d, hd] = ST * ebL[:, ks] + lax.dot_general(vb, kd[:, ks], TN_DIMS, preferred_element_type=F32)


def _gla_body(qf_ref, kf_ref, vf_ref, lrf_ref, qb_ref, kb_ref, vb_ref, lrb_ref, gw_ref, gb_ref, s0_ref,
              of_ref, ob_ref, so_ref, ST_s, *, nc, **kw):
    c = pl.program_id(1)
    states = [(d, hd) for d in range(2) for hd in range(C_HEADS)]

    @pl.when(c == 0)
    def _():
        for d, hd in states:
            ST_s[d, hd] = s0_ref[d, hd].T

    _gla_dir(0, qf_ref, kf_ref, vf_ref, lrf_ref, gw_ref, gb_ref, of_ref, ST_s, **kw)
    _gla_dir(1, qb_ref, kb_ref, vb_ref, lrb_ref, gw_ref, gb_ref, ob_ref, ST_s, **kw)

    @pl.when(c == nc - 1)
    def _():
        for d, hd in states:
            so_ref[d, hd] = ST_s[d, hd].T


def _gla(proj, lr, gw_pad, gate_b, row0, nseq, L, s0):
    H = C_HEADS
    dk, dv = s0.shape[-2:]
    DK, DV = H * dk, H * dv
    Tc = min(SCAN_CHUNK, L)
    nc = L // Tc
    rb0 = row0 // Tc
    fwd = lambda s, c: s * nc + c
    bwd = lambda s, c: s * nc + nc - 1 - c

    def chunk_specs(rb):
        return [pl.BlockSpec((Tc, DK), lambda s, c: (rb0 + rb(s, c), 0)),
                pl.BlockSpec((Tc, DK), lambda s, c: (rb0 + rb(s, c), 1)),
                pl.BlockSpec((Tc, DV), lambda s, c: (rb0 + rb(s, c), (2 * DK) // DV)),
                pl.BlockSpec((Tc, LANE), lambda s, c: (rb0 + rb(s, c), 0))]

    st_map = lambda s, c: (s, 0, 0, 0, 0)
    of, ob, S = pl.pallas_call(
        functools.partial(_gla_body, Tc=Tc, nc=nc, q_scale=dk ** -0.5, dk=dk, dv=dv),
        grid=(nseq, nc),
        in_specs=chunk_specs(fwd) + chunk_specs(bwd) + [
            pl.BlockSpec((2, LANE, DK), lambda s, c: (0, 0, 0)),
            pl.BlockSpec((2, 1, DK), lambda s, c: (0, 0, 0)),
            pl.BlockSpec((None, 2, H, dk, dv), st_map)],
        out_specs=[pl.BlockSpec((Tc, DV), lambda s, c: (fwd(s, c), 0)),
                   pl.BlockSpec((Tc, DV), lambda s, c: (bwd(s, c), 0)),
                   pl.BlockSpec((None, 2, H, dk, dv), st_map)],
        out_shape=[jax.ShapeDtypeStruct((nseq * L, DV), SCAN_OUT_DTYPE),
                   jax.ShapeDtypeStruct((nseq * L, DV), SCAN_OUT_DTYPE),
                   jax.ShapeDtypeStruct((nseq, 2, H, dk, dv), F32)],
        scratch_shapes=[pltpu.VMEM((2, H, dv, dk), F32)],
        compiler_params=_cparams(2),
        name="gla",
    )(*([proj] * 3 + [lr]) * 2, gw_pad, gate_b.reshape(2, 1, DK), s0)
    return (of, ob), S


def _gla_out_body(ofp_ref, ofs_ref, obp_ref, obs_ref, g_ref, ng_ref, y_ref, *, npt):
    oo = _pick(ofp_ref, ofs_ref, npt).astype(F32) + _pick(obp_ref, obs_ref, npt).astype(F32)
    ms = jnp.mean(oo * oo, axis=1, keepdims=True)
    y_ref[...] = (oo * lax.rsqrt(ms + LN_EPS) * ng_ref[...] * _silu(g_ref[...])).astype(y_ref.dtype)


def _gla_out(o_fwd, o_bwd, proj, col0, norm_g, tm):
    T = proj.shape[0]
    DV = o_fwd[0].shape[-1]
    dv = DV // C_HEADS
    cb0 = col0 // dv
    blk = lambda r, h=0: ((tm, dv), (r, h))
    npt, f_specs = _pair_specs(o_fwd, blk, tm)
    _, b_specs = _pair_specs(o_bwd, blk, tm)
    return pl.pallas_call(
        functools.partial(_gla_out_body, npt=npt),
        grid=(T // tm, C_HEADS),
        in_specs=f_specs + b_specs + [pl.BlockSpec((tm, dv), lambda i, h: (i, cb0 + h)),
                                      pl.BlockSpec((1, dv), lambda i, h: (0, h))],
        out_specs=pl.BlockSpec((tm, dv), lambda i, h: (i, h)),
        out_shape=jax.ShapeDtypeStruct((T, DV), BF16),
        compiler_params=_cparams(2),
        name="gla_out",
    )(*o_fwd, *o_bwd, proj, norm_g.reshape(1, DV))


def _layer_norm(z, g, b):
    mu = jnp.mean(z, axis=1, keepdims=True)
    zc = z - mu
    var = jnp.mean(zc * zc, axis=1, keepdims=True)
    return zc * lax.rsqrt(var + LN_EPS) * g + b


def _outproj_body(*refs, npts, g_row, alpha):
    n = len(npts)
    n_y = sum(1 if p is None else 2 for p in npts)
    y_refs = list(refs[:n_y])
    ws = refs[n_y:n_y + n]
    x_ref, m_ref, lg_ref, lb_ref, rw_ref, rb_ref, o_ref, h_ref, ids_ref, gts_ref = refs[n_y + n:]
    acc = None
    for npt, w in zip(npts, ws):
        y = y_refs.pop(0)[...] if npt is None else _pick(y_refs.pop(0), y_refs.pop(0), npt)
        part = jnp.dot(y, w[...], preferred_element_type=F32)
        acc = part if acc is None else acc + part
    z = alpha * x_ref[...] + m_ref[g_row:g_row + 1, :] * acc
    x_new = _layer_norm(z, lg_ref[...], lb_ref[...])
    o_ref[...] = x_new
    _route(x_new, m_ref, rw_ref, rb_ref, h_ref, ids_ref, gts_ref, sh_row=g_row + 1, sc_row=g_row + 2)


def _outproj_ln(ys, ws, x, modt, ln_g, ln_b, router_w, router_b, *, g_row, alpha, tm):
    T, D = x.shape
    E = router_w.shape[1]
    mod4, mod_map = modt[0], modt[1](tm)
    y_specs, y_args, npts = [], [], []
    for y in ys:
        if isinstance(y, tuple):
            K = y[0].shape[1]
            npt, specs = _pair_specs(y, lambda r, K=K: ((tm, K), (r, 0)), tm)
            y_specs += specs
            y_args += list(y)
            npts.append(npt)
        else:
            y_specs.append(pl.BlockSpec((tm, y.shape[1]), lambda i: (i, 0)))
            y_args.append(y)
            npts.append(None)
    in_specs = (y_specs
                + [pl.BlockSpec(w.shape, lambda i: (0, 0)) for w in ws]
                + [pl.BlockSpec((tm, D), lambda i: (i, 0)),
                   pl.BlockSpec((None, None, 6, D), lambda i: mod_map(i)),
                   pl.BlockSpec((1, D), lambda i: (0, 0)),
                   pl.BlockSpec((1, D), lambda i: (0, 0)),
                   pl.BlockSpec((E, D), lambda i: (0, 0)),
                   pl.BlockSpec((E, 1), lambda i: (0, 0))])
    return pl.pallas_call(
        functools.partial(_outproj_body, npts=tuple(npts), g_row=g_row, alpha=alpha),
        grid=(T // tm,),
        in_specs=in_specs,
        out_specs=[pl.BlockSpec((tm, D), lambda i: (i, 0)),
                   pl.BlockSpec((tm * TOK_ROWS, LANE), lambda i: (i, 0)),
                   pl.BlockSpec((SUBLANE, tm), lambda i: (0, i)),
                   pl.BlockSpec((SUBLANE, tm), lambda i: (0, i))],
        out_shape=[jax.ShapeDtypeStruct((T, D), F32),
                   jax.ShapeDtypeStruct((T * TOK_ROWS, LANE), F32),
                   jax.ShapeDtypeStruct((SUBLANE, T), jnp.int32),
                   jax.ShapeDtypeStruct((SUBLANE, T), F32)],
        compiler_params=_cparams(1),
        name="outproj_ln",
    )(*y_args, *ws, x, mod4, ln_g.reshape(1, D), ln_b.reshape(1, D), router_w.T, router_b.reshape(E, 1))


def _route(x, m_ref, w_ref, b_ref, h_ref, ids_ref, gts_ref, *, sh_row, sc_row):
    h = x * (1.0 + m_ref[sc_row:sc_row + 1, :]) + m_ref[sh_row:sh_row + 1, :]
    _to_tok_blocks(h_ref, h)
    logits = lax.dot_general(w_ref[...], h, NT_DIMS, precision=HI, preferred_element_type=F32) + b_ref[...]
    E, tm = logits.shape
    row = lax.broadcasted_iota(jnp.int32, (E, tm), 0)
    mx = jnp.max(logits, axis=0, keepdims=True)
    ex = jnp.exp(logits - mx)
    p = ex / jnp.sum(ex, axis=0, keepdims=True)
    grp = row // (E // N_GROUPS)
    best = None
    for g in range(N_GROUPS):
        pg = jnp.where(grp == g, p, -1.0)
        v1 = jnp.max(pg, axis=0, keepdims=True)
        i1 = jnp.min(jnp.where(pg == v1, row, E), axis=0, keepdims=True)
        pg2 = jnp.where(row == i1, -1.0, pg)
        v2 = jnp.max(pg2, axis=0, keepdims=True)
        i2 = jnp.min(jnp.where(pg2 == v2, row, E), axis=0, keepdims=True)
        score = v1 + v2
        if best is None:
            best = (score, v1, i1, v2, i2)
        else:
            take = score > best[0]
            best = tuple(jnp.where(take, n, o) for n, o in zip((score, v1, i1, v2, i2), best))
    _, v1, i1, v2, i2 = best
    tot = v1 + v2
    slot = lax.broadcasted_iota(jnp.int32, (SUBLANE, tm), 0)
    ids_ref[...] = jnp.where(slot == 0, i1, jnp.where(slot == 1, i2, 0))
    gts_ref[...] = jnp.where(slot == 0, v1 / tot, jnp.where(slot == 1, v2 / tot, 0.0))


def _to_tok_blocks(ref, x):
    tm = x.shape[0]
    for s in range(TOK_ROWS):
        ref[pl.ds(s, tm, stride=TOK_ROWS), :] = x[:, s * LANE:(s + 1) * LANE]


def _tok_chunk(ref, s, tm):
    return ref[pl.ds(s, tm, stride=TOK_ROWS), :]


class _RowGather:
    def __init__(self, src_hbm, idx_hbm, idx_s, bufs, isem, gsem, tm):
        self.src, self.idx_hbm, self.idx_s, self.bufs = src_hbm, idx_hbm, idx_s, bufs
        self.isem, self.gsem, self.tm = isem, gsem, tm

    def _idx_copy(self, tile, slot):
        return pltpu.make_async_copy(self.idx_hbm.at[tile], self.idx_s.at[slot], self.isem.at[slot])

    def _row_copy(self, slot, a, r):
        row = pl.multiple_of(self.idx_s[slot, a, r], TOK_ROWS)
        dst = pl.multiple_of(r * TOK_ROWS, TOK_ROWS)
        return pltpu.make_async_copy(self.src.at[pl.ds(row, TOK_ROWS), :],
                                     self.bufs[a][slot].at[pl.ds(dst, TOK_ROWS), :], self.gsem.at[slot, a])

    def rows_loop(self, slot, start):
        def body(r, carry):
            for a in range(len(self.bufs)):
                cp = self._row_copy(slot, a, r)
                cp.start() if start else cp.wait()
            return carry
        lax.fori_loop(0, self.tm, body, 0, unroll=8)

    def rows_start_inline(self, slot):
        for r in range(self.tm):
            for a in range(len(self.bufs)):
                self._row_copy(slot, a, r).start()

    def prologue(self, n_tiles):
        self._idx_copy(0, 0).start()
        self._idx_copy(0, 0).wait()
        self.rows_loop(0, True)
        if n_tiles > 1:
            self._idx_copy(1, 1).start()

    def advance_indices(self, i, n_tiles):
        slot = i % 2

        @pl.when(i + 1 < n_tiles)
        def _():
            self._idx_copy(i + 1, 1 - slot).wait()

        @pl.when(i + 2 < n_tiles)
        def _():
            self._idx_copy(i + 2, slot).start()


def _ffn_body(te_ref, va_ref, idx_hbm, h_hbm, wi_ref, wo_ref, o_ref, xbuf0, xbuf1, x16, a16, idx_s, isem, gsem,
              *, F, tm, n_tiles):
    i = pl.program_id(0)
    valid = va_ref[i] != 0
    first_invalid = jnp.logical_and(jnp.logical_not(valid), va_ref[jnp.maximum(i - 1, 0)] != 0)
    xbufs = (xbuf0, xbuf1)
    g = _RowGather(h_hbm, idx_hbm, idx_s, [xbufs], isem, gsem, tm)

    @pl.when(i == 0)
    def _():
        g.prologue(n_tiles)

    for slot in (0, 1):
        here = jnp.logical_and(valid, i % 2 == slot)

        @pl.when(here)
        def _(slot=slot):
            g.rows_loop(slot, False)
            g.advance_indices(i, n_tiles)

        @pl.when(here)
        def _(slot=slot):
            if slot in FFN_INLINE_START_SLOTS:
                g.rows_start_inline(1 - slot)
            else:
                g.rows_loop(1 - slot, True)
            for s in range(TOK_ROWS):
                x16[:, s * LANE:(s + 1) * LANE] = _tok_chunk(xbufs[slot], s, tm).astype(BF16)
            x = x16[...]
            fc = F // FFN_PARTS
            for p in range(FFN_PARTS):
                u = jnp.dot(x, wi_ref[:, p * fc:(p + 1) * fc], preferred_element_type=F32)
                w = jnp.dot(x, wi_ref[:, F + p * fc:F + (p + 1) * fc], preferred_element_type=F32)
                a16[:, p * fc:(p + 1) * fc] = (_silu(u) * w).astype(BF16)
            _to_tok_blocks(o_ref, jnp.dot(a16[...], wo_ref[...], preferred_element_type=F32))

        @pl.when(jnp.logical_and(first_invalid, i % 2 == slot))
        def _(slot=slot):
            g.rows_loop(slot, False)

            @pl.when(i + 1 < n_tiles)
            def _():
                g._idx_copy(i + 1, 1 - slot).wait()

    @pl.when(jnp.logical_not(valid))
    def _():
        o_ref[...] = jnp.zeros_like(o_ref)


def _expert_ffn(h, src_tok, w_in, w_out, layer, tile_e, tile_valid):
    n_tiles, _, tm = src_tok.shape
    F, D = w_out.shape[2:]
    return pl.pallas_call(
        functools.partial(_ffn_body, F=F, tm=tm, n_tiles=n_tiles),
        grid_spec=pltpu.PrefetchScalarGridSpec(
            num_scalar_prefetch=2,
            grid=(n_tiles,),
            in_specs=[pl.BlockSpec(memory_space=pl.ANY),
                      pl.BlockSpec(memory_space=pl.ANY),
                      pl.BlockSpec((None, None, D, 2 * F), lambda i, te, va: (layer, te[i], 0, 0)),
                      pl.BlockSpec((None, None, F, D), lambda i, te, va: (layer, te[i], 0, 0))],
            out_specs=pl.BlockSpec((tm * TOK_ROWS, LANE), lambda i, te, va: (i, 0)),
            scratch_shapes=[pltpu.VMEM((tm * TOK_ROWS, LANE), F32), pltpu.VMEM((tm * TOK_ROWS, LANE), F32),
                            pltpu.VMEM((tm, D), BF16), pltpu.VMEM((tm, F), BF16), pltpu.SMEM((2, 1, tm), jnp.int32),
                            pltpu.SemaphoreType.DMA((2,)), pltpu.SemaphoreType.DMA((2, 1))]),
        out_shape=jax.ShapeDtypeStruct((n_tiles * tm * TOK_ROWS, LANE), F32),
        compiler_params=_cparams(1),
        name="expert_ffn",
    )(tile_e, tile_valid, src_tok, h, w_in, w_out)


def _combine_body(*refs, g_row, alpha, tm, n_tiles, npt):
    x_ref, idx_hbm, y_hbm, gt_ref, m_ref, lg_ref, lb_ref = refs[:7]
    out_refs = refs[7:-7]
    y0a, y0b, y1a, y1b, idx_s, isem, gsem = refs[-7:]
    i = pl.program_id(0)
    y0, y1 = (y0a, y0b), (y1a, y1b)
    g = _RowGather(y_hbm, idx_hbm, idx_s, [y0, y1], isem, gsem, tm)

    @pl.when(i == 0)
    def _():
        g.prologue(n_tiles)

    for slot in (0, 1):
        @pl.when(i % 2 == slot)
        def _(slot=slot):
            g.rows_loop(slot, False)
            g.advance_indices(i, n_tiles)
            if n_tiles > 1:
                g.rows_start_inline(1 - slot)
            gt = gt_ref[...]
            g0, g1 = gt[:, 0:1], gt[:, 1:2]
            moe = jnp.concatenate([g0 * _tok_chunk(y0[slot], s, tm) + g1 * _tok_chunk(y1[slot], s, tm)
                                   for s in range(TOK_ROWS)], axis=1)
            z = alpha * x_ref[...] + m_ref[g_row:g_row + 1, :] * moe
            res = _layer_norm(z, lg_ref[...], lb_ref[...])
            if npt is None:
                out_refs[0][...] = res
            else:
                @pl.when(i < npt)
                def _():
                    out_refs[0][...] = res

                @pl.when(i >= npt)
                def _():
                    out_refs[1][...] = res

            if n_tiles > 1:
                @pl.when(i == n_tiles - 1)
                def _():
                    g.rows_loop(1 - slot, False)


def _combine_ln(x, ys, dest, gts, modt, ln_g, ln_b, *, g_row, alpha, split_rows=None):
    T, D = x.shape
    n_tiles, _, tm = dest.shape
    mod4, mod_map = modt[0], modt[1](tm)
    row = pl.BlockSpec((tm, D), lambda i: (i, 0))
    vec = pl.BlockSpec((1, D), lambda i: (0, 0))
    hbm = pl.BlockSpec(memory_space=pl.ANY)
    if split_rows is None:
        npt, out_specs, out_shape = None, row, jax.ShapeDtypeStruct((T, D), F32)
    else:
        npt = split_rows // tm
        out_specs = [pl.BlockSpec((tm, D), lambda i: (jnp.minimum(i, npt - 1), 0)),
                     pl.BlockSpec((tm, D), lambda i: (jnp.maximum(i - npt, 0), 0))]
        out_shape = [jax.ShapeDtypeStruct((split_rows, D), F32), jax.ShapeDtypeStruct((T - split_rows, D), F32)]
    return pl.pallas_call(
        functools.partial(_combine_body, g_row=g_row, alpha=alpha, tm=tm, n_tiles=n_tiles, npt=npt),
        grid=(n_tiles,),
        in_specs=[row, hbm, hbm, pl.BlockSpec((tm, LANE), lambda i: (i, 0)),
                  pl.BlockSpec((None, None, 6, D), lambda i: mod_map(i)), vec, vec],
        out_specs=out_specs,
        out_shape=out_shape,
        scratch_shapes=[pltpu.VMEM((tm * TOK_ROWS, LANE), F32)] * 4 + [
                        pltpu.SMEM((2, 2, tm), jnp.int32),
                        pltpu.SemaphoreType.DMA((2,)), pltpu.SemaphoreType.DMA((2, 2))],
        compiler_params=_cparams(1),
        name="combine_ln",
    )(x, dest, ys, gts, mod4, ln_g.reshape(1, D), ln_b.reshape(1, D))


def _route_meta(ids, tm, tm_tok):
    T = ids.shape[0]
    E = N_EXPERTS
    e_flat = ids.reshape(-1)
    onehot = (e_flat[:, None] == jnp.arange(E, dtype=jnp.int32)[None, :]).astype(jnp.int32)
    csum = jnp.cumsum(onehot, axis=0)
    rank = jnp.sum(csum * onehot, axis=1) - 1
    counts = csum[-1]
    padded = ((counts + tm - 1) // tm) * tm
    ends = jnp.cumsum(padded)
    starts = ends - padded
    dest = jnp.sum(starts[None, :] * onehot, axis=1) + rank
    n_rows = 2 * T + E * tm
    src_tok = jnp.zeros((n_rows,), jnp.int32).at[dest].set(jnp.arange(2 * T, dtype=jnp.int32) // 2,
                                                            unique_indices=True, mode="promise_in_bounds")
    tile_start = jnp.arange(n_rows // tm, dtype=jnp.int32) * tm
    valid = (tile_start < ends[-1]).astype(jnp.int32)
    probe = jnp.minimum(tile_start, ends[-1] - 1)
    tile_e = jnp.sum((ends[None, :] <= probe[:, None]).astype(jnp.int32), axis=1)
    dest_t = dest.reshape(T // tm_tok, tm_tok, 2).transpose(0, 2, 1)
    return (dest_t * TOK_ROWS, src_tok.reshape(n_rows // tm, 1, tm) * TOK_ROWS, jnp.minimum(tile_e, E - 1), valid)


def _pos_embed_2d(n_tokens, dim):
    rows = n_tokens // GRID_W
    quarter = dim // 4
    freqs = jnp.exp(-math.log(10000.0) * jnp.arange(quarter, dtype=F32) / quarter)
    r = jnp.broadcast_to(jnp.arange(rows, dtype=F32)[:, None], (rows, GRID_W)).reshape(-1)
    col = jnp.broadcast_to(jnp.arange(GRID_W, dtype=F32)[None, :], (rows, GRID_W)).reshape(-1)
    ar = r[:, None] * freqs
    ac = col[:, None] * freqs
    return jnp.concatenate([jnp.sin(ar), jnp.cos(ar), jnp.sin(ac), jnp.cos(ac)], -1)


def _pad_cols(w, n):
    return jnp.pad(w, ((0, 0),) * (w.ndim - 1) + ((0, n - w.shape[-1]),))


def kernel(x_prompt, x_sample, state_lru, state_mlstm_c, state_mlstm_n, state_mlstm_m, state_gla, c, c_ctx, mod_w, mod_b, ln_g, ln_b, even_w_in, even_w_out, lru_conv_w, lru_conv_b, lru_gate_w, lru_gate_b, lru_lambda, mlstm_gate_b, mlstm_norm_g, odd_w_in, odd_w_out, gla_gate_w, gla_gate_b, gla_norm_g, router_w, router_b, moe_w_in, moe_w_out):
    Bp, Lp, D = x_prompt.shape
    Bs, Ls, _ = x_sample.shape
    depth = mod_w.shape[0]
    Tp, Ts = Bp * Lp, Bs * Ls
    T = Tp + Ts
    alpha = (2 * depth) ** 0.25
    D_A = lru_conv_w.shape[-1]
    D_B = mlstm_norm_g.shape[-1]
    DK = gla_gate_w.shape[-1]
    DV = gla_norm_g.shape[-1]
    tm = next(t for t in (512, 256, 128) if Tp % t == 0 and Ls % t == 0)
    tn = 1024

    n_cond = 1 + Bs
    R = -(-n_cond // SUBLANE) * SUBLANE
    cond = jnp.zeros((R, D), F32).at[0].set(c_ctx).at[1:n_cond].set(c)
    mod_all = _modulation(cond, mod_w, mod_b)
    mod4 = mod_all.reshape(depth, R, 6, D)
    tm_proj = next(t for t in (1024, 512, 256, 128) if Tp % t == 0 and Ls % t == 0)
    assert D == TOK_ROWS * LANE

    x = _embed(x_prompt.reshape(Tp, D), x_sample.reshape(Ts, D), _pos_embed_2d(Ls, D), tm)

    moe_w_in_b = moe_w_in.astype(BF16)
    moe_w_out_b = moe_w_out.astype(BF16)
    groups = ((0, Bp, Lp), (Tp, Bs, Ls))

    s_lru, s_c, s_n, s_m, s_gla = [], [], [], [], []
    for l in range(depth):
        j = l // 2
        mt = (mod4, lambda t, l=l: (lambda i: (l, jnp.where(i < Tp // t, 0, 1 + (i - Tp // t) // (Ls // t)), 0, 0)))
        if l % 2 == 0:
            w_in = even_w_in[j]
            n_main = 2 * D_A + 4 * D_B
            proj, gates = _proj(x, mt, w_in[:, :n_main].astype(BF16), _pad_cols(w_in[:, n_main:], LANE),
                                sh_row=0, sc_row=1, tm=tm_proj, tn=tn)
            gate_bias = _pad_cols(mlstm_gate_b[j].reshape(1, -1), LANE)
            ya, hs = [], []
            for gi, (row0, nseq, L) in enumerate(groups):
                if gi == 0:
                    lru0 = jnp.zeros((nseq, 2, D_A), F32)
                    c0 = jnp.zeros((nseq, 2, B_HEADS, D_B // B_HEADS, D_B // B_HEADS), F32)
                    n0 = jnp.zeros((nseq, 2, B_HEADS, D_B // B_HEADS), F32)
                    m0 = jnp.zeros((nseq, 2, B_HEADS), F32)
                else:
                    lru0, c0, n0, m0 = state_lru[:, j], state_mlstm_c[:, j], state_mlstm_n[:, j], state_mlstm_m[:, j]
                y_a, st = _lru(proj, row0, nseq, L, lru_conv_w[j], lru_conv_b[j], lru_gate_w[j], lru_gate_b[j],
                               lru_lambda[j], lru0)
                h, C, n, m = _mlstm(proj, gates, gate_bias, row0, nseq, L, 2 * D_A, c0, n0, m0)
                ya.append(y_a)
                hs.append(h)
                if gi == 0:
                    s_lru.append(st)
                    s_c.append(C)
                    s_n.append(n.reshape(nseq, 2, B_HEADS, -1))
                    s_m.append(m.reshape(nseq, 2, B_HEADS))
            h_fwd, h_bwd = zip(*hs)
            y_b = _mlstm_out(h_fwd, h_bwd, proj, 2 * D_A + 3 * D_B, mlstm_norm_g[j], tm)
            w_out = even_w_out[j].astype(BF16)
            x, hf, ids, gts = _outproj_ln([tuple(ya), y_b], [w_out[:D_A], w_out[D_A:]], x, mt, ln_g[l, 0],
                                          ln_b[l, 0], router_w, router_b, g_row=2, alpha=alpha, tm=tm)
        else:
            w_in = odd_w_in[j]
            n_main = 2 * DK + 2 * DV
            proj, lr = _proj(x, mt, w_in[:, :n_main].astype(BF16), _pad_cols(w_in[:, n_main:], LANE),
                             sh_row=0, sc_row=1, tm=tm_proj, tn=tn)
            gw_pad = jnp.zeros((2, LANE, DK), F32)
            for dd in range(2):
                gw_pad = gw_pad.at[dd, dd * GLA_RANK:(dd + 1) * GLA_RANK].set(gla_gate_w[j, dd])
            os_ = []
            for gi, (row0, nseq, L) in enumerate(groups):
                s0 = (jnp.zeros((nseq, 2, C_HEADS, DK // C_HEADS, DV // C_HEADS), F32) if gi == 0
                      else state_gla[:, j])
                o, S = _gla(proj, lr, gw_pad, gla_gate_b[j], row0, nseq, L, s0)
                os_.append(o)
                if gi == 0:
                    s_gla.append(S)
            o_fwd, o_bwd = zip(*os_)
            y = _gla_out(o_fwd, o_bwd, proj, 2 * DK + DV, gla_norm_g[j], tm)
            x, hf, ids, gts = _outproj_ln([y], [odd_w_out[j].astype(BF16)], x, mt, ln_g[l, 0], ln_b[l, 0],
                                          router_w, router_b, g_row=2, alpha=alpha, tm=tm)

        gts = _pad_cols(gts[:2].T, LANE)
        dest, src_tok, tile_e, tile_valid = _route_meta(ids[:2].T, MOE_TM, tm)
        ys = _expert_ffn(hf, src_tok, moe_w_in_b, moe_w_out_b, l, tile_e, tile_valid)
        x = _combine_ln(x, ys, dest, gts, mt, ln_g[l, 1], ln_b[l, 1], g_row=5, alpha=alpha,
                        split_rows=Tp if l == depth - 1 else None)

    y_prompt = x[0].reshape(Bp, Lp, D)
    y_sample = x[1].reshape(Bs, Ls, D)
    return (y_prompt, y_sample, jnp.stack(s_lru, 1), jnp.stack(s_c, 1), jnp.stack(s_n, 1),
            jnp.stack(s_m, 1), jnp.stack(s_gla, 1))
```

```python
import functools
import math

import jax
import jax.numpy as jnp
from jax import lax
from jax.experimental import pallas as pl
from jax.experimental.pallas import tpu as pltpu

F32 = jnp.float32
BF16 = jnp.bfloat16
HI = lax.Precision.HIGHEST

LN_EPS = 1e-5
LRU_C = 8.0
A_BLOCKS = 8
B_HEADS = 4
C_HEADS = 4
GLA_RANK = 16
GLA_TAU = 16.0
N_EXPERTS = 16
N_GROUPS = 4
GRID_W = 64
LANE = 128
SUBLANE = 8
SCAN_CHUNK = 256
SCAN_OUT_DTYPE = jnp.bfloat16
MOE_TM = 512
FFN_PARTS = 4
FFN_INLINE_START_SLOTS = (0,)
TOK_ROWS = 16
VMEM_LIMIT = 56 * 1024 * 1024

NT_DIMS = (((1,), (1,)), ((), ()))
TN_DIMS = (((0,), (0,)), ((), ()))


def _cparams(n_axes):
    return pltpu.CompilerParams(dimension_semantics=("arbitrary",) * n_axes,
                                vmem_limit_bytes=VMEM_LIMIT)


def _sigmoid(x):
    return 1.0 / (1.0 + jnp.exp(-x))


def _silu(x):
    return x * _sigmoid(x)


def _log_sigmoid(x):
    return jnp.minimum(x, 0.0) - jnp.log1p(jnp.exp(-jnp.abs(x)))


def _gelu_tanh(x):
    return 0.5 * x * (1.0 + jnp.tanh(math.sqrt(2.0 / math.pi) * (x + 0.044715 * (x * x * x))))


def _mod_body(c_ref, w_ref, b_ref, o_ref):
    c = c_ref[...]
    o_ref[...] = jnp.dot(_silu(c), w_ref[...], precision=HI, preferred_element_type=F32) + b_ref[...]


def _modulation(cond, mod_w, mod_b):
    R, D = cond.shape
    nl, _, N = mod_w.shape
    tn = 1024
    return pl.pallas_call(
        _mod_body,
        grid=(nl, N // tn),
        in_specs=[pl.BlockSpec((R, D), lambda l, j: (0, 0)),
                  pl.BlockSpec((None, D, tn), lambda l, j: (l, 0, j)),
                  pl.BlockSpec((None, 1, tn), lambda l, j: (l, 0, j))],
        out_specs=pl.BlockSpec((None, R, tn), lambda l, j: (l, 0, j)),
        out_shape=jax.ShapeDtypeStruct((nl, R, N), F32),
        compiler_params=_cparams(2),
        name="modulation",
    )(cond, mod_w, mod_b.reshape(nl, 1, N))


def _embed_body(xp_ref, xs_ref, pos_ref, o_ref, *, n_prompt_tiles):
    i = pl.program_id(0)

    @pl.when(i < n_prompt_tiles)
    def _():
        o_ref[...] = xp_ref[...]

    @pl.when(i >= n_prompt_tiles)
    def _():
        o_ref[...] = xs_ref[...] + pos_ref[...]


def _embed(xp, xs, pos, tm):
    Tp, D = xp.shape
    Ts = xs.shape[0]
    Ls = pos.shape[0]
    npt, nst, npos = Tp // tm, Ts // tm, Ls // tm
    return pl.pallas_call(
        functools.partial(_embed_body, n_prompt_tiles=npt),
        grid=(npt + nst,),
        in_specs=[pl.BlockSpec((tm, D), lambda i: (jnp.minimum(i, npt - 1), 0)),
                  pl.BlockSpec((tm, D), lambda i: (jnp.maximum(i - npt, 0), 0)),
                  pl.BlockSpec((tm, D), lambda i: (jnp.maximum(i - npt, 0) % npos, 0))],
        out_specs=pl.BlockSpec((tm, D), lambda i: (i, 0)),
        out_shape=jax.ShapeDtypeStruct((Tp + Ts, D), F32),
        compiler_params=_cparams(1),
        name="embed",
    )(xp, xs, pos)


def _split3(x):
    hi = x.astype(BF16)
    r1 = x - hi.astype(F32)
    mid = r1.astype(BF16)
    return hi, mid, (r1 - mid.astype(F32)).astype(BF16)


def _proj_body(x_ref, m_ref, w_ref, wg_ref, o_ref, og_ref, h_ref, *, sh_row, sc_row):
    @pl.when(pl.program_id(1) == 0)
    def _():
        h = x_ref[...] * (1.0 + m_ref[sc_row:sc_row + 1, :]) + m_ref[sh_row:sh_row + 1, :]
        h_hi, h_mid, _ = _split3(h)
        h_ref[...] = h_hi
        wg_hi, wg_mid, _ = _split3(wg_ref[...])
        og_ref[...] = (jnp.dot(h_hi, wg_hi, preferred_element_type=F32)
                       + jnp.dot(h_hi, wg_mid, preferred_element_type=F32)
                       + jnp.dot(h_mid, wg_hi, preferred_element_type=F32))

    o_ref[...] = jnp.dot(h_ref[...], w_ref[...], preferred_element_type=F32)


def _proj(x, modt, w, w_gate, *, sh_row, sc_row, tm, tn):
    T, D = x.shape
    N = w.shape[1]
    mod4, mod_map = modt[0], modt[1](tm)
    return pl.pallas_call(
        functools.partial(_proj_body, sh_row=sh_row, sc_row=sc_row),
        grid=(T // tm, N // tn),
        in_specs=[pl.BlockSpec((tm, D), lambda i, j: (i, 0)),
                  pl.BlockSpec((None, None, 6, D), lambda i, j: mod_map(i)),
                  pl.BlockSpec((D, tn), lambda i, j: (0, j)),
                  pl.BlockSpec((D, LANE), lambda i, j: (0, 0))],
        out_specs=[pl.BlockSpec((tm, tn), lambda i, j: (i, j)),
                   pl.BlockSpec((tm, LANE), lambda i, j: (i, 0))],
        out_shape=[jax.ShapeDtypeStruct((T, N), F32), jax.ShapeDtypeStruct((T, LANE), F32)],
        scratch_shapes=[pltpu.VMEM((tm, D), BF16)],
        compiler_params=_cparams(2),
        name="proj",
    )(x, mod4, w, w_gate)


def _lru_body(ag_ref, ax_ref, cw_ref, cb_ref, gw_ref, gb_ref, lam_ref, h0_ref, y_ref, st_ref,
              af_s, uf_s, ab_s, ub_s, hf_s, hb_s, *, L):
    x = ax_ref[...]
    row = lax.broadcasted_iota(jnp.int32, (L, LANE), 0)
    xm1 = jnp.where(row >= 1, pltpu.roll(x, 1, 0), 0.0)
    xp1 = jnp.where(row < L - 1, pltpu.roll(x, L - 1, 0), 0.0)
    xp2 = jnp.where(row < L - 2, pltpu.roll(x, L - 2, 0), 0.0)
    xc = cb_ref[...] + xm1 * cw_ref[0:1, :] + x * cw_ref[1:2, :] + xp1 * cw_ref[2:3, :] + xp2 * cw_ref[3:4, :]
    xcb = xc.astype(BF16)
    nj = L // SUBLANE
    sub = lax.broadcasted_iota(jnp.int32, (nj, SUBLANE, LANE), 1)

    for d, (a_s, u_s) in enumerate(((af_s, uf_s), (ab_s, ub_s))):
        g = jnp.dot(xcb, gw_ref[d], preferred_element_type=F32) + gb_ref[d]
        r = _sigmoid(g[:, :LANE])
        ig = _sigmoid(g[:, LANE:])
        nlam = -lam_ref[d]
        softplus = jnp.maximum(nlam, 0.0) + jnp.log1p(jnp.exp(-jnp.abs(nlam)))
        log_a = (-LRU_C * softplus) * r
        a = jnp.exp(log_a)
        u = jnp.sqrt(jnp.tanh(-log_a) * (1.0 + a * a)) * ig * xc
        a3 = a.reshape(nj, SUBLANE, LANE)
        u3 = u.reshape(nj, SUBLANE, LANE)
        for k in (1, 2, 4):
            if d == 0:
                sh, keep = k, sub >= k
            else:
                sh, keep = SUBLANE - k, sub < SUBLANE - k
            a_sh = pltpu.roll(a3, sh, 1)
            u_sh = pltpu.roll(u3, sh, 1)
            u3 = jnp.where(keep, a3 * u_sh + u3, u3)
            a3 = jnp.where(keep, a3 * a_sh, a3)
        a_s[...] = a3.reshape(L, LANE)
        u_s[...] = u3.reshape(L, LANE)

    def carry(j, hs):
        hf, hb = hs
        rf = pl.multiple_of(j * SUBLANE, SUBLANE)
        rb = pl.multiple_of((nj - 1 - j) * SUBLANE, SUBLANE)
        of = af_s[pl.ds(rf, SUBLANE), :] * hf + uf_s[pl.ds(rf, SUBLANE), :]
        ob = ab_s[pl.ds(rb, SUBLANE), :] * hb + ub_s[pl.ds(rb, SUBLANE), :]
        hf_s[pl.ds(rf, SUBLANE), :] = of
        hb_s[pl.ds(rb, SUBLANE), :] = ob
        return (jnp.broadcast_to(of[SUBLANE - 1:SUBLANE, :], (SUBLANE, LANE)),
                jnp.broadcast_to(ob[0:1, :], (SUBLANE, LANE)))

    h0f = jnp.broadcast_to(h0_ref[0:1, :], (SUBLANE, LANE))
    h0b = jnp.broadcast_to(h0_ref[1:2, :], (SUBLANE, LANE))
    lax.fori_loop(0, nj, carry, (h0f, h0b))
    y_ref[...] = (_gelu_tanh(ag_ref[...]) * (hf_s[...] + hb_s[...])).astype(y_ref.dtype)
    st_ref[0:1, :] = hf_s[L - 1:L, :]
    st_ref[1:2, :] = hb_s[0:1, :]


def _lru(proj, row0, nseq, L, conv_w, conv_b, gate_w, gate_b, lam, h0):
    D_A = conv_w.shape[1]
    nb = D_A // LANE
    rb0 = row0 // L
    in_specs = [pl.BlockSpec((L, LANE), lambda s, h: (rb0 + s, h)),
                pl.BlockSpec((L, LANE), lambda s, h: (rb0 + s, nb + h)),
                pl.BlockSpec((4, LANE), lambda s, h: (0, h)),
                pl.BlockSpec((1, LANE), lambda s, h: (0, h)),
                pl.BlockSpec((2, None, LANE, 2 * LANE), lambda s, h: (0, h, 0, 0)),
                pl.BlockSpec((2, None, 1, 2 * LANE), lambda s, h: (0, h, 0, 0)),
                pl.BlockSpec((2, None, 1, LANE), lambda s, h: (0, h, 0, 0)),
                pl.BlockSpec((None, 2, LANE), lambda s, h: (s, 0, h))]
    args = [proj, proj, conv_w, conv_b.reshape(1, D_A), gate_w.astype(BF16),
            gate_b.reshape(2, nb, 1, 2 * LANE), lam.reshape(2, nb, 1, LANE), h0]
    return pl.pallas_call(
        functools.partial(_lru_body, L=L),
        grid=(nseq, nb),
        in_specs=in_specs,
        out_specs=[pl.BlockSpec((L, LANE), lambda s, h: (s, h)),
                   pl.BlockSpec((None, 2, LANE), lambda s, h: (s, 0, h))],
        out_shape=[jax.ShapeDtypeStruct((nseq * L, D_A), BF16),
                   jax.ShapeDtypeStruct((nseq, 2, D_A), F32)],
        scratch_shapes=[pltpu.VMEM((L, LANE), F32)] * 6,
        compiler_params=_cparams(2),
        name="rglru",
    )(*args)


def _tri_mask(Tc, d):
    r = lax.broadcasted_iota(jnp.int32, (Tc, Tc), 0)
    c = lax.broadcasted_iota(jnp.int32, (Tc, Tc), 1)
    return (c - r) * (1 - 2 * d) <= 0


def _mlstm_dir(d, q_ref, k_ref, v_ref, g_ref, gb_ref, h_ref, C_s, n_s, m_s, *, Tc, k_scale, dh):
    H = B_HEADS
    last = Tc - 1 if d == 0 else 0
    lane = lax.broadcasted_iota(jnp.int32, (Tc, LANE), 1)
    G = g_ref[...] + gb_ref[...]
    tri = _tri_mask(Tc, d)
    Bm = jnp.dot(tri.astype(F32), _log_sigmoid(G), precision=HI, preferred_element_type=F32)
    col0 = d * (2 * H)
    X = jnp.where(lane >= col0 + H, Bm, G)
    srow = lax.broadcasted_iota(jnp.int32, (2 * H, LANE), 0)
    slane = lax.broadcasted_iota(jnp.int32, (2 * H, LANE), 1)
    sel = (slane == col0 + srow).astype(F32)
    R = lax.dot_general(sel, X, NT_DIMS, precision=HI, preferred_element_type=F32)
    for hd in range(H):
        i_col = jnp.sum(jnp.where(lane == col0 + hd, X, 0.0), axis=1, keepdims=True)
        b_col = jnp.sum(jnp.where(lane == col0 + H + hd, X, 0.0), axis=1, keepdims=True)
        i_row = R[hd:hd + 1, :]
        b_row = R[H + hd:H + hd + 1, :]
        dmat = jnp.where(tri, b_col - b_row + i_row, -jnp.inf)
        m_prev = m_s[d, hd]
        m_inter = b_col + m_prev
        m_t = jnp.maximum(m_inter, jnp.max(dmat, axis=1, keepdims=True))
        cs = slice(hd * dh, (hd + 1) * dh)
        qf = q_ref[:, cs]
        qb = qf.astype(BF16)
        kf = k_ref[:, cs] * k_scale
        kb = kf.astype(BF16)
        vb = v_ref[:, cs].astype(BF16)
        S = lax.dot_general(qb, kb, NT_DIMS, preferred_element_type=F32) * jnp.exp(dmat - m_t)
        inter_scale = jnp.exp(m_inter - m_t)
        Cm = C_s[d, hd]
        num =(jnp.dot(S.astype(BF16), vb, preferred_element_type=F32)
               + inter_scale * jnp.dot(qb, Cm.astype(BF16), preferred_element_type=F32))
        nv = n_s[d, hd]
        qn = jnp.sum(qf * nv, axis=1, keepdims=True)
        den = jnp.sum(S, axis=1, keepdims=True) + inter_scale * qn
        h_ref[:, cs] = (num / jnp.maximum(jnp.abs(den), jnp.exp(-m_t))).astype(h_ref.dtype)
        bL = b_col[last:last + 1, :]
        g_col = bL - b_col + i_col
        m_new = jnp.maximum(bL + m_prev, jnp.max(g_col, axis=0, keepdims=True))
        wk = jnp.exp(g_col - m_new)
        decay = jnp.exp(bL + m_prev - m_new)
        kw = kf * wk
        C_s[d, hd] = decay * Cm + lax.dot_general(kw.astype(BF16), vb, TN_DIMS, preferred_element_type=F32)
        n_s[d, hd] = decay * nv + jnp.sum(kw, axis=0, keepdims=True)
        m_s[d, hd] = m_new


def _mlstm_body(qf_ref, kf_ref, vf_ref, gf_ref, qb_ref, kb_ref, vb_ref, gbk_ref, gb_ref, c0_ref, n0_ref, m0_ref,
                hf_ref, hb_ref, co_ref, no_ref, mo_ref, C_s, n_s, m_s, *, nc, **kw):
    c = pl.program_id(1)

    @pl.when(c == 0)
    def _():
        C_s[...] = c0_ref[...]
        n_s[...] = n0_ref[...]
        m_s[...] = m0_ref[...]

    _mlstm_dir(0, qf_ref, kf_ref, vf_ref, gf_ref, gb_ref, hf_ref, C_s, n_s, m_s, **kw)
    _mlstm_dir(1, qb_ref, kb_ref, vb_ref, gbk_ref, gb_ref, hb_ref, C_s, n_s, m_s, **kw)

    @pl.when(c == nc - 1)
    def _():
        co_ref[...] = C_s[...]
        no_ref[...] = n_s[...]
        mo_ref[...] = m_s[...]


def _mlstm(proj, gates, gate_bias, row0, nseq, L, col0, c0, n0, m0):
    H = B_HEADS
    dh = c0.shape[-1]
    D_B = H * dh
    Tc = min(SCAN_CHUNK, L)
    nc = L // Tc
    rb0 = row0 // Tc
    cb0 = col0 // D_B
    fwd = lambda s, c: s * nc + c
    bwd = lambda s, c: s * nc + nc - 1 - c

    def chunk_specs(rb):
        return [pl.BlockSpec((Tc, D_B), lambda s, c, off=off: (rb0 + rb(s, c), cb0 + off)) for off in range(3)] + [
            pl.BlockSpec((Tc, LANE), lambda s, c: (rb0 + rb(s, c), 0))]

    def st_specs():
        st_map = lambda s, c: (s, 0, 0, 0, 0)
        return [pl.BlockSpec((None, 2, H, dh, dh), st_map), pl.BlockSpec((None, 2, H, 1, dh), st_map),
                pl.BlockSpec((None, 2, H, 1, 1), st_map)]

    hf, hb, C, n, m = pl.pallas_call(
        functools.partial(_mlstm_body, Tc=Tc, nc=nc, k_scale=dh ** -0.5, dh=dh),
        grid=(nseq, nc),
        in_specs=chunk_specs(fwd) + chunk_specs(bwd) + [pl.BlockSpec((1, LANE), lambda s, c: (0, 0))] + st_specs(),
        out_specs=[pl.BlockSpec((Tc, D_B), lambda s, c: (fwd(s, c), 0)),
                   pl.BlockSpec((Tc, D_B), lambda s, c: (bwd(s, c), 0))] + st_specs(),
        out_shape=[jax.ShapeDtypeStruct((nseq * L, D_B), SCAN_OUT_DTYPE),
                   jax.ShapeDtypeStruct((nseq * L, D_B), SCAN_OUT_DTYPE),
                   jax.ShapeDtypeStruct((nseq, 2, H, dh, dh), F32),
                   jax.ShapeDtypeStruct((nseq, 2, H, 1, dh), F32),
                   jax.ShapeDtypeStruct((nseq, 2, H, 1, 1), F32)],
        scratch_shapes=[pltpu.VMEM((2, H, dh, dh), F32), pltpu.VMEM((2, H, 1, dh), F32),
                        pltpu.VMEM((2, H, 1, 1), F32)],
        compiler_params=_cparams(2),
        name="mlstm",
    )(*([proj] * 3 + [gates]) * 2, gate_bias, c0, n0.reshape(nseq, 2, H, 1, dh), m0.reshape(nseq, 2, H, 1, 1))
    return (hf, hb), C, n, m


def _pair_specs(pair, block, tm):
    npt = pair[0].shape[-2] // tm
    (shape, _), mk = block(0), block
    return npt, [pl.BlockSpec(shape, lambda i, *rest: mk(jnp.minimum(i, npt - 1), *rest)[1]),
                 pl.BlockSpec(shape, lambda i, *rest: mk(jnp.maximum(i - npt, 0), *rest)[1])]


def _pick(p_ref, s_ref, npt):
    return jnp.where(pl.program_id(0) < npt, p_ref[...], s_ref[...])


def _mlstm_out_body(hfp_ref, hfs_ref, hbp_ref, hbs_ref, o_ref, g_ref, y_ref, *, npt, dh):
    h = _pick(hfp_ref, hfs_ref, npt).astype(F32) + _pick(hbp_ref, hbs_ref, npt).astype(F32)
    for hd in range(B_HEADS):
        cs = slice(hd * dh, (hd + 1) * dh)
        hm = h[:, cs]
        mu = jnp.mean(hm, axis=1, keepdims=True)
        xc = hm - mu
        var = jnp.mean(xc * xc, axis=1, keepdims=True)
        y_ref[:, cs] = (_sigmoid(o_ref[:, cs]) * (xc * lax.rsqrt(var + LN_EPS) * g_ref[:, cs])).astype(y_ref.dtype)


def _mlstm_out(h_fwd, h_bwd, proj, col0, norm_g, tm):
    T = proj.shape[0]
    D_B = h_fwd[0].shape[-1]
    cb0 = col0 // D_B
    blk = lambda r: ((tm, D_B), (r, 0))
    npt, f_specs = _pair_specs(h_fwd, blk, tm)
    _, b_specs = _pair_specs(h_bwd, blk, tm)
    return pl.pallas_call(
        functools.partial(_mlstm_out_body, npt=npt, dh=D_B // B_HEADS),
        grid=(T // tm,),
        in_specs=f_specs + b_specs + [pl.BlockSpec((tm, D_B), lambda i: (i, cb0)),
                                      pl.BlockSpec((1, D_B), lambda i: (0, 0))],
        out_specs=pl.BlockSpec((tm, D_B), lambda i: (i, 0)),
        out_shape=jax.ShapeDtypeStruct((T, D_B), BF16),
        compiler_params=_cparams(1),
        name="mlstm_out",
    )(*h_fwd, *h_bwd, proj, norm_g.reshape(1, D_B))


def _gla_dir(d, q_ref, k_ref, v_ref, lr_ref, gw_ref, gb_ref, o_ref, ST_s, *, Tc, q_scale, dk, dv):
    H = C_HEADS
    last = Tc - 1 if d == 0 else 0
    lr_hi, lr_mid, _ = _split3(lr_ref[...])
    gw_hi, gw_mid, _ = _split3(gw_ref[d])
    z = (jnp.dot(lr_hi, gw_hi, preferred_element_type=F32)
         + jnp.dot(lr_hi, gw_mid, preferred_element_type=F32)
         + jnp.dot(lr_mid, gw_hi, preferred_element_type=F32)) + gb_ref[d]
    loga = _log_sigmoid(z) * (1.0 / GLA_TAU)
    tri = _tri_mask(Tc, d)
    trib = tri.astype(F32).astype(BF16)
    b = sum(jnp.dot(trib, piece, preferred_element_type=F32) for piece in _split3(loga))
    qs = (q_ref[...] * q_scale * jnp.exp(b)).astype(BF16)
    kf = k_ref[...]
    ke = (kf * jnp.exp(-b)).astype(BF16)
    bL = b[last:last + 1, :]
    kd = (kf * jnp.exp(bL - b)).astype(BF16)
    ebL = jnp.exp(bL)
    for hd in range(H):
        ks = slice(hd * dk, (hd + 1) * dk)
        vs = slice(hd * dv, (hd + 1) * dv)
        vb = v_ref[:, vs].astype(BF16)
        att = jnp.where(tri, lax.dot_general(qs[:, ks], ke[:, ks], NT_DIMS, preferred_element_type=F32), 0.0)
        ST = ST_s[d, hd]
        inter = lax.dot_general(qs[:, ks], ST.astype(BF16), NT_DIMS, preferred_element_type=F32)
        o_ref[:, vs] = (inter + jnp.dot(att.astype(BF16), vb, preferred_element_type=F32)).astype(o_ref.dtype)
        ST_s[d, hd] = ST * ebL[:, ks] + lax.dot_general(vb, kd[:, ks], TN_DIMS, preferred_element_type=F32)


def _gla_body(qf_ref, kf_ref, vf_ref, lrf_ref, qb_ref, kb_ref, vb_ref, lrb_ref, gw_ref, gb_ref, s0_ref,
              of_ref, ob_ref, so_ref, ST_s, *, nc, **kw):
    c = pl.program_id(1)
    states = [(d, hd) for d in range(2) for hd in range(C_HEADS)]

    @pl.when(c == 0)
    def _():
        for d, hd in states:
            ST_s[d, hd] = s0_ref[d, hd].T

    _gla_dir(0, qf_ref, kf_ref, vf_ref, lrf_ref, gw_ref, gb_ref, of_ref, ST_s, **kw)
    _gla_dir(1, qb_ref, kb_ref, vb_ref, lrb_ref, gw_ref, gb_ref, ob_ref, ST_s, **kw)

    @pl.when(c == nc - 1)
    def _():
        for d, hd in states:
            so_ref[d, hd] = ST_s[d, hd].T


def _gla(proj, lr, gw_pad, gate_b, row0, nseq, L, s0):
    H = C_HEADS
    dk, dv = s0.shape[-2:]
    DK, DV = H * dk, H * dv
    Tc = min(SCAN_CHUNK, L)
    nc = L // Tc
    rb0 = row0 // Tc
    fwd = lambda s, c: s * nc + c
    bwd = lambda s, c: s * nc + nc - 1 - c

    def chunk_specs(rb):
        return [pl.BlockSpec((Tc, DK), lambda s, c: (rb0 + rb(s, c), 0)),
                pl.BlockSpec((Tc, DK), lambda s, c: (rb0 + rb(s, c), 1)),
                pl.BlockSpec((Tc, DV), lambda s, c: (rb0 + rb(s, c), (2 * DK) // DV)),
                pl.BlockSpec((Tc, LANE), lambda s, c: (rb0 + rb(s, c), 0))]

    st_map = lambda s, c: (s, 0, 0, 0, 0)
    of, ob, S = pl.pallas_call(
        functools.partial(_gla_body, Tc=Tc, nc=nc, q_scale=dk ** -0.5, dk=dk, dv=dv),
        grid=(nseq, nc),
        in_specs=chunk_specs(fwd) + chunk_specs(bwd) + [
            pl.BlockSpec((2, LANE, DK), lambda s, c: (0, 0, 0)),
            pl.BlockSpec((2, 1, DK), lambda s, c: (0, 0, 0)),
            pl.BlockSpec((None, 2, H, dk, dv), st_map)],
        out_specs=[pl.BlockSpec((Tc, DV), lambda s, c: (fwd(s, c), 0)),
                   pl.BlockSpec((Tc, DV), lambda s, c: (bwd(s, c), 0)),
                   pl.BlockSpec((None, 2, H, dk, dv), st_map)],
        out_shape=[jax.ShapeDtypeStruct((nseq * L, DV), SCAN_OUT_DTYPE),
                   jax.ShapeDtypeStruct((nseq * L, DV), SCAN_OUT_DTYPE),
                   jax.ShapeDtypeStruct((nseq, 2, H, dk, dv), F32)],
        scratch_shapes=[pltpu.VMEM((2, H, dv, dk), F32)],
        compiler_params=_cparams(2),
        name="gla",
    )(*([proj] * 3 + [lr]) * 2, gw_pad, gate_b.reshape(2, 1, DK), s0)
    return (of, ob), S


def _gla_out_body(ofp_ref, ofs_ref, obp_ref, obs_ref, g_ref, ng_ref, y_ref, *, npt, dv):
    o = _pick(ofp_ref, ofs_ref, npt).astype(F32) + _pick(obp_ref, obs_ref, npt).astype(F32)
    for hd in range(C_HEADS):
        vs = slice(hd * dv, (hd + 1) * dv)
        oo = o[:, vs]
        ms = jnp.mean(oo * oo, axis=1, keepdims=True)
        y_ref[:, vs] = (oo * lax.rsqrt(ms + LN_EPS) * ng_ref[:, vs] * _silu(g_ref[:, vs])).astype(y_ref.dtype)


def _gla_out(o_fwd, o_bwd, proj, col0, norm_g, tm):
    T = proj.shape[0]
    DV = o_fwd[0].shape[-1]
    cb0 = col0 // DV
    blk = lambda r: ((tm, DV), (r, 0))
    npt, f_specs = _pair_specs(o_fwd, blk, tm)
    _, b_specs = _pair_specs(o_bwd, blk, tm)
    return pl.pallas_call(
        functools.partial(_gla_out_body, npt=npt, dv=DV // C_HEADS),
        grid=(T // tm,),
        in_specs=f_specs + b_specs + [pl.BlockSpec((tm, DV), lambda i: (i, cb0)),
                                      pl.BlockSpec((1, DV), lambda i: (0, 0))],
        out_specs=pl.BlockSpec((tm, DV), lambda i: (i, 0)),
        out_shape=jax.ShapeDtypeStruct((T, DV), BF16),
        compiler_params=_cparams(1),
        name="gla_out",
    )(*o_fwd, *o_bwd, proj, norm_g.reshape(1, DV))


def _layer_norm(z, g, b):
    mu = jnp.mean(z, axis=1, keepdims=True)
    zc = z - mu
    var = jnp.mean(zc * zc, axis=1, keepdims=True)
    return zc * lax.rsqrt(var + LN_EPS) * g + b


def _outproj_body(*refs, npts, g_row, alpha):
    n = len(npts)
    n_y = sum(1 if p is None else 2 for p in npts)
    y_refs = list(refs[:n_y])
    ws = refs[n_y:n_y + n]
    x_ref, m_ref, lg_ref, lb_ref, rw_ref, rb_ref, o_ref, h_ref, ids_ref, gts_ref = refs[n_y + n:]
    acc = None
    for npt, w in zip(npts, ws):
        y = y_refs.pop(0)[...] if npt is None else _pick(y_refs.pop(0), y_refs.pop(0), npt)
        part = jnp.dot(y, w[...], preferred_element_type=F32)
        acc = part if acc is None else acc + part
    z = alpha * x_ref[...] + m_ref[g_row:g_row + 1, :] * acc
    x_new = _layer_norm(z, lg_ref[...], lb_ref[...])
    o_ref[...] = x_new
    _route(x_new, m_ref, rw_ref, rb_ref, h_ref, ids_ref, gts_ref, sh_row=g_row + 1, sc_row=g_row + 2)


def _outproj_ln(ys, ws, x, modt, ln_g, ln_b, router_w, router_b, *, g_row, alpha, tm):
    T, D = x.shape
    E = router_w.shape[1]
    mod4, mod_map = modt[0], modt[1](tm)
    y_specs, y_args, npts = [], [], []
    for y in ys:
        if isinstance(y, tuple):
            K = y[0].shape[1]
            npt, specs = _pair_specs(y, lambda r, K=K: ((tm, K), (r, 0)), tm)
            y_specs += specs
            y_args += list(y)
            npts.append(npt)
        else:
            y_specs.append(pl.BlockSpec((tm, y.shape[1]), lambda i: (i, 0)))
            y_args.append(y)
            npts.append(None)
    in_specs = (y_specs
                + [pl.BlockSpec(w.shape, lambda i: (0, 0)) for w in ws]
                + [pl.BlockSpec((tm, D), lambda i: (i, 0)),
                   pl.BlockSpec((None, None, 6, D), lambda i: mod_map(i)),
                   pl.BlockSpec((1, D), lambda i: (0, 0)),
                   pl.BlockSpec((1, D), lambda i: (0, 0)),
                   pl.BlockSpec((E, D), lambda i: (0, 0)),
                   pl.BlockSpec((E, 1), lambda i: (0, 0))])
    return pl.pallas_call(
        functools.partial(_outproj_body, npts=tuple(npts), g_row=g_row, alpha=alpha),
        grid=(T // tm,),
        in_specs=in_specs,
        out_specs=[pl.BlockSpec((tm, D), lambda i: (i, 0)),
                   pl.BlockSpec((tm * TOK_ROWS, LANE), lambda i: (i, 0)),
                   pl.BlockSpec((SUBLANE, tm), lambda i: (0, i)),
                   pl.BlockSpec((SUBLANE, tm), lambda i: (0, i))],
        out_shape=[jax.ShapeDtypeStruct((T, D), F32),
                   jax.ShapeDtypeStruct((T * TOK_ROWS, LANE), F32),
                   jax.ShapeDtypeStruct((SUBLANE, T), jnp.int32),
                   jax.ShapeDtypeStruct((SUBLANE, T), F32)],
        compiler_params=_cparams(1),
        name="outproj_ln",
    )(*y_args, *ws, x, mod4, ln_g.reshape(1, D), ln_b.reshape(1, D), router_w.T, router_b.reshape(E, 1))


def _route(x, m_ref, w_ref, b_ref, h_ref, ids_ref, gts_ref, *, sh_row, sc_row):
    h = x * (1.0 + m_ref[sc_row:sc_row + 1, :]) + m_ref[sh_row:sh_row + 1, :]
    _to_tok_blocks(h_ref, h)
    logits = lax.dot_general(w_ref[...], h, NT_DIMS, precision=HI, preferred_element_type=F32) + b_ref[...]
    E, tm = logits.shape
    row = lax.broadcasted_iota(jnp.int32, (E, tm), 0)
    mx = jnp.max(logits, axis=0, keepdims=True)
    ex = jnp.exp(logits - mx)
    p = ex / jnp.sum(ex, axis=0, keepdims=True)
    grp = row // (E // N_GROUPS)
    best = None
    for g in range(N_GROUPS):
        pg = jnp.where(grp == g, p, -1.0)
        v1 = jnp.max(pg, axis=0, keepdims=True)
        i1 = jnp.min(jnp.where(pg == v1, row, E), axis=0, keepdims=True)
        pg2 = jnp.where(row == i1, -1.0, pg)
        v2 = jnp.max(pg2, axis=0, keepdims=True)
        i2 = jnp.min(jnp.where(pg2 == v2, row, E), axis=0, keepdims=True)
        score = v1 + v2
        if best is None:
            best = (score, v1, i1, v2, i2)
        else:
            take = score > best[0]
            best = tuple(jnp.where(take, n, o) for n, o in zip((score, v1, i1, v2, i2), best))
    _, v1, i1, v2, i2 = best
    tot = v1 + v2
    slot = lax.broadcasted_iota(jnp.int32, (SUBLANE, tm), 0)
    ids_ref[...] = jnp.where(slot == 0, i1, jnp.where(slot == 1, i2, 0))
    gts_ref[...] = jnp.where(slot == 0, v1 / tot, jnp.where(slot == 1, v2 / tot, 0.0))


def _to_tok_blocks(ref, x):
    tm = x.shape[0]
    for s in range(TOK_ROWS):
        ref[pl.ds(s, tm, stride=TOK_ROWS), :] = x[:, s * LANE:(s + 1) * LANE]


def _tok_chunk(ref, s, tm):
    return ref[pl.ds(s, tm, stride=TOK_ROWS), :]


class _RowGather:
    def __init__(self, src_hbm, idx_hbm, idx_s, bufs, isem, gsem, tm):
        self.src, self.idx_hbm, self.idx_s, self.bufs = src_hbm, idx_hbm, idx_s, bufs
        self.isem, self.gsem, self.tm = isem, gsem, tm

    def _idx_copy(self, tile, slot):
        return pltpu.make_async_copy(self.idx_hbm.at[tile], self.idx_s.at[slot], self.isem.at[slot])

    def _row_copy(self, slot, a, r):
        row = pl.multiple_of(self.idx_s[slot, a, r], TOK_ROWS)
        dst = pl.multiple_of(r * TOK_ROWS, TOK_ROWS)
        return pltpu.make_async_copy(self.src.at[pl.ds(row, TOK_ROWS), :],
                                     self.bufs[a][slot].at[pl.ds(dst, TOK_ROWS), :], self.gsem.at[slot, a])

    def rows_loop(self, slot, start):
        def body(r, carry):
            for a in range(len(self.bufs)):
                cp = self._row_copy(slot, a, r)
                cp.start() if start else cp.wait()
            return carry
        lax.fori_loop(0, self.tm, body, 0, unroll=8)

    def rows_start_inline(self, slot):
        for r in range(self.tm):
            for a in range(len(self.bufs)):
                self._row_copy(slot, a, r).start()

    def prologue(self, n_tiles):
        self._idx_copy(0, 0).start()
        self._idx_copy(0, 0).wait()
        self.rows_loop(0, True)
        if n_tiles > 1:
            self._idx_copy(1, 1).start()

    def advance_indices(self, i, n_tiles):
        slot = i % 2

        @pl.when(i + 1 < n_tiles)
        def _():
            self._idx_copy(i + 1, 1 - slot).wait()

        @pl.when(i + 2 < n_tiles)
        def _():
            self._idx_copy(i + 2, slot).start()


def _ffn_body(te_ref, va_ref, idx_hbm, h_hbm, wi_ref, wo_ref, o_ref, xbuf0, xbuf1, x16, a16, idx_s, isem, gsem,
              *, F, tm, n_tiles):
    i = pl.program_id(0)
    valid = va_ref[i] != 0
    first_invalid = jnp.logical_and(jnp.logical_not(valid), va_ref[jnp.maximum(i - 1, 0)] != 0)
    xbufs = (xbuf0, xbuf1)
    g = _RowGather(h_hbm, idx_hbm, idx_s, [xbufs], isem, gsem, tm)

    @pl.when(i == 0)
    def _():
        g.prologue(n_tiles)

    for slot in (0, 1):
        here = jnp.logical_and(valid, i % 2 == slot)

        @pl.when(here)
        def _(slot=slot):
            g.rows_loop(slot, False)
            g.advance_indices(i, n_tiles)

        @pl.when(here)
        def _(slot=slot):
            if slot in FFN_INLINE_START_SLOTS:
                g.rows_start_inline(1 - slot)
            else:
                g.rows_loop(1 - slot, True)
            for s in range(TOK_ROWS):
                x16[:, s * LANE:(s + 1) * LANE] = _tok_chunk(xbufs[slot], s, tm).astype(BF16)
            x = x16[...]
            fc = F // FFN_PARTS
            for p in range(FFN_PARTS):
                u = jnp.dot(x, wi_ref[:, p * fc:(p + 1) * fc], preferred_element_type=F32)
                w = jnp.dot(x, wi_ref[:, F + p * fc:F + (p + 1) * fc], preferred_element_type=F32)
                a16[:, p * fc:(p + 1) * fc] = (_silu(u) * w).astype(BF16)
            _to_tok_blocks(o_ref, jnp.dot(a16[...], wo_ref[...], preferred_element_type=F32))

        @pl.when(jnp.logical_and(first_invalid, i % 2 == slot))
        def _(slot=slot):
            g.rows_loop(slot, False)

            @pl.when(i + 1 < n_tiles)
            def _():
                g._idx_copy(i + 1, 1 - slot).wait()

    @pl.when(jnp.logical_not(valid))
    def _():
        o_ref[...] = jnp.zeros_like(o_ref)


def _expert_ffn(h, src_tok, w_in, w_out, layer, tile_e, tile_valid):
    n_tiles, _, tm = src_tok.shape
    F, D = w_out.shape[2:]
    return pl.pallas_call(
        functools.partial(_ffn_body, F=F, tm=tm, n_tiles=n_tiles),
        grid_spec=pltpu.PrefetchScalarGridSpec(
            num_scalar_prefetch=2,
            grid=(n_tiles,),
            in_specs=[pl.BlockSpec(memory_space=pl.ANY),
                      pl.BlockSpec(memory_space=pl.ANY),
                      pl.BlockSpec((None, None, D, 2 * F), lambda i, te, va: (layer, te[i], 0, 0)),
                      pl.BlockSpec((None, None, F, D), lambda i, te, va: (layer, te[i], 0, 0))],
            out_specs=pl.BlockSpec((tm * TOK_ROWS, LANE), lambda i, te, va: (i, 0)),
            scratch_shapes=[pltpu.VMEM((tm * TOK_ROWS, LANE), F32), pltpu.VMEM((tm * TOK_ROWS, LANE), F32),
                            pltpu.VMEM((tm, D), BF16), pltpu.VMEM((tm, F), BF16), pltpu.SMEM((2, 1, tm), jnp.int32),
                            pltpu.SemaphoreType.DMA((2,)), pltpu.SemaphoreType.DMA((2, 1))]),
        out_shape=jax.ShapeDtypeStruct((n_tiles * tm * TOK_ROWS, LANE), F32),
        compiler_params=_cparams(1),
        name="expert_ffn",
    )(tile_e, tile_valid, src_tok, h, w_in, w_out)


def _combine_body(*refs, g_row, alpha, tm, n_tiles, npt):
    x_ref, idx_hbm, y_hbm, gt_ref, m_ref, lg_ref, lb_ref = refs[:7]
    out_refs = refs[7:-7]
    y0a, y0b, y1a, y1b, idx_s, isem, gsem = refs[-7:]
    i = pl.program_id(0)
    y0, y1 = (y0a, y0b), (y1a, y1b)
    g = _RowGather(y_hbm, idx_hbm, idx_s, [y0, y1], isem, gsem, tm)

    @pl.when(i == 0)
    def _():
        g.prologue(n_tiles)

    for slot in (0, 1):
        @pl.when(i % 2 == slot)
        def _(slot=slot):
            g.rows_loop(slot, False)
            g.advance_indices(i, n_tiles)
            if n_tiles > 1:
                g.rows_start_inline(1 - slot)
            gt = gt_ref[...]
            g0, g1 = gt[:, 0:1], gt[:, 1:2]
            moe = jnp.concatenate([g0 * _tok_chunk(y0[slot], s, tm) + g1 * _tok_chunk(y1[slot], s, tm)
                                   for s in range(TOK_ROWS)], axis=1)
            z = alpha * x_ref[...] + m_ref[g_row:g_row + 1, :] * moe
            res = _layer_norm(z, lg_ref[...], lb_ref[...])
            if npt is None:
                out_refs[0][...] = res
            else:
                @pl.when(i < npt)
                def _():
                    out_refs[0][...] = res

                @pl.when(i >= npt)
                def _():
                    out_refs[1][...] = res

            if n_tiles > 1:
                @pl.when(i == n_tiles - 1)
                def _():
                    g.rows_loop(1 - slot, False)


def _combine_ln(x, ys, dest, gts, modt, ln_g, ln_b, *, g_row, alpha, split_rows=None):
    T, D = x.shape
    n_tiles, _, tm = dest.shape
    mod4, mod_map = modt[0], modt[1](tm)
    row = pl.BlockSpec((tm, D), lambda i: (i, 0))
    vec = pl.BlockSpec((1, D), lambda i: (0, 0))
    hbm = pl.BlockSpec(memory_space=pl.ANY)
    if split_rows is None:
        npt, out_specs, out_shape = None, row, jax.ShapeDtypeStruct((T, D), F32)
    else:
        npt = split_rows // tm
        out_specs = [pl.BlockSpec((tm, D), lambda i: (jnp.minimum(i, npt - 1), 0)),
                     pl.BlockSpec((tm, D), lambda i: (jnp.maximum(i - npt, 0), 0))]
        out_shape = [jax.ShapeDtypeStruct((split_rows, D), F32), jax.ShapeDtypeStruct((T - split_rows, D), F32)]
    return pl.pallas_call(
        functools.partial(_combine_body, g_row=g_row, alpha=alpha, tm=tm, n_tiles=n_tiles, npt=npt),
        grid=(n_tiles,),
        in_specs=[row, hbm, hbm, pl.BlockSpec((tm, LANE), lambda i: (i, 0)),
                  pl.BlockSpec((None, None, 6, D), lambda i: mod_map(i)), vec, vec],
        out_specs=out_specs,
        out_shape=out_shape,
        scratch_shapes=[pltpu.VMEM((tm * TOK_ROWS, LANE), F32)] * 4 + [
                        pltpu.SMEM((2, 2, tm), jnp.int32),
                        pltpu.SemaphoreType.DMA((2,)), pltpu.SemaphoreType.DMA((2, 2))],
        compiler_params=_cparams(1),
        name="combine_ln",
    )(x, dest, ys, gts, mod4, ln_g.reshape(1, D), ln_b.reshape(1, D))


def _route_meta(ids, tm, tm_tok):
    T = ids.shape[0]
    E = N_EXPERTS
    e_flat = ids.reshape(-1)
    onehot = (e_flat[:, None] == jnp.arange(E, dtype=jnp.int32)[None, :]).astype(jnp.int32)
    csum = jnp.cumsum(onehot, axis=0)
    rank = jnp.sum(csum * onehot, axis=1) - 1
    counts = csum[-1]
    padded = ((counts + tm - 1) // tm) * tm
    ends = jnp.cumsum(padded)
    starts = ends - padded
    dest = jnp.sum(starts[None, :] * onehot, axis=1) + rank
    n_rows = 2 * T + E * tm
    src_tok = jnp.zeros((n_rows,), jnp.int32).at[dest].set(jnp.arange(2 * T, dtype=jnp.int32) // 2,
                                                            unique_indices=True, mode="promise_in_bounds")
    tile_start = jnp.arange(n_rows // tm, dtype=jnp.int32) * tm
    valid = (tile_start < ends[-1]).astype(jnp.int32)
    probe = jnp.minimum(tile_start, ends[-1] - 1)
    tile_e = jnp.sum((ends[None, :] <= probe[:, None]).astype(jnp.int32), axis=1)
    dest_t = dest.reshape(T // tm_tok, tm_tok, 2).transpose(0, 2, 1)
    return (dest_t * TOK_ROWS, src_tok.reshape(n_rows // tm, 1, tm) * TOK_ROWS, jnp.minimum(tile_e, E - 1), valid)


def _pos_embed_2d(n_tokens, dim):
    rows = n_tokens // GRID_W
    quarter = dim // 4
    freqs = jnp.exp(-math.log(10000.0) * jnp.arange(quarter, dtype=F32) / quarter)
    r = jnp.broadcast_to(jnp.arange(rows, dtype=F32)[:, None], (rows, GRID_W)).reshape(-1)
    col = jnp.broadcast_to(jnp.arange(GRID_W, dtype=F32)[None, :], (rows, GRID_W)).reshape(-1)
    ar = r[:, None] * freqs
    ac = col[:, None] * freqs
    return jnp.concatenate([jnp.sin(ar), jnp.cos(ar), jnp.sin(ac), jnp.cos(ac)], -1)


def _pad_cols(w, n):
    return jnp.pad(w, ((0, 0),) * (w.ndim - 1) + ((0, n - w.shape[-1]),))


def kernel(x_prompt, x_sample, state_lru, state_mlstm_c, state_mlstm_n, state_mlstm_m, state_gla, c, c_ctx, mod_w, mod_b, ln_g, ln_b, even_w_in, even_w_out, lru_conv_w, lru_conv_b, lru_gate_w, lru_gate_b, lru_lambda, mlstm_gate_b, mlstm_norm_g, odd_w_in, odd_w_out, gla_gate_w, gla_gate_b, gla_norm_g, router_w, router_b, moe_w_in, moe_w_out):
    Bp, Lp, D = x_prompt.shape
    Bs, Ls, _ = x_sample.shape
    depth = mod_w.shape[0]
    Tp, Ts = Bp * Lp, Bs * Ls
    T = Tp + Ts
    alpha = (2 * depth) ** 0.25
    D_A = lru_conv_w.shape[-1]
    D_B = mlstm_norm_g.shape[-1]
    DK = gla_gate_w.shape[-1]
    DV = gla_norm_g.shape[-1]
    tm = next(t for t in (512, 256, 128) if Tp % t == 0 and Ls % t == 0)
    tn = 1024

    n_cond = 1 + Bs
    R = -(-n_cond // SUBLANE) * SUBLANE
    cond = jnp.zeros((R, D), F32).at[0].set(c_ctx).at[1:n_cond].set(c)
    mod_all = _modulation(cond, mod_w, mod_b)
    mod4 = mod_all.reshape(depth, R, 6, D)
    tm_proj = next(t for t in (1024, 512, 256, 128) if Tp % t == 0 and Ls % t == 0)
    assert D == TOK_ROWS * LANE

    x = _embed(x_prompt.reshape(Tp, D), x_sample.reshape(Ts, D), _pos_embed_2d(Ls, D), tm)

    moe_w_in_b = moe_w_in.astype(BF16)
    moe_w_out_b = moe_w_out.astype(BF16)
    groups = ((0, Bp, Lp), (Tp, Bs, Ls))

    s_lru, s_c, s_n, s_m, s_gla = [], [], [], [], []
    for l in range(depth):
        j = l // 2
        mt = (mod4, lambda t, l=l: (lambda i: (l, jnp.where(i < Tp // t, 0, 1 + (i - Tp // t) // (Ls // t)), 0, 0)))
        if l % 2 == 0:
            w_in = even_w_in[j]
            n_main = 2 * D_A + 4 * D_B
            proj, gates = _proj(x, mt, w_in[:, :n_main].astype(BF16), _pad_cols(w_in[:, n_main:], LANE),
                                sh_row=0, sc_row=1, tm=tm_proj, tn=tn)
            gate_bias = _pad_cols(mlstm_gate_b[j].reshape(1, -1), LANE)
            ya, hs = [], []
            for gi, (row0, nseq, L) in enumerate(groups):
                if gi == 0:
                    lru0 = jnp.zeros((nseq, 2, D_A), F32)
                    c0 = jnp.zeros((nseq, 2, B_HEADS, D_B // B_HEADS, D_B // B_HEADS), F32)
                    n0 = jnp.zeros((nseq, 2, B_HEADS, D_B // B_HEADS), F32)
                    m0 = jnp.zeros((nseq, 2, B_HEADS), F32)
                else:
                    lru0, c0, n0, m0 = state_lru[:, j], state_mlstm_c[:, j], state_mlstm_n[:, j], state_mlstm_m[:, j]
                y_a, st = _lru(proj, row0, nseq, L, lru_conv_w[j], lru_conv_b[j], lru_gate_w[j], lru_gate_b[j],
                               lru_lambda[j], lru0)
                h, C, n, m = _mlstm(proj, gates, gate_bias, row0, nseq, L, 2 * D_A, c0, n0, m0)
                ya.append(y_a)
                hs.append(h)
                if gi == 0:
                    s_lru.append(st)
                    s_c.append(C)
                    s_n.append(n.reshape(nseq, 2, B_HEADS, -1))
                    s_m.append(m.reshape(nseq, 2, B_HEADS))
            h_fwd, h_bwd = zip(*hs)
            y_b = _mlstm_out(h_fwd, h_bwd, proj, 2 * D_A + 3 * D_B, mlstm_norm_g[j], tm)
            w_out = even_w_out[j].astype(BF16)
            x, hf, ids, gts = _outproj_ln([tuple(ya), y_b], [w_out[:D_A], w_out[D_A:]], x, mt, ln_g[l, 0],
                                          ln_b[l, 0], router_w, router_b, g_row=2, alpha=alpha, tm=tm)
        else:
            w_in = odd_w_in[j]
            n_main = 2 * DK + 2 * DV
            proj, lr = _proj(x, mt, w_in[:, :n_main].astype(BF16), _pad_cols(w_in[:, n_main:], LANE),
                             sh_row=0, sc_row=1, tm=tm_proj, tn=tn)
            gw_pad = jnp.zeros((2, LANE, DK), F32)
            for dd in range(2):
                gw_pad = gw_pad.at[dd, dd * GLA_RANK:(dd + 1) * GLA_RANK].set(gla_gate_w[j, dd])
            os_ = []
            for gi, (row0, nseq, L) in enumerate(groups):
                s0 = (jnp.zeros((nseq, 2, C_HEADS, DK // C_HEADS, DV // C_HEADS), F32) if gi == 0
                      else state_gla[:, j])
                o, S = _gla(proj, lr, gw_pad, gla_gate_b[j], row0, nseq, L, s0)
                os_.append(o)
                if gi == 0:
                    s_gla.append(S)
            o_fwd, o_bwd = zip(*os_)
            y = _gla_out(o_fwd, o_bwd, proj, 2 * DK + DV, gla_norm_g[j], tm)
            x, hf, ids, gts = _outproj_ln([y], [odd_w_out[j].astype(BF16)], x, mt, ln_g[l, 0], ln_b[l, 0],
                                          router_w, router_b, g_row=2, alpha=alpha, tm=tm)

        gts = _pad_cols(gts[:2].T, LANE)
        dest, src_tok, tile_e, tile_valid = _route_meta(ids[:2].T, MOE_TM, tm)
        ys = _expert_ffn(hf, src_tok, moe_w_in_b, moe_w_out_b, l, tile_e, tile_valid)
        x = _combine_ln(x, ys, dest, gts, mt, ln_g[l, 1], ln_b[l, 1], g_row=5, alpha=alpha,
                        split_rows=Tp if l == depth - 1 else None)

    y_prompt = x[0].reshape(Bp, Lp, D)
    y_sample = x[1].reshape(Bs, Ls, D)
    return (y_prompt, y_sample, jnp.stack(s_lru, 1), jnp.stack(s_c, 1), jnp.stack(s_n, 1),
            jnp.stack(s_m, 1), jnp.stack(s_gla, 1))
```

```python
import functools
import math

import jax
import jax.numpy as jnp
from jax import lax
from jax.experimental import pallas as pl
from jax.experimental.pallas import tpu as pltpu

F32 = jnp.float32
BF16 = jnp.bfloat16
HI = lax.Precision.HIGHEST

LN_EPS = 1e-5
LRU_C = 8.0
A_BLOCKS = 8
B_HEADS = 4
C_HEADS = 4
GLA_RANK = 16
GLA_TAU = 16.0
N_EXPERTS = 16
N_GROUPS = 4
GRID_W = 64
LANE = 128
SUBLANE = 8
SCAN_CHUNK = 256
SCAN_OUT_DTYPE = jnp.bfloat16
MOE_TM = 512
FFN_PARTS = 4
FFN_INLINE_START_SLOTS = (0,)
TOK_ROWS = 16
VMEM_LIMIT = 56 * 1024 * 1024

NT_DIMS = (((1,), (1,)), ((), ()))
TN_DIMS = (((0,), (0,)), ((), ()))


def _cparams(n_axes):
    return pltpu.CompilerParams(dimension_semantics=("arbitrary",) * n_axes,
                                vmem_limit_bytes=VMEM_LIMIT)


def _sigmoid(x):
    return 1.0 / (1.0 + jnp.exp(-x))


def _silu(x):
    return x * _sigmoid(x)


def _log_sigmoid(x):
    return jnp.minimum(x, 0.0) - jnp.log1p(jnp.exp(-jnp.abs(x)))


def _gelu_tanh(x):
    return 0.5 * x * (1.0 + jnp.tanh(math.sqrt(2.0 / math.pi) * (x + 0.044715 * (x * x * x))))


def _mod_body(c_ref, w_ref, b_ref, o_ref):
    c = c_ref[...]
    o_ref[...] = jnp.dot(_silu(c), w_ref[...], precision=HI, preferred_element_type=F32) + b_ref[...]


def _modulation(cond, mod_w, mod_b):
    R, D = cond.shape
    nl, _, N = mod_w.shape
    tn = 1024
    return pl.pallas_call(
        _mod_body,
        grid=(nl, N // tn),
        in_specs=[pl.BlockSpec((R, D), lambda l, j: (0, 0)),
                  pl.BlockSpec((None, D, tn), lambda l, j: (l, 0, j)),
                  pl.BlockSpec((None, 1, tn), lambda l, j: (l, 0, j))],
        out_specs=pl.BlockSpec((None, R, tn), lambda l, j: (l, 0, j)),
        out_shape=jax.ShapeDtypeStruct((nl, R, N), F32),
        compiler_params=_cparams(2),
        name="modulation",
    )(cond, mod_w, mod_b.reshape(nl, 1, N))


def _embed_body(xp_ref, xs_ref, pos_ref, o_ref, *, n_prompt_tiles):
    i = pl.program_id(0)

    @pl.when(i < n_prompt_tiles)
    def _():
        o_ref[...] = xp_ref[...]

    @pl.when(i >= n_prompt_tiles)
    def _():
        o_ref[...] = xs_ref[...] + pos_ref[...]


def _embed(xp, xs, pos, tm):
    Tp, D = xp.shape
    Ts = xs.shape[0]
    Ls = pos.shape[0]
    npt, nst, npos = Tp // tm, Ts // tm, Ls // tm
    return pl.pallas_call(
        functools.partial(_embed_body, n_prompt_tiles=npt),
        grid=(npt + nst,),
        in_specs=[pl.BlockSpec((tm, D), lambda i: (jnp.minimum(i, npt - 1), 0)),
                  pl.BlockSpec((tm, D), lambda i: (jnp.maximum(i - npt, 0), 0)),
                  pl.BlockSpec((tm, D), lambda i: (jnp.maximum(i - npt, 0) % npos, 0))],
        out_specs=pl.BlockSpec((tm, D), lambda i: (i, 0)),
        out_shape=jax.ShapeDtypeStruct((Tp + Ts, D), F32),
        compiler_params=_cparams(1),
        name="embed",
    )(xp, xs, pos)


def _split3(x):
    hi = x.astype(BF16)
    r1 = x - hi.astype(F32)
    mid = r1.astype(BF16)
    return hi, mid, (r1 - mid.astype(F32)).astype(BF16)


def _proj_body(x_ref, m_ref, w_ref, wg_ref, o_ref, og_ref, h_ref, *, sh_row, sc_row):
    @pl.when(pl.program_id(1) == 0)
    def _():
        h = x_ref[...] * (1.0 + m_ref[sc_row:sc_row + 1, :]) + m_ref[sh_row:sh_row + 1, :]
        h_hi, h_mid, _ = _split3(h)
        h_ref[...] = h_hi
        wg_hi, wg_mid, _ = _split3(wg_ref[...])
        both = jnp.dot(h_hi, jnp.concatenate([wg_hi, wg_mid], axis=1), preferred_element_type=F32)
        og_ref[...] = (both[:, :LANE] + both[:, LANE:]
                       + jnp.dot(h_mid, wg_hi, preferred_element_type=F32))

    o_ref[...] = jnp.dot(h_ref[...], w_ref[...], preferred_element_type=F32)


def _proj(x, modt, w, w_gate, *, sh_row, sc_row, tm, tn):
    T, D = x.shape
    N = w.shape[1]
    mod4, mod_map = modt[0], modt[1](tm)
    return pl.pallas_call(
        functools.partial(_proj_body, sh_row=sh_row, sc_row=sc_row),
        grid=(T // tm, N // tn),
        in_specs=[pl.BlockSpec((tm, D), lambda i, j: (i, 0)),
                  pl.BlockSpec((None, None, 6, D), lambda i, j: mod_map(i)),
                  pl.BlockSpec((D, tn), lambda i, j: (0, j)),
                  pl.BlockSpec((D, LANE), lambda i, j: (0, 0))],
        out_specs=[pl.BlockSpec((tm, tn), lambda i, j: (i, j)),
                   pl.BlockSpec((tm, LANE), lambda i, j: (i, 0))],
        out_shape=[jax.ShapeDtypeStruct((T, N), F32), jax.ShapeDtypeStruct((T, LANE), F32)],
        scratch_shapes=[pltpu.VMEM((tm, D), BF16)],
        compiler_params=_cparams(2),
        name="proj",
    )(x, mod4, w, w_gate)


def _lru_body(ag_ref, ax_ref, cw_ref, cb_ref, gw_ref, gb_ref, lam_ref, h0_ref, y_ref, st_ref,
              af_s, uf_s, ab_s, ub_s, hf_s, hb_s, *, L):
    x = ax_ref[...]
    row = lax.broadcasted_iota(jnp.int32, (L, LANE), 0)
    xm1 = jnp.where(row >= 1, pltpu.roll(x, 1, 0), 0.0)
    xp1 = jnp.where(row < L - 1, pltpu.roll(x, L - 1, 0), 0.0)
    xp2 = jnp.where(row < L - 2, pltpu.roll(x, L - 2, 0), 0.0)
    xc = cb_ref[...] + xm1 * cw_ref[0:1, :] + x * cw_ref[1:2, :] + xp1 * cw_ref[2:3, :] + xp2 * cw_ref[3:4, :]
    xcb = xc.astype(BF16)
    nj = L // SUBLANE
    sub = lax.broadcasted_iota(jnp.int32, (nj, SUBLANE, LANE), 1)

    for d, (a_s, u_s) in enumerate(((af_s, uf_s), (ab_s, ub_s))):
        g = jnp.dot(xcb, gw_ref[d], preferred_element_type=F32) + gb_ref[d]
        r = _sigmoid(g[:, :LANE])
        ig = _sigmoid(g[:, LANE:])
        nlam = -lam_ref[d]
        softplus = jnp.maximum(nlam, 0.0) + jnp.log1p(jnp.exp(-jnp.abs(nlam)))
        log_a = (-LRU_C * softplus) * r
        a = jnp.exp(log_a)
        u = jnp.sqrt(jnp.tanh(-log_a) * (1.0 + a * a)) * ig * xc
        a3 = a.reshape(nj, SUBLANE, LANE)
        u3 = u.reshape(nj, SUBLANE, LANE)
        for k in (1, 2, 4):
            if d == 0:
                sh, keep = k, sub >= k
            else:
                sh, keep = SUBLANE - k, sub < SUBLANE - k
            a_sh = pltpu.roll(a3, sh, 1)
            u_sh = pltpu.roll(u3, sh, 1)
            u3 = jnp.where(keep, a3 * u_sh + u3, u3)
            a3 = jnp.where(keep, a3 * a_sh, a3)
        a_s[...] = a3.reshape(L, LANE)
        u_s[...] = u3.reshape(L, LANE)

    def carry(j, hs):
        hf, hb = hs
        rf = pl.multiple_of(j * SUBLANE, SUBLANE)
        rb = pl.multiple_of((nj - 1 - j) * SUBLANE, SUBLANE)
        of = af_s[pl.ds(rf, SUBLANE), :] * hf + uf_s[pl.ds(rf, SUBLANE), :]
        ob = ab_s[pl.ds(rb, SUBLANE), :] * hb + ub_s[pl.ds(rb, SUBLANE), :]
        hf_s[pl.ds(rf, SUBLANE), :] = of
        hb_s[pl.ds(rb, SUBLANE), :] = ob
        return (jnp.broadcast_to(of[SUBLANE - 1:SUBLANE, :], (SUBLANE, LANE)),
                jnp.broadcast_to(ob[0:1, :], (SUBLANE, LANE)))

    h0f = jnp.broadcast_to(h0_ref[0:1, :], (SUBLANE, LANE))
    h0b = jnp.broadcast_to(h0_ref[1:2, :], (SUBLANE, LANE))
    lax.fori_loop(0, nj, carry, (h0f, h0b))
    y_ref[...] = (_gelu_tanh(ag_ref[...]) * (hf_s[...] + hb_s[...])).astype(y_ref.dtype)
    st_ref[0:1, :] = hf_s[L - 1:L, :]
    st_ref[1:2, :] = hb_s[0:1, :]


def _lru(proj, row0, nseq, L, conv_w, conv_b, gate_w, gate_b, lam, h0):
    D_A = conv_w.shape[1]
    nb = D_A // LANE
    rb0 = row0 // L
    in_specs = [pl.BlockSpec((L, LANE), lambda s, h: (rb0 + s, h)),
                pl.BlockSpec((L, LANE), lambda s, h: (rb0 + s, nb + h)),
                pl.BlockSpec((4, LANE), lambda s, h: (0, h)),
                pl.BlockSpec((1, LANE), lambda s, h: (0, h)),
                pl.BlockSpec((2, None, LANE, 2 * LANE), lambda s, h: (0, h, 0, 0)),
                pl.BlockSpec((2, None, 1, 2 * LANE), lambda s, h: (0, h, 0, 0)),
                pl.BlockSpec((2, None, 1, LANE), lambda s, h: (0, h, 0, 0)),
                pl.BlockSpec((None, 2, LANE), lambda s, h: (s, 0, h))]
    args = [proj, proj, conv_w, conv_b.reshape(1, D_A), gate_w.astype(BF16),
            gate_b.reshape(2, nb, 1, 2 * LANE), lam.reshape(2, nb, 1, LANE), h0]
    return pl.pallas_call(
        functools.partial(_lru_body, L=L),
        grid=(nseq, nb),
        in_specs=in_specs,
        out_specs=[pl.BlockSpec((L, LANE), lambda s, h: (s, h)),
                   pl.BlockSpec((None, 2, LANE), lambda s, h: (s, 0, h))],
        out_shape=[jax.ShapeDtypeStruct((nseq * L, D_A), BF16),
                   jax.ShapeDtypeStruct((nseq, 2, D_A), F32)],
        scratch_shapes=[pltpu.VMEM((L, LANE), F32)] * 6,
        compiler_params=_cparams(2),
        name="rglru",
    )(*args)


def _tri_mask(Tc, d):
    r = lax.broadcasted_iota(jnp.int32, (Tc, Tc), 0)
    c = lax.broadcasted_iota(jnp.int32, (Tc, Tc), 1)
    return (c - r) * (1 - 2 * d) <= 0


def _mlstm_dir(d, q_ref, k_ref, v_ref, g_ref, gb_ref, h_ref, C_s, n_s, m_s, *, Tc, k_scale, dh):
    H = B_HEADS
    last = Tc - 1 if d == 0 else 0
    lane = lax.broadcasted_iota(jnp.int32, (Tc, LANE), 1)
    G = g_ref[...] + gb_ref[...]
    tri = _tri_mask(Tc, d)
    Bm = jnp.dot(tri.astype(F32), _log_sigmoid(G), precision=HI, preferred_element_type=F32)
    col0 = d * (2 * H)
    X = jnp.where(lane >= col0 + H, Bm, G)
    srow = lax.broadcasted_iota(jnp.int32, (2 * H, LANE), 0)
    slane = lax.broadcasted_iota(jnp.int32, (2 * H, LANE), 1)
    sel = (slane == col0 + srow).astype(F32)
    R = lax.dot_general(sel, X, NT_DIMS, precision=HI, preferred_element_type=F32)
    for hd in range(H):
        i_col = jnp.sum(jnp.where(lane == col0 + hd, X, 0.0), axis=1, keepdims=True)
        b_col = jnp.sum(jnp.where(lane == col0 + H + hd, X, 0.0), axis=1, keepdims=True)
        i_row = R[hd:hd + 1, :]
        b_row = R[H + hd:H + hd + 1, :]
        dmat = jnp.where(tri, b_col - b_row + i_row, -jnp.inf)
        m_prev = m_s[d, hd]
        m_inter = b_col + m_prev
        m_t = jnp.maximum(m_inter, jnp.max(dmat, axis=1, keepdims=True))
        cs = slice(hd * dh, (hd + 1) * dh)
        qf = q_ref[:, cs]
        qb = qf.astype(BF16)
        kf = k_ref[:, cs] * k_scale
        kb = kf.astype(BF16)
        vb = v_ref[:, cs].astype(BF16)
        S = lax.dot_general(qb, kb, NT_DIMS, preferred_element_type=F32) * jnp.exp(dmat - m_t)
        inter_scale = jnp.exp(m_inter - m_t)
        Cm = C_s[d, hd]
        num =(jnp.dot(S.astype(BF16), vb, preferred_element_type=F32)
               + inter_scale * jnp.dot(qb, Cm.astype(BF16), preferred_element_type=F32))
        nv = n_s[d, hd]
        qn = jnp.sum(qf * nv, axis=1, keepdims=True)
        den = jnp.sum(S, axis=1, keepdims=True) + inter_scale * qn
        h_ref[:, cs] = (num / jnp.maximum(jnp.abs(den), jnp.exp(-m_t))).astype(h_ref.dtype)
        bL = b_col[last:last + 1, :]
        g_col = bL - b_col + i_col
        m_new = jnp.maximum(bL + m_prev, jnp.max(g_col, axis=0, keepdims=True))
        wk = jnp.exp(g_col - m_new)
        decay = jnp.exp(bL + m_prev - m_new)
        kw = kf * wk
        C_s[d, hd] = decay * Cm + lax.dot_general(kw.astype(BF16), vb, TN_DIMS, preferred_element_type=F32)
        n_s[d, hd] = decay * nv + jnp.sum(kw, axis=0, keepdims=True)
        m_s[d, hd] = m_new


def _mlstm_body(qf_ref, kf_ref, vf_ref, gf_ref, qb_ref, kb_ref, vb_ref, gbk_ref, gb_ref, c0_ref, n0_ref, m0_ref,
                hf_ref, hb_ref, co_ref, no_ref, mo_ref, C_s, n_s, m_s, *, nc, **kw):
    c = pl.program_id(1)

    @pl.when(c == 0)
    def _():
        C_s[...] = c0_ref[...]
        n_s[...] = n0_ref[...]
        m_s[...] = m0_ref[...]

    _mlstm_dir(0, qf_ref, kf_ref, vf_ref, gf_ref, gb_ref, hf_ref, C_s, n_s, m_s, **kw)
    _mlstm_dir(1, qb_ref, kb_ref, vb_ref, gbk_ref, gb_ref, hb_ref, C_s, n_s, m_s, **kw)

    @pl.when(c == nc - 1)
    def _():
        co_ref[...] = C_s[...]
        no_ref[...] = n_s[...]
        mo_ref[...] = m_s[...]


def _mlstm(proj, gates, gate_bias, row0, nseq, L, col0, c0, n0, m0):
    H = B_HEADS
    dh = c0.shape[-1]
    D_B = H * dh
    Tc = min(SCAN_CHUNK, L)
    nc = L // Tc
    rb0 = row0 // Tc
    cb0 = col0 // D_B
    fwd = lambda s, c: s * nc + c
    bwd = lambda s, c: s * nc + nc - 1 - c

    def chunk_specs(rb):
        return [pl.BlockSpec((Tc, D_B), lambda s, c, off=off: (rb0 + rb(s, c), cb0 + off)) for off in range(3)] + [
            pl.BlockSpec((Tc, LANE), lambda s, c: (rb0 + rb(s, c), 0))]

    def st_specs():
        st_map = lambda s, c: (s, 0, 0, 0, 0)
        return [pl.BlockSpec((None, 2, H, dh, dh), st_map), pl.BlockSpec((None, 2, H, 1, dh), st_map),
                pl.BlockSpec((None, 2, H, 1, 1), st_map)]

    hf, hb, C, n, m = pl.pallas_call(
        functools.partial(_mlstm_body, Tc=Tc, nc=nc, k_scale=dh ** -0.5, dh=dh),
        grid=(nseq, nc),
        in_specs=chunk_specs(fwd) + chunk_specs(bwd) + [pl.BlockSpec((1, LANE), lambda s, c: (0, 0))] + st_specs(),
        out_specs=[pl.BlockSpec((Tc, D_B), lambda s, c: (fwd(s, c), 0)),
                   pl.BlockSpec((Tc, D_B), lambda s, c: (bwd(s, c), 0))] + st_specs(),
        out_shape=[jax.ShapeDtypeStruct((nseq * L, D_B), SCAN_OUT_DTYPE),
                   jax.ShapeDtypeStruct((nseq * L, D_B), SCAN_OUT_DTYPE),
                   jax.ShapeDtypeStruct((nseq, 2, H, dh, dh), F32),
                   jax.ShapeDtypeStruct((nseq, 2, H, 1, dh), F32),
                   jax.ShapeDtypeStruct((nseq, 2, H, 1, 1), F32)],
        scratch_shapes=[pltpu.VMEM((2, H, dh, dh), F32), pltpu.VMEM((2, H, 1, dh), F32),
                        pltpu.VMEM((2, H, 1, 1), F32)],
        compiler_params=_cparams(2),
        name="mlstm",
    )(*([proj] * 3 + [gates]) * 2, gate_bias, c0, n0.reshape(nseq, 2, H, 1, dh), m0.reshape(nseq, 2, H, 1, 1))
    return (hf, hb), C, n, m


def _pair_specs(pair, block, tm):
    npt = pair[0].shape[-2] // tm
    (shape, _), mk = block(0), block
    return npt, [pl.BlockSpec(shape, lambda i, *rest: mk(jnp.minimum(i, npt - 1), *rest)[1]),
                 pl.BlockSpec(shape, lambda i, *rest: mk(jnp.maximum(i - npt, 0), *rest)[1])]


def _pick(p_ref, s_ref, npt):
    return jnp.where(pl.program_id(0) < npt, p_ref[...], s_ref[...])


def _mlstm_out_body(hfp_ref, hfs_ref, hbp_ref, hbs_ref, o_ref, g_ref, y_ref, *, npt, dh):
    h = _pick(hfp_ref, hfs_ref, npt).astype(F32) + _pick(hbp_ref, hbs_ref, npt).astype(F32)
    for hd in range(B_HEADS):
        cs = slice(hd * dh, (hd + 1) * dh)
        hm = h[:, cs]
        mu = jnp.mean(hm, axis=1, keepdims=True)
        xc = hm - mu
        var = jnp.mean(xc * xc, axis=1, keepdims=True)
        y_ref[:, cs] = (_sigmoid(o_ref[:, cs]) * (xc * lax.rsqrt(var + LN_EPS) * g_ref[:, cs])).astype(y_ref.dtype)


def _mlstm_out(h_fwd, h_bwd, proj, col0, norm_g, tm):
    T = proj.shape[0]
    D_B = h_fwd[0].shape[-1]
    cb0 = col0 // D_B
    blk = lambda r: ((tm, D_B), (r, 0))
    npt, f_specs = _pair_specs(h_fwd, blk, tm)
    _, b_specs = _pair_specs(h_bwd, blk, tm)
    return pl.pallas_call(
        functools.partial(_mlstm_out_body, npt=npt, dh=D_B // B_HEADS),
        grid=(T // tm,),
        in_specs=f_specs + b_specs + [pl.BlockSpec((tm, D_B), lambda i: (i, cb0)),
                                      pl.BlockSpec((1, D_B), lambda i: (0, 0))],
        out_specs=pl.BlockSpec((tm, D_B), lambda i: (i, 0)),
        out_shape=jax.ShapeDtypeStruct((T, D_B), BF16),
        compiler_params=_cparams(1),
        name="mlstm_out",
    )(*h_fwd, *h_bwd, proj, norm_g.reshape(1, D_B))


def _gla_dir(d, q_ref, k_ref, v_ref, lr_ref, gw_ref, gb_ref, o_ref, ST_s, *, Tc, q_scale, dk, dv):
    H = C_HEADS
    last = Tc - 1 if d == 0 else 0
    lr_hi, lr_mid, _ = _split3(lr_ref[...])
    gw_hi, gw_mid, _ = _split3(gw_ref[d])
    z = (jnp.dot(lr_hi, gw_hi, preferred_element_type=F32)
         + jnp.dot(lr_hi, gw_mid, preferred_element_type=F32)
         + jnp.dot(lr_mid, gw_hi, preferred_element_type=F32)) + gb_ref[d]
    loga = _log_sigmoid(z) * (1.0 / GLA_TAU)
    tri = _tri_mask(Tc, d)
    trib = tri.astype(F32).astype(BF16)
    b = sum(jnp.dot(trib, piece, preferred_element_type=F32) for piece in _split3(loga))
    qs = (q_ref[...] * q_scale * jnp.exp(b)).astype(BF16)
    kf = k_ref[...]
    ke = (kf * jnp.exp(-b)).astype(BF16)
    bL = b[last:last + 1, :]
    kd = (kf * jnp.exp(bL - b)).astype(BF16)
    ebL = jnp.exp(bL)
    for hd in range(H):
        ks = slice(hd * dk, (hd + 1) * dk)
        vs = slice(hd * dv, (hd + 1) * dv)
        vb = v_ref[:, vs].astype(BF16)
        att = jnp.where(tri, lax.dot_general(qs[:, ks], ke[:, ks], NT_DIMS, preferred_element_type=F32), 0.0)
        ST = ST_s[d, hd]
        inter = lax.dot_general(qs[:, ks], ST.astype(BF16), NT_DIMS, preferred_element_type=F32)
        o_ref[:, vs] = (inter + jnp.dot(att.astype(BF16), vb, preferred_element_type=F32)).astype(o_ref.dtype)
        ST_s[d, hd] = ST * ebL[:, ks] + lax.dot_general(vb, kd[:, ks], TN_DIMS, preferred_element_type=F32)


def _gla_body(qf_ref, kf_ref, vf_ref, lrf_ref, qb_ref, kb_ref, vb_ref, lrb_ref, gw_ref, gb_ref, s0_ref,
              of_ref, ob_ref, so_ref, ST_s, *, nc, **kw):
    c = pl.program_id(1)
    states = [(d, hd) for d in range(2) for hd in range(C_HEADS)]

    @pl.when(c == 0)
    def _():
        for d, hd in states:
            ST_s[d, hd] = s0_ref[d, hd].T

    _gla_dir(0, qf_ref, kf_ref, vf_ref, lrf_ref, gw_ref, gb_ref, of_ref, ST_s, **kw)
    _gla_dir(1, qb_ref, kb_ref, vb_ref, lrb_ref, gw_ref, gb_ref, ob_ref, ST_s, **kw)

    @pl.when(c == nc - 1)
    def _():
        for d, hd in states:
            so_ref[d, hd] = ST_s[d, hd].T


def _gla(proj, lr, gw_pad, gate_b, row0, nseq, L, s0):
    H = C_HEADS
    dk, dv = s0.shape[-2:]
    DK, DV = H * dk, H * dv
    Tc = min(SCAN_CHUNK, L)
    nc = L // Tc
    rb0 = row0 // Tc
    fwd = lambda s, c: s * nc + c
    bwd = lambda s, c: s * nc + nc - 1 - c

    def chunk_specs(rb):
        return [pl.BlockSpec((Tc, DK), lambda s, c: (rb0 + rb(s, c), 0)),
                pl.BlockSpec((Tc, DK), lambda s, c: (rb0 + rb(s, c), 1)),
                pl.BlockSpec((Tc, DV), lambda s, c: (rb0 + rb(s, c), (2 * DK) // DV)),
                pl.BlockSpec((Tc, LANE), lambda s, c: (rb0 + rb(s, c), 0))]

    st_map = lambda s, c: (s, 0, 0, 0, 0)
    of, ob, S = pl.pallas_call(
        functools.partial(_gla_body, Tc=Tc, nc=nc, q_scale=dk ** -0.5, dk=dk, dv=dv),
        grid=(nseq, nc),
        in_specs=chunk_specs(fwd) + chunk_specs(bwd) + [
            pl.BlockSpec((2, LANE, DK), lambda s, c: (0, 0, 0)),
            pl.BlockSpec((2, 1, DK), lambda s, c: (0, 0, 0)),
            pl.BlockSpec((None, 2, H, dk, dv), st_map)],
        out_specs=[pl.BlockSpec((Tc, DV), lambda s, c: (fwd(s, c), 0)),
                   pl.BlockSpec((Tc, DV), lambda s, c: (bwd(s, c), 0)),
                   pl.BlockSpec((None, 2, H, dk, dv), st_map)],
        out_shape=[jax.ShapeDtypeStruct((nseq * L, DV), SCAN_OUT_DTYPE),
                   jax.ShapeDtypeStruct((nseq * L, DV), SCAN_OUT_DTYPE),
                   jax.ShapeDtypeStruct((nseq, 2, H, dk, dv), F32)],
        scratch_shapes=[pltpu.VMEM((2, H, dv, dk), F32)],
        compiler_params=_cparams(2),
        name="gla",
    )(*([proj] * 3 + [lr]) * 2, gw_pad, gate_b.reshape(2, 1, DK), s0)
    return (of, ob), S


def _gla_out_body(ofp_ref, ofs_ref, obp_ref, obs_ref, g_ref, ng_ref, y_ref, *, npt, dv):
    o = _pick(ofp_ref, ofs_ref, npt).astype(F32) + _pick(obp_ref, obs_ref, npt).astype(F32)
    for hd in range(C_HEADS):
        vs = slice(hd * dv, (hd + 1) * dv)
        oo = o[:, vs]
        ms = jnp.mean(oo * oo, axis=1, keepdims=True)
        y_ref[:, vs] = (oo * lax.rsqrt(ms + LN_EPS) * ng_ref[:, vs] * _silu(g_ref[:, vs])).astype(y_ref.dtype)


def _gla_out(o_fwd, o_bwd, proj, col0, norm_g, tm):
    T = proj.shape[0]
    DV = o_fwd[0].shape[-1]
    cb0 = col0 // DV
    blk = lambda r: ((tm, DV), (r, 0))
    npt, f_specs = _pair_specs(o_fwd, blk, tm)
    _, b_specs = _pair_specs(o_bwd, blk, tm)
    return pl.pallas_call(
        functools.partial(_gla_out_body, npt=npt, dv=DV // C_HEADS),
        grid=(T // tm,),
        in_specs=f_specs + b_specs + [pl.BlockSpec((tm, DV), lambda i: (i, cb0)),
                                      pl.BlockSpec((1, DV), lambda i: (0, 0))],
        out_specs=pl.BlockSpec((tm, DV), lambda i: (i, 0)),
        out_shape=jax.ShapeDtypeStruct((T, DV), BF16),
        compiler_params=_cparams(1),
        name="gla_out",
    )(*o_fwd, *o_bwd, proj, norm_g.reshape(1, DV))


def _layer_norm(z, g, b):
    mu = jnp.mean(z, axis=1, keepdims=True)
    zc = z - mu
    var = jnp.mean(zc * zc, axis=1, keepdims=True)
    return zc * lax.rsqrt(var + LN_EPS) * g + b


def _outproj_body(*refs, npts, g_row, alpha):
    n = len(npts)
    n_y = sum(1 if p is None else 2 for p in npts)
    y_refs = list(refs[:n_y])
    ws = refs[n_y:n_y + n]
    x_ref, m_ref, lg_ref, lb_ref, rw_ref, rb_ref, o_ref, h_ref, ids_ref, gts_ref = refs[n_y + n:]
    acc = None
    for npt, w in zip(npts, ws):
        y = y_refs.pop(0)[...] if npt is None else _pick(y_refs.pop(0), y_refs.pop(0), npt)
        part = jnp.dot(y, w[...], preferred_element_type=F32)
        acc = part if acc is None else acc + part
    z = alpha * x_ref[...] + m_ref[g_row:g_row + 1, :] * acc
    x_new = _layer_norm(z, lg_ref[...], lb_ref[...])
    o_ref[...] = x_new
    _route(x_new, m_ref, rw_ref, rb_ref, h_ref, ids_ref, gts_ref, sh_row=g_row + 1, sc_row=g_row + 2)


def _outproj_ln(ys, ws, x, modt, ln_g, ln_b, router_w, router_b, *, g_row, alpha, tm):
    T, D = x.shape
    E = router_w.shape[1]
    mod4, mod_map = modt[0], modt[1](tm)
    y_specs, y_args, npts = [], [], []
    for y in ys:
        if isinstance(y, tuple):
            K = y[0].shape[1]
            npt, specs = _pair_specs(y, lambda r, K=K: ((tm, K), (r, 0)), tm)
            y_specs += specs
            y_args += list(y)
            npts.append(npt)
        else:
            y_specs.append(pl.BlockSpec((tm, y.shape[1]), lambda i: (i, 0)))
            y_args.append(y)
            npts.append(None)
    in_specs = (y_specs
                + [pl.BlockSpec(w.shape, lambda i: (0, 0)) for w in ws]
                + [pl.BlockSpec((tm, D), lambda i: (i, 0)),
                   pl.BlockSpec((None, None, 6, D), lambda i: mod_map(i)),
                   pl.BlockSpec((1, D), lambda i: (0, 0)),
                   pl.BlockSpec((1, D), lambda i: (0, 0)),
                   pl.BlockSpec((E, D), lambda i: (0, 0)),
                   pl.BlockSpec((E, 1), lambda i: (0, 0))])
    return pl.pallas_call(
        functools.partial(_outproj_body, npts=tuple(npts), g_row=g_row, alpha=alpha),
        grid=(T // tm,),
        in_specs=in_specs,
        out_specs=[pl.BlockSpec((tm, D), lambda i: (i, 0)),
                   pl.BlockSpec((tm * TOK_ROWS, LANE), lambda i: (i, 0)),
                   pl.BlockSpec((SUBLANE, tm), lambda i: (0, i)),
                   pl.BlockSpec((SUBLANE, tm), lambda i: (0, i))],
        out_shape=[jax.ShapeDtypeStruct((T, D), F32),
                   jax.ShapeDtypeStruct((T * TOK_ROWS, LANE), F32),
                   jax.ShapeDtypeStruct((SUBLANE, T), jnp.int32),
                   jax.ShapeDtypeStruct((SUBLANE, T), F32)],
        compiler_params=_cparams(1),
        name="outproj_ln",
    )(*y_args, *ws, x, mod4, ln_g.reshape(1, D), ln_b.reshape(1, D), router_w.T, router_b.reshape(E, 1))


def _route(x, m_ref, w_ref, b_ref, h_ref, ids_ref, gts_ref, *, sh_row, sc_row):
    h = x * (1.0 + m_ref[sc_row:sc_row + 1, :]) + m_ref[sh_row:sh_row + 1, :]
    _to_tok_blocks(h_ref, h)
    logits = lax.dot_general(w_ref[...], h, NT_DIMS, precision=HI, preferred_element_type=F32) + b_ref[...]
    E, tm = logits.shape
    row = lax.broadcasted_iota(jnp.int32, (E, tm), 0)
    mx = jnp.max(logits, axis=0, keepdims=True)
    ex = jnp.exp(logits - mx)
    p = ex / jnp.sum(ex, axis=0, keepdims=True)
    grp = row // (E // N_GROUPS)
    best = None
    for g in range(N_GROUPS):
        pg = jnp.where(grp == g, p, -1.0)
        v1 = jnp.max(pg, axis=0, keepdims=True)
        i1 = jnp.min(jnp.where(pg == v1, row, E), axis=0, keepdims=True)
        pg2 = jnp.where(row == i1, -1.0, pg)
        v2 = jnp.max(pg2, axis=0, keepdims=True)
        i2 = jnp.min(jnp.where(pg2 == v2, row, E), axis=0, keepdims=True)
        score = v1 + v2
        if best is None:
            best = (score, v1, i1, v2, i2)
        else:
            take = score > best[0]
            best = tuple(jnp.where(take, n, o) for n, o in zip((score, v1, i1, v2, i2), best))
    _, v1, i1, v2, i2 = best
    tot = v1 + v2
    slot = lax.broadcasted_iota(jnp.int32, (SUBLANE, tm), 0)
    ids_ref[...] = jnp.where(slot == 0, i1, jnp.where(slot == 1, i2, 0))
    gts_ref[...] = jnp.where(slot == 0, v1 / tot, jnp.where(slot == 1, v2 / tot, 0.0))


def _to_tok_blocks(ref, x):
    tm = x.shape[0]
    for s in range(TOK_ROWS):
        ref[pl.ds(s, tm, stride=TOK_ROWS), :] = x[:, s * LANE:(s + 1) * LANE]


def _tok_chunk(ref, s, tm):
    return ref[pl.ds(s, tm, stride=TOK_ROWS), :]


class _RowGather:
    def __init__(self, src_hbm, idx_hbm, idx_s, bufs, isem, gsem, tm):
        self.src, self.idx_hbm, self.idx_s, self.bufs = src_hbm, idx_hbm, idx_s, bufs
        self.isem, self.gsem, self.tm = isem, gsem, tm

    def _idx_copy(self, tile, slot):
        return pltpu.make_async_copy(self.idx_hbm.at[tile], self.idx_s.at[slot], self.isem.at[slot])

    def _row_copy(self, slot, a, r):
        row = pl.multiple_of(self.idx_s[slot, a, r], TOK_ROWS)
        dst = pl.multiple_of(r * TOK_ROWS, TOK_ROWS)
        return pltpu.make_async_copy(self.src.at[pl.ds(row, TOK_ROWS), :],
                                     self.bufs[a][slot].at[pl.ds(dst, TOK_ROWS), :], self.gsem.at[slot, a])

    def rows_loop(self, slot, start):
        def body(r, carry):
            for a in range(len(self.bufs)):
                cp = self._row_copy(slot, a, r)
                cp.start() if start else cp.wait()
            return carry
        lax.fori_loop(0, self.tm, body, 0, unroll=8)

    def rows_start_inline(self, slot):
        for r in range(self.tm):
            for a in range(len(self.bufs)):
                self._row_copy(slot, a, r).start()

    def prologue(self, n_tiles):
        self._idx_copy(0, 0).start()
        self._idx_copy(0, 0).wait()
        self.rows_loop(0, True)
        if n_tiles > 1:
            self._idx_copy(1, 1).start()

    def advance_indices(self, i, n_tiles):
        slot = i % 2

        @pl.when(i + 1 < n_tiles)
        def _():
            self._idx_copy(i + 1, 1 - slot).wait()

        @pl.when(i + 2 < n_tiles)
        def _():
            self._idx_copy(i + 2, slot).start()


def _ffn_body(te_ref, va_ref, idx_hbm, h_hbm, wi_ref, wo_ref, o_ref, xbuf0, xbuf1, x16, a16, idx_s, isem, gsem,
              *, F, tm, n_tiles):
    i = pl.program_id(0)
    valid = va_ref[i] != 0
    first_invalid = jnp.logical_and(jnp.logical_not(valid), va_ref[jnp.maximum(i - 1, 0)] != 0)
    xbufs = (xbuf0, xbuf1)
    g = _RowGather(h_hbm, idx_hbm, idx_s, [xbufs], isem, gsem, tm)

    @pl.when(i == 0)
    def _():
        g.prologue(n_tiles)

    for slot in (0, 1):
        here = jnp.logical_and(valid, i % 2 == slot)

        @pl.when(here)
        def _(slot=slot):
            g.rows_loop(slot, False)
            g.advance_indices(i, n_tiles)

        @pl.when(here)
        def _(slot=slot):
            if slot in FFN_INLINE_START_SLOTS:
                g.rows_start_inline(1 - slot)
            else:
                g.rows_loop(1 - slot, True)
            for s in range(TOK_ROWS):
                x16[:, s * LANE:(s + 1) * LANE] = _tok_chunk(xbufs[slot], s, tm).astype(BF16)
            x = x16[...]
            fc = F // FFN_PARTS
            for p in range(FFN_PARTS):
                u = jnp.dot(x, wi_ref[:, p * fc:(p + 1) * fc], preferred_element_type=F32)
                w = jnp.dot(x, wi_ref[:, F + p * fc:F + (p + 1) * fc], preferred_element_type=F32)
                a16[:, p * fc:(p + 1) * fc] = (_silu(u) * w).astype(BF16)
            _to_tok_blocks(o_ref, jnp.dot(a16[...], wo_ref[...], preferred_element_type=F32))

        @pl.when(jnp.logical_and(first_invalid, i % 2 == slot))
        def _(slot=slot):
            g.rows_loop(slot, False)

            @pl.when(i + 1 < n_tiles)
            def _():
                g._idx_copy(i + 1, 1 - slot).wait()

    @pl.when(jnp.logical_not(valid))
    def _():
        o_ref[...] = jnp.zeros_like(o_ref)


def _expert_ffn(h, src_tok, w_in, w_out, layer, tile_e, tile_valid):
    n_tiles, _, tm = src_tok.shape
    F, D = w_out.shape[2:]
    return pl.pallas_call(
        functools.partial(_ffn_body, F=F, tm=tm, n_tiles=n_tiles),
        grid_spec=pltpu.PrefetchScalarGridSpec(
            num_scalar_prefetch=2,
            grid=(n_tiles,),
            in_specs=[pl.BlockSpec(memory_space=pl.ANY),
                      pl.BlockSpec(memory_space=pl.ANY),
                      pl.BlockSpec((None, None, D, 2 * F), lambda i, te, va: (layer, te[i], 0, 0)),
                      pl.BlockSpec((None, None, F, D), lambda i, te, va: (layer, te[i], 0, 0))],
            out_specs=pl.BlockSpec((tm * TOK_ROWS, LANE), lambda i, te, va: (i, 0)),
            scratch_shapes=[pltpu.VMEM((tm * TOK_ROWS, LANE), F32), pltpu.VMEM((tm * TOK_ROWS, LANE), F32),
                            pltpu.VMEM((tm, D), BF16), pltpu.VMEM((tm, F), BF16), pltpu.SMEM((2, 1, tm), jnp.int32),
                            pltpu.SemaphoreType.DMA((2,)), pltpu.SemaphoreType.DMA((2, 1))]),
        out_shape=jax.ShapeDtypeStruct((n_tiles * tm * TOK_ROWS, LANE), F32),
        compiler_params=_cparams(1),
        name="expert_ffn",
    )(tile_e, tile_valid, src_tok, h, w_in, w_out)


def _combine_body(*refs, g_row, alpha, tm, n_tiles, npt):
    x_ref, idx_hbm, y_hbm, gt_ref, m_ref, lg_ref, lb_ref = refs[:7]
    out_refs = refs[7:-7]
    y0a, y0b, y1a, y1b, idx_s, isem, gsem = refs[-7:]
    i = pl.program_id(0)
    y0, y1 = (y0a, y0b), (y1a, y1b)
    g = _RowGather(y_hbm, idx_hbm, idx_s, [y0, y1], isem, gsem, tm)

    @pl.when(i == 0)
    def _():
        g.prologue(n_tiles)

    for slot in (0, 1):
        @pl.when(i % 2 == slot)
        def _(slot=slot):
            g.rows_loop(slot, False)
            g.advance_indices(i, n_tiles)
            if n_tiles > 1:
                g.rows_start_inline(1 - slot)
            gt = gt_ref[...]
            g0, g1 = gt[:, 0:1], gt[:, 1:2]
            moe = jnp.concatenate([g0 * _tok_chunk(y0[slot], s, tm) + g1 * _tok_chunk(y1[slot], s, tm)
                                   for s in range(TOK_ROWS)], axis=1)
            z = alpha * x_ref[...] + m_ref[g_row:g_row + 1, :] * moe
            res = _layer_norm(z, lg_ref[...], lb_ref[...])
            if npt is None:
                out_refs[0][...] = res
            else:
                @pl.when(i < npt)
                def _():
                    out_refs[0][...] = res

                @pl.when(i >= npt)
                def _():
                    out_refs[1][...] = res

            if n_tiles > 1:
                @pl.when(i == n_tiles - 1)
                def _():
                    g.rows_loop(1 - slot, False)


def _combine_ln(x, ys, dest, gts, modt, ln_g, ln_b, *, g_row, alpha, split_rows=None):
    T, D = x.shape
    n_tiles, _, tm = dest.shape
    mod4, mod_map = modt[0], modt[1](tm)
    row = pl.BlockSpec((tm, D), lambda i: (i, 0))
    vec = pl.BlockSpec((1, D), lambda i: (0, 0))
    hbm = pl.BlockSpec(memory_space=pl.ANY)
    if split_rows is None:
        npt, out_specs, out_shape = None, row, jax.ShapeDtypeStruct((T, D), F32)
    else:
        npt = split_rows // tm
        out_specs = [pl.BlockSpec((tm, D), lambda i: (jnp.minimum(i, npt - 1), 0)),
                     pl.BlockSpec((tm, D), lambda i: (jnp.maximum(i - npt, 0), 0))]
        out_shape = [jax.ShapeDtypeStruct((split_rows, D), F32), jax.ShapeDtypeStruct((T - split_rows, D), F32)]
    return pl.pallas_call(
        functools.partial(_combine_body, g_row=g_row, alpha=alpha, tm=tm, n_tiles=n_tiles, npt=npt),
        grid=(n_tiles,),
        in_specs=[row, hbm, hbm, pl.BlockSpec((tm, LANE), lambda i: (i, 0)),
                  pl.BlockSpec((None, None, 6, D), lambda i: mod_map(i)), vec, vec],
        out_specs=out_specs,
        out_shape=out_shape,
        scratch_shapes=[pltpu.VMEM((tm * TOK_ROWS, LANE), F32)] * 4 + [
                        pltpu.SMEM((2, 2, tm), jnp.int32),
                        pltpu.SemaphoreType.DMA((2,)), pltpu.SemaphoreType.DMA((2, 2))],
        compiler_params=_cparams(1),
        name="combine_ln",
    )(x, dest, ys, gts, mod4, ln_g.reshape(1, D), ln_b.reshape(1, D))


def _route_meta(ids, tm, tm_tok):
    T = ids.shape[0]
    E = N_EXPERTS
    e_flat = ids.reshape(-1)
    onehot = (e_flat[:, None] == jnp.arange(E, dtype=jnp.int32)[None, :]).astype(jnp.int32)
    csum = jnp.cumsum(onehot, axis=0)
    rank = jnp.sum(csum * onehot, axis=1) - 1
    counts = csum[-1]
    padded = ((counts + tm - 1) // tm) * tm
    ends = jnp.cumsum(padded)
    starts = ends - padded
    dest = jnp.sum(starts[None, :] * onehot, axis=1) + rank
    n_rows = 2 * T + E * tm
    src_tok = jnp.zeros((n_rows,), jnp.int32).at[dest].set(jnp.arange(2 * T, dtype=jnp.int32) // 2,
                                                            unique_indices=True, mode="promise_in_bounds")
    tile_start = jnp.arange(n_rows // tm, dtype=jnp.int32) * tm
    valid = (tile_start < ends[-1]).astype(jnp.int32)
    probe = jnp.minimum(tile_start, ends[-1] - 1)
    tile_e = jnp.sum((ends[None, :] <= probe[:, None]).astype(jnp.int32), axis=1)
    dest_t = dest.reshape(T // tm_tok, tm_tok, 2).transpose(0, 2, 1)
    return (dest_t * TOK_ROWS, src_tok.reshape(n_rows // tm, 1, tm) * TOK_ROWS, jnp.minimum(tile_e, E - 1), valid)


def _pos_embed_2d(n_tokens, dim):
    rows = n_tokens // GRID_W
    quarter = dim // 4
    freqs = jnp.exp(-math.log(10000.0) * jnp.arange(quarter, dtype=F32) / quarter)
    r = jnp.broadcast_to(jnp.arange(rows, dtype=F32)[:, None], (rows, GRID_W)).reshape(-1)
    col = jnp.broadcast_to(jnp.arange(GRID_W, dtype=F32)[None, :], (rows, GRID_W)).reshape(-1)
    ar = r[:, None] * freqs
    ac = col[:, None] * freqs
    return jnp.concatenate([jnp.sin(ar), jnp.cos(ar), jnp.sin(ac), jnp.cos(ac)], -1)


def _pad_cols(w, n):
    return jnp.pad(w, ((0, 0),) * (w.ndim - 1) + ((0, n - w.shape[-1]),))


def kernel(x_prompt, x_sample, state_lru, state_mlstm_c, state_mlstm_n, state_mlstm_m, state_gla, c, c_ctx, mod_w, mod_b, ln_g, ln_b, even_w_in, even_w_out, lru_conv_w, lru_conv_b, lru_gate_w, lru_gate_b, lru_lambda, mlstm_gate_b, mlstm_norm_g, odd_w_in, odd_w_out, gla_gate_w, gla_gate_b, gla_norm_g, router_w, router_b, moe_w_in, moe_w_out):
    Bp, Lp, D = x_prompt.shape
    Bs, Ls, _ = x_sample.shape
    depth = mod_w.shape[0]
    Tp, Ts = Bp * Lp, Bs * Ls
    T = Tp + Ts
    alpha = (2 * depth) ** 0.25
    D_A = lru_conv_w.shape[-1]
    D_B = mlstm_norm_g.shape[-1]
    DK = gla_gate_w.shape[-1]
    DV = gla_norm_g.shape[-1]
    tm = next(t for t in (512, 256, 128) if Tp % t == 0 and Ls % t == 0)
    tn = 1024

    n_cond = 1 + Bs
    R = -(-n_cond // SUBLANE) * SUBLANE
    cond = jnp.zeros((R, D), F32).at[0].set(c_ctx).at[1:n_cond].set(c)
    mod_all = _modulation(cond, mod_w, mod_b)
    mod4 = mod_all.reshape(depth, R, 6, D)
    tm_proj = next(t for t in (1024, 512, 256, 128) if Tp % t == 0 and Ls % t == 0)
    assert D == TOK_ROWS * LANE

    x = _embed(x_prompt.reshape(Tp, D), x_sample.reshape(Ts, D), _pos_embed_2d(Ls, D), tm)

    moe_w_in_b = moe_w_in.astype(BF16)
    moe_w_out_b = moe_w_out.astype(BF16)
    groups = ((0, Bp, Lp), (Tp, Bs, Ls))

    s_lru, s_c, s_n, s_m, s_gla = [], [], [], [], []
    for l in range(depth):
        j = l // 2
        mt = (mod4, lambda t, l=l: (lambda i: (l, jnp.where(i < Tp // t, 0, 1 + (i - Tp // t) // (Ls // t)), 0, 0)))
        if l % 2 == 0:
            w_in = even_w_in[j]
            n_main = 2 * D_A + 4 * D_B
            proj, gates = _proj(x, mt, w_in[:, :n_main].astype(BF16), _pad_cols(w_in[:, n_main:], LANE),
                                sh_row=0, sc_row=1, tm=tm_proj, tn=tn)
            gate_bias = _pad_cols(mlstm_gate_b[j].reshape(1, -1), LANE)
            ya, hs = [], []
            for gi, (row0, nseq, L) in enumerate(groups):
                if gi == 0:
                    lru0 = jnp.zeros((nseq, 2, D_A), F32)
                    c0 = jnp.zeros((nseq, 2, B_HEADS, D_B // B_HEADS, D_B // B_HEADS), F32)
                    n0 = jnp.zeros((nseq, 2, B_HEADS, D_B // B_HEADS), F32)
                    m0 = jnp.zeros((nseq, 2, B_HEADS), F32)
                else:
                    lru0, c0, n0, m0 = state_lru[:, j], state_mlstm_c[:, j], state_mlstm_n[:, j], state_mlstm_m[:, j]
                y_a, st = _lru(proj, row0, nseq, L, lru_conv_w[j], lru_conv_b[j], lru_gate_w[j], lru_gate_b[j],
                               lru_lambda[j], lru0)
                h, C, n, m = _mlstm(proj, gates, gate_bias, row0, nseq, L, 2 * D_A, c0, n0, m0)
                ya.append(y_a)
                hs.append(h)
                if gi == 0:
                    s_lru.append(st)
                    s_c.append(C)
                    s_n.append(n.reshape(nseq, 2, B_HEADS, -1))
                    s_m.append(m.reshape(nseq, 2, B_HEADS))
            h_fwd, h_bwd = zip(*hs)
            y_b = _mlstm_out(h_fwd, h_bwd, proj, 2 * D_A + 3 * D_B, mlstm_norm_g[j], tm)
            w_out = even_w_out[j].astype(BF16)
            x, hf, ids, gts = _outproj_ln([tuple(ya), y_b], [w_out[:D_A], w_out[D_A:]], x, mt, ln_g[l, 0],
                                          ln_b[l, 0], router_w, router_b, g_row=2, alpha=alpha, tm=tm)
        else:
            w_in = odd_w_in[j]
            n_main = 2 * DK + 2 * DV
            proj, lr = _proj(x, mt, w_in[:, :n_main].astype(BF16), _pad_cols(w_in[:, n_main:], LANE),
                             sh_row=0, sc_row=1, tm=tm_proj, tn=tn)
            gw_pad = jnp.zeros((2, LANE, DK), F32)
            for dd in range(2):
                gw_pad = gw_pad.at[dd, dd * GLA_RANK:(dd + 1) * GLA_RANK].set(gla_gate_w[j, dd])
            os_ = []
            for gi, (row0, nseq, L) in enumerate(groups):
                s0 = (jnp.zeros((nseq, 2, C_HEADS, DK // C_HEADS, DV // C_HEADS), F32) if gi == 0
                      else state_gla[:, j])
                o, S = _gla(proj, lr, gw_pad, gla_gate_b[j], row0, nseq, L, s0)
                os_.append(o)
                if gi == 0:
                    s_gla.append(S)
            o_fwd, o_bwd = zip(*os_)
            y = _gla_out(o_fwd, o_bwd, proj, 2 * DK + DV, gla_norm_g[j], tm)
            x, hf, ids, gts = _outproj_ln([y], [odd_w_out[j].astype(BF16)], x, mt, ln_g[l, 0], ln_b[l, 0],
                                          router_w, router_b, g_row=2, alpha=alpha, tm=tm)

        gts = _pad_cols(gts[:2].T, LANE)
        dest, src_tok, tile_e, tile_valid = _route_meta(ids[:2].T, MOE_TM, tm)
        ys = _expert_ffn(hf, src_tok, moe_w_in_b, moe_w_out_b, l, tile_e, tile_valid)
        x = _combine_ln(x, ys, dest, gts, mt, ln_g[l, 1], ln_b[l, 1], g_row=5, alpha=alpha,
                        split_rows=Tp if l == depth - 1 else None)

    y_prompt = x[0].reshape(Bp, Lp, D)
    y_sample = x[1].reshape(Bs, Ls, D)
    return (y_prompt, y_sample, jnp.stack(s_lru, 1), jnp.stack(s_c, 1), jnp.stack(s_n, 1),
            jnp.stack(s_m, 1), jnp.stack(s_gla, 1))
```
